```python
import math, functools
import jax, jax.numpy as jnp
from jax import lax
import numpy as np

D_MODEL = 2048
BATCH = 2
SEQ = 4096
DEPTH = 2
DEC_BATCH = 16
DEC_SEQ = 64
PAST_LEN = 2048

CHUNK = 64
PAST_CHUNKS = 8
PAST_WINDOW = PAST_CHUNKS * CHUNK
BAND = (PAST_CHUNKS + 1) * CHUNK
N_HEADS = 8
HEAD_DIM = 128
ATT_W = N_HEADS * HEAD_DIM
ATT_SCALE = HEAD_DIM ** -0.5
REL_CLIP = 256
SSM_W = 1024
SSM_GROUP = 16
SSM_GROUPS = SSM_W // SSM_GROUP
SSM_STATE = 64
D_FF = 5632
N_EXPERTS = 8
TOP_K = 2
D_FF_EXPERT = 5632
IN_COLS = 3 * ATT_W + SSM_W + 2 * D_MODEL
EPS = 1e-6
NEG_INF = -1e30

kernel_name = 'hybrid_s5_chunkband_adaln_moe_step'


def rms_norm(x, g):
    xf = x.astype(jnp.float32)
    y = xf * lax.rsqrt(jnp.mean(xf * xf, axis=-1, keepdims=True) + EPS)
    return (y * g.astype(jnp.float32)).astype(x.dtype)


def modulate(x, g, shift, scale):
    return rms_norm(x, g) * (1 + scale[:, None, :]) + shift[:, None, :]


def band_attention_prompt(q, k, v, rel_bias):
    bsz, L = q.shape[0], q.shape[1]
    nc = L // CHUNK
    pad = jnp.zeros((bsz, PAST_WINDOW, N_HEADS, HEAD_DIM), k.dtype)
    kp = jnp.concatenate([pad, k], axis=1).reshape(bsz, nc + PAST_CHUNKS, CHUNK, N_HEADS, HEAD_DIM)
    vp = jnp.concatenate([pad, v], axis=1).reshape(bsz, nc + PAST_CHUNKS, CHUNK, N_HEADS, HEAD_DIM)
    band_idx = jnp.arange(nc)[:, None] + jnp.arange(PAST_CHUNKS + 1)[None, :]
    kb = kp[:, band_idx].reshape(bsz, nc, BAND, N_HEADS, HEAD_DIM)
    vb = vp[:, band_idx].reshape(bsz, nc, BAND, N_HEADS, HEAD_DIM)
    qc = q.reshape(bsz, nc, CHUNK, N_HEADS, HEAD_DIM)
    s = jnp.einsum('bnqhd,bnkhd->bnhqk', qc, kb, preferred_element_type=jnp.float32) * ATT_SCALE
    rel = PAST_WINDOW + jnp.arange(CHUNK)[:, None] - jnp.arange(BAND)[None, :]
    bias = rel_bias[jnp.clip(rel, -REL_CLIP, REL_CLIP) + REL_CLIP].astype(jnp.float32)
    s = s + jnp.transpose(bias, (2, 0, 1))
    k_pos = (jnp.arange(nc)[:, None] - PAST_CHUNKS) * CHUNK + jnp.arange(BAND)[None, :]
    s = jnp.where((k_pos >= 0)[None, :, None, None, :], s, NEG_INF)
    p = jax.nn.softmax(s, axis=-1).astype(vb.dtype)
    o = jnp.einsum('bnhqk,bnkhd->bnqhd', p, vb).reshape(bsz, L, ATT_W)
    keep = min(PAST_WINDOW, L)
    return o, k[:, L - keep:], v[:, L - keep:]


def band_attention_sample(q, k, v, cache_k, cache_v, rel_bias):
    bsz, S = q.shape[0], q.shape[1]
    cl = cache_k.shape[1]
    kk = jnp.concatenate([cache_k.astype(k.dtype), k], axis=1)
    vv = jnp.concatenate([cache_v.astype(v.dtype), v], axis=1)
    s = jnp.einsum('bqhd,bkhd->bhqk', q, kk, preferred_element_type=jnp.float32) * ATT_SCALE
    rel = cl + jnp.arange(S)[:, None] - jnp.arange(cl + S)[None, :]
    bias = rel_bias[jnp.clip(rel, -REL_CLIP, REL_CLIP) + REL_CLIP].astype(jnp.float32)
    s = s + jnp.transpose(bias, (2, 0, 1))[None]
    p = jax.nn.softmax(s, axis=-1).astype(vv.dtype)
    o = jnp.einsum('bhqk,bkhd->bqhd', p, vv).reshape(bsz, S, ATT_W)
    return o, k, v


def s5_scan(u, h0_re, h0_im, a_re, a_im, log_step, b_re, b_im, c_re, c_im, d_skip):
    f32 = jnp.float32
    bsz, L = u.shape[0], u.shape[1]
    uf = u.astype(f32)
    ug = uf.reshape(bsz, L, SSM_GROUPS, SSM_GROUP)
    a_re = a_re.astype(f32)
    a_im = a_im.astype(f32)
    step = jnp.exp(log_step.astype(f32))[:, None]
    mag = jnp.exp(a_re * step)
    ang = a_im * step
    ab_re = mag * jnp.cos(ang)
    ab_im = mag * jnp.sin(ang)
    den = a_re * a_re + a_im * a_im
    n_re = ab_re - 1.0
    f_re = (n_re * a_re + ab_im * a_im) / den
    f_im = (ab_im * a_re - n_re * a_im) / den
    b_re = b_re.astype(f32)
    b_im = b_im.astype(f32)
    bb_re = f_re[..., None] * b_re - f_im[..., None] * b_im
    bb_im = f_re[..., None] * b_im + f_im[..., None] * b_re
    bu_re = jnp.einsum('blgc,gpc->blgp', ug, bb_re)
    bu_im = jnp.einsum('blgc,gpc->blgp', ug, bb_im)
    a_seq_re = jnp.broadcast_to(ab_re, bu_re.shape)
    a_seq_im = jnp.broadcast_to(ab_im, bu_re.shape)

    def combine(e1, e2):
        a1r, a1i, b1r, b1i = e1
        a2r, a2i, b2r, b2i = e2
        return (a2r * a1r - a2i * a1i,
                a2r * a1i + a2i * a1r,
                a2r * b1r - a2i * b1i + b2r,
                a2r * b1i + a2i * b1r + b2i)

    pr, pi, sr, si = lax.associative_scan(combine, (a_seq_re, a_seq_im, bu_re, bu_im), axis=1)
    h0r = h0_re.astype(f32)[:, None]
    h0i = h0_im.astype(f32)[:, None]
    x_re = pr * h0r - pi * h0i + sr
    x_im = pr * h0i + pi * h0r + si
    y = (jnp.einsum('blgp,gcp->blgc', x_re, c_re.astype(f32))
         - jnp.einsum('blgp,gcp->blgc', x_im, c_im.astype(f32)))
    y = y.reshape(bsz, L, SSM_W) + d_skip.astype(f32) * uf
    return y, x_re[:, -1], x_im[:, -1]


def dense_swiglu(h, w_gate, w_up, w_down):
    return (jax.nn.silu(h @ w_gate) * (h @ w_up)) @ w_down


def moe_swiglu(h, w_router, b_router, w_gate, w_up, w_down):
    logits = (h @ w_router).astype(jnp.float32) + b_router.astype(jnp.float32)
    top_v, top_i = lax.top_k(logits, TOP_K)
    top_w = jax.nn.softmax(top_v, axis=-1)
    gates = jnp.sum(jax.nn.one_hot(top_i, N_EXPERTS, dtype=jnp.float32) * top_w[..., None], axis=-2)
    out = jnp.zeros(h.shape, jnp.float32)
    for e in range(N_EXPERTS):
        he = (jax.nn.silu(h @ w_gate[e]) * (h @ w_up[e])) @ w_down[e]
        out = out + gates[..., e:e + 1] * he.astype(jnp.float32)
    return out.astype(h.dtype)


def trunk_layer(x, c, attend, h0_re, h0_im, p, ffn):
    bsz, L = x.shape[0], x.shape[1]
    ada = jax.nn.silu(c) @ p['w_ada'] + p['b_ada']
    sh1, sc1, g1, sh2, sc2, g2 = jnp.split(ada, 6, axis=-1)
    h = modulate(x, p['g_mix'], sh1, sc1)
    z = h @ p['w_in']
    cuts = [ATT_W, 2 * ATT_W, 3 * ATT_W, 3 * ATT_W + SSM_W, 3 * ATT_W + SSM_W + D_MODEL]
    q, k, v, u, ga, gs = jnp.split(z, cuts, axis=-1)
    q = q.reshape(bsz, L, N_HEADS, HEAD_DIM)
    k = k.reshape(bsz, L, N_HEADS, HEAD_DIM)
    v = v.reshape(bsz, L, N_HEADS, HEAD_DIM)
    o, k_state, v_state = attend(q, k, v)
    ys, s_re, s_im = s5_scan(u, h0_re, h0_im, p['a_re'], p['a_im'], p['log_step'],
                             p['b_re'], p['b_im'], p['c_re'], p['c_im'], p['d'])
    ys = jax.nn.gelu(ys.astype(x.dtype))
    ys = ys * jax.nn.sigmoid(ys @ p['w_glu'])
    merged = jax.nn.sigmoid(ga) * (o @ p['w_ba']) + jax.nn.sigmoid(gs) * (ys @ p['w_bs'])
    x = x + g1[:, None, :] * (merged @ p['w_out'])
    h2 = modulate(x, p['g_ffn'], sh2, sc2)
    x = x + g2[:, None, :] * ffn(h2)
    return x, k_state, v_state, s_re, s_im


def setup_inputs(seed: int = 0) -> dict:
    key = jax.random.key(seed)
    ks = jax.random.split(key, 40)
    it = iter(range(40))
    f32 = jnp.float32

    def nrm(shape, scale):
        return jax.random.normal(ks[next(it)], shape, f32) * scale

    n_dense = (DEPTH + 1) // 2
    n_moe = DEPTH // 2
    cache_len = min(PAST_WINDOW, PAST_LEN)
    d_inv = D_MODEL ** -0.5
    inputs = {}
    inputs['x_prompt'] = nrm((BATCH, SEQ, D_MODEL), 1.0)
    inputs['x_sample'] = nrm((DEC_BATCH, DEC_SEQ, D_MODEL), 1.0)
    inputs['cache_k'] = nrm((DEPTH, DEC_BATCH, cache_len, N_HEADS, HEAD_DIM), 1.0)
    inputs['cache_v'] = nrm((DEPTH, DEC_BATCH, cache_len, N_HEADS, HEAD_DIM), 1.0)
    inputs['state_ssm_re'] = nrm((DEPTH, DEC_BATCH, SSM_GROUPS, SSM_STATE), 0.5)
    inputs['state_ssm_im'] = nrm((DEPTH, DEC_BATCH, SSM_GROUPS, SSM_STATE), 0.5)
    inputs['c_prompt'] = nrm((BATCH, D_MODEL), 1.0)
    inputs['c_sample'] = nrm((DEC_BATCH, D_MODEL), 1.0)
    inputs['w_ada'] = nrm((DEPTH, D_MODEL, 6 * D_MODEL), 0.5 * d_inv)
    inputs['b_ada'] = nrm((DEPTH, 6 * D_MODEL), 0.02)
    inputs['g_norm_mix'] = 1.0 + nrm((DEPTH, D_MODEL), 0.02)
    inputs['g_norm_ffn'] = 1.0 + nrm((DEPTH, D_MODEL), 0.02)
    inputs['g_norm_final'] = 1.0 + nrm((D_MODEL,), 0.02)
    inputs['w_in'] = nrm((DEPTH, D_MODEL, IN_COLS), d_inv)
    inputs['rel_bias'] = nrm((DEPTH, 2 * REL_CLIP + 1, N_HEADS), 0.1)
    inputs['ssm_a_re'] = -0.5 + nrm((DEPTH, SSM_GROUPS, SSM_STATE), 0.01)
    a_im0 = jnp.pi * jnp.arange(SSM_STATE, dtype=f32)
    inputs['ssm_a_im'] = a_im0 + nrm((DEPTH, SSM_GROUPS, SSM_STATE), 0.01)
    inputs['ssm_log_step'] = jax.random.uniform(ks[next(it)], (DEPTH, SSM_GROUPS), f32,
                                                minval=math.log(1e-3), maxval=math.log(1e-1))
    b_scale = (2.0 * SSM_GROUP) ** -0.5
    c_scale = (2.0 * SSM_STATE) ** -0.5
    inputs['ssm_b_re'] = nrm((DEPTH, SSM_GROUPS, SSM_STATE, SSM_GROUP), b_scale)
    inputs['ssm_b_im'] = nrm((DEPTH, SSM_GROUPS, SSM_STATE, SSM_GROUP), b_scale)
    inputs['ssm_c_re'] = nrm((DEPTH, SSM_GROUPS, SSM_GROUP, SSM_STATE), c_scale)
    inputs['ssm_c_im'] = nrm((DEPTH, SSM_GROUPS, SSM_GROUP, SSM_STATE), c_scale)
    inputs['ssm_d'] = nrm((DEPTH, SSM_W), 1.0)
    inputs['w_ssm_glu'] = nrm((DEPTH, SSM_W, SSM_W), SSM_W ** -0.5)
    inputs['w_branch_attn'] = nrm((DEPTH, ATT_W, D_MODEL), ATT_W ** -0.5)
    inputs['w_branch_ssm'] = nrm((DEPTH, SSM_W, D_MODEL), SSM_W ** -0.5)
    inputs['w_out'] = nrm((DEPTH, D_MODEL, D_MODEL), d_inv)
    inputs['w_ffn_gate'] = nrm((n_dense, D_MODEL, D_FF), d_inv)
    inputs['w_ffn_up'] = nrm((n_dense, D_MODEL, D_FF), d_inv)
    inputs['w_ffn_down'] = nrm((n_dense, D_FF, D_MODEL), D_FF ** -0.5)
    inputs['w_router'] = nrm((n_moe, D_MODEL, N_EXPERTS), d_inv)
    inputs['b_router'] = nrm((n_moe, N_EXPERTS), 0.01)
    inputs['w_exp_gate'] = nrm((n_moe, N_EXPERTS, D_MODEL, D_FF_EXPERT), d_inv)
    inputs['w_exp_up'] = nrm((n_moe, N_EXPERTS, D_MODEL, D_FF_EXPERT), d_inv)
    inputs['w_exp_down'] = nrm((n_moe, N_EXPERTS, D_FF_EXPERT, D_MODEL), D_FF_EXPERT ** -0.5)
    return inputs


def reference(x_prompt, x_sample, cache_k, cache_v, state_ssm_re, state_ssm_im, c_prompt, c_sample,
              w_ada, b_ada, g_norm_mix, g_norm_ffn, g_norm_final, w_in, rel_bias,
              ssm_a_re, ssm_a_im, ssm_log_step, ssm_b_re, ssm_b_im, ssm_c_re, ssm_c_im, ssm_d,
              w_ssm_glu, w_branch_attn, w_branch_ssm, w_out,
              w_ffn_gate, w_ffn_up, w_ffn_down,
              w_router, b_router, w_exp_gate, w_exp_up, w_exp_down):
    y_p = x_prompt
    y_s = x_sample
    kp_l, vp_l, srp_l, sip_l = [], [], [], []
    ks_l, vs_l, srs_l, sis_l = [], [], [], []
    zeros_h0 = jnp.zeros((x_prompt.shape[0], SSM_GROUPS, SSM_STATE), jnp.float32)
    for l in range(DEPTH):
        p = {'w_ada': w_ada[l], 'b_ada': b_ada[l], 'g_mix': g_norm_mix[l], 'g_ffn': g_norm_ffn[l],
             'w_in': w_in[l], 'a_re': ssm_a_re[l], 'a_im': ssm_a_im[l], 'log_step': ssm_log_step[l],
             'b_re': ssm_b_re[l], 'b_im': ssm_b_im[l], 'c_re': ssm_c_re[l], 'c_im': ssm_c_im[l],
             'd': ssm_d[l], 'w_glu': w_ssm_glu[l], 'w_ba': w_branch_attn[l], 'w_bs': w_branch_ssm[l],
             'w_out': w_out[l]}
        j = l // 2
        if l % 2 == 0:
            ffn = functools.partial(dense_swiglu, w_gate=w_ffn_gate[j], w_up=w_ffn_up[j],
                                    w_down=w_ffn_down[j])
        else:
            ffn = functools.partial(moe_swiglu, w_router=w_router[j], b_router=b_router[j],
                                    w_gate=w_exp_gate[j], w_up=w_exp_up[j], w_down=w_exp_down[j])
        att_p = functools.partial(band_attention_prompt, rel_bias=rel_bias[l])
        att_s = functools.partial(band_attention_sample, cache_k=cache_k[l], cache_v=cache_v[l],
                                  rel_bias=rel_bias[l])
        y_p, kp, vp, srp, sip = trunk_layer(y_p, c_prompt, att_p, zeros_h0, zeros_h0, p, ffn)
        y_s, kn, vn, srs, sis = trunk_layer(y_s, c_sample, att_s, state_ssm_re[l], state_ssm_im[l], p, ffn)
        kp_l.append(kp); vp_l.append(vp); srp_l.append(srp); sip_l.append(sip)
        ks_l.append(kn); vs_l.append(vn); srs_l.append(srs); sis_l.append(sis)
    y_prompt = rms_norm(y_p, g_norm_final)
    y_sample = rms_norm(y_s, g_norm_final)
    return (y_prompt, y_sample,
            jnp.stack(kp_l), jnp.stack(vp_l), jnp.stack(srp_l), jnp.stack(sip_l),
            jnp.stack(ks_l), jnp.stack(vs_l), jnp.stack(srs_l), jnp.stack(sis_l))
```

```python
import functools
import math

import jax
import jax.numpy as jnp
from jax import lax
from jax.experimental import pallas as pl
from jax.experimental.pallas import tpu as pltpu

F32 = jnp.float32
BF16 = jnp.bfloat16

D_MODEL = 2048
DEPTH = 2
CHUNK = 64
PAST_CHUNKS = 8
PAST_WINDOW = PAST_CHUNKS * CHUNK
BAND = PAST_WINDOW + CHUNK
N_HEADS = 8
HEAD_DIM = 128
ATT_W = N_HEADS * HEAD_DIM
ATT_SCALE = HEAD_DIM ** -0.5
REL_CLIP = 256
SSM_W = 1024
SSM_GROUP = 16
SSM_GROUPS = SSM_W // SSM_GROUP
SSM_STATE = 64
N_EXPERTS = 8
IN_COLS = 3 * ATT_W + SSM_W + 2 * D_MODEL
EPS = 1e-6
NEG_INF = -1e30

LANES = 128
VMEM_LIMIT_BYTES = 56 * 1024 * 1024

SSM_T = 16
SSM_K = SSM_T * SSM_GROUP
ATT_GROUP = 4
ATT_ROWS = ATT_GROUP * CHUNK
ATT_KEYS = 3 * ATT_ROWS


def _cparams(*sem):
    return pltpu.CompilerParams(dimension_semantics=sem, vmem_limit_bytes=VMEM_LIMIT_BYTES)


def _ada_kernel(c_ref, w_ref, b_ref, o_ref):
    c = c_ref[...]
    a = (c * jax.nn.sigmoid(c)).astype(BF16)
    o_ref[...] = jnp.dot(a, w_ref[...].astype(BF16), preferred_element_type=F32) + b_ref[...]


def _ada(c_pad, w_ada, b_ada):
    rows = c_pad.shape[0]
    n = w_ada.shape[-1]
    tn = 1024
    return pl.pallas_call(
        _ada_kernel,
        grid=(DEPTH, n // tn),
        in_specs=[
            pl.BlockSpec((rows, D_MODEL), lambda l, j: (0, 0)),
            pl.BlockSpec((None, D_MODEL, tn), lambda l, j: (l, 0, j)),
            pl.BlockSpec((None, 1, tn), lambda l, j: (l, 0, j)),
        ],
        out_specs=pl.BlockSpec((None, rows, tn), lambda l, j: (l, 0, j)),
        out_shape=jax.ShapeDtypeStruct((DEPTH, rows, n), F32),
        compiler_params=_cparams("arbitrary", "arbitrary"),
        name="ada",
    )(c_pad, w_ada, b_ada.reshape(DEPTH, 1, n))


def _modulated(x_ref, g_ref, sh_ref, sc_ref):
    x = x_ref[...]
    tm = x.shape[0]
    y = x * lax.rsqrt(jnp.mean(x * x, axis=-1, keepdims=True) + EPS) * g_ref[...]
    y3 = y.reshape(tm // CHUNK, CHUNK, D_MODEL)
    h = y3 * (1.0 + sc_ref[...]) + sh_ref[...]
    return h.reshape(tm, D_MODEL)


def _norm_mod_kernel(x_ref, g_ref, sh_ref, sc_ref, h_ref):
    h_ref[...] = _modulated(x_ref, g_ref, sh_ref, sc_ref).astype(BF16)


def _split_bf16(a):
    hi = a.astype(BF16)
    lo = (a - hi.astype(F32)).astype(BF16)
    return hi, lo


def _norm_mod_router_kernel(x_ref, g_ref, sh_ref, sc_ref, wr_ref, br_ref, h_ref, gates_ref):
    h = _modulated(x_ref, g_ref, sh_ref, sc_ref)
    h_ref[...] = h.astype(BF16)
    h_hi, h_lo = _split_bf16(h)
    w_hi, w_lo = _split_bf16(wr_ref[...])
    dot = functools.partial(jnp.dot, preferred_element_type=F32)
    logits = (dot(h_hi, w_hi) + (dot(h_hi, w_lo) + dot(h_lo, w_hi)) + dot(h_lo, w_lo)) + br_ref[...]
    lane = lax.broadcasted_iota(jnp.int32, logits.shape, 1).astype(F32)
    lg = jnp.where(lane < N_EXPERTS, logits, -jnp.inf)
    m1 = jnp.max(lg, axis=-1, keepdims=True)
    i1 = jnp.min(jnp.where(lg == m1, lane, float(LANES)), axis=-1, keepdims=True)
    lg2 = jnp.where(lane == i1, -jnp.inf, lg)
    m2 = jnp.max(lg2, axis=-1, keepdims=True)
    i2 = jnp.min(jnp.where(lg2 == m2, lane, float(LANES)), axis=-1, keepdims=True)
    e2 = jnp.exp(m2 - m1)
    den = 1.0 + e2
    gates_ref[...] = jnp.where(lane == i1, 1.0 / den, 0.0) + jnp.where(lane == i2, e2 / den, 0.0)


def _norm_mod(x, g, sh_tab, sc_tab, router=None):
    nt = x.shape[0]
    tm = 512
    cpt = tm // CHUNK
    in_specs = [
        pl.BlockSpec((tm, D_MODEL), lambda i: (i, 0)),
        pl.BlockSpec((1, D_MODEL), lambda i: (0, 0)),
        pl.BlockSpec((cpt, 1, D_MODEL), lambda i: (i, 0, 0)),
        pl.BlockSpec((cpt, 1, D_MODEL), lambda i: (i, 0, 0)),
    ]
    h_spec = pl.BlockSpec((tm, D_MODEL), lambda i: (i, 0))
    h_shape = jax.ShapeDtypeStruct((nt, D_MODEL), BF16)
    args = [x, g.reshape(1, D_MODEL), sh_tab, sc_tab]
    if router is None:
        return pl.pallas_call(
            _norm_mod_kernel, grid=(nt // tm,), in_specs=in_specs, out_specs=h_spec, out_shape=h_shape,
            compiler_params=_cparams("arbitrary"), name="norm_mod",
        )(*args)
    w_r, b_r = router
    w_pad = jnp.zeros((D_MODEL, LANES), F32).at[:, :N_EXPERTS].set(w_r)
    b_pad = jnp.zeros((1, LANES), F32).at[0, :N_EXPERTS].set(b_r)
    in_specs += [
        pl.BlockSpec((D_MODEL, LANES), lambda i: (0, 0)),
        pl.BlockSpec((1, LANES), lambda i: (0, 0)),
    ]
    return pl.pallas_call(
        _norm_mod_router_kernel, grid=(nt // tm,), in_specs=in_specs,
        out_specs=[h_spec, pl.BlockSpec((tm, LANES), lambda i: (i, 0))],
        out_shape=[h_shape, jax.ShapeDtypeStruct((nt, LANES), F32)],
        compiler_params=_cparams("arbitrary"), name="norm_mod_router",
    )(*args, w_pad, b_pad)


def _mm_kernel(x_ref, w_ref, o_ref):
    o_ref[...] = jnp.dot(x_ref[...], w_ref[...].astype(BF16), preferred_element_type=F32).astype(o_ref.dtype)


def _in_proj(h, w):
    nt, k = h.shape
    n = w.shape[1]
    tm, tn = 1024, 512
    return pl.pallas_call(
        _mm_kernel,
        grid=(nt // tm, n // tn),
        in_specs=[
            pl.BlockSpec((tm, k), lambda i, j: (i, 0)),
            pl.BlockSpec((k, tn), lambda i, j: (0, j)),
        ],
        out_specs=pl.BlockSpec((tm, tn), lambda i, j: (i, j)),
        out_shape=jax.ShapeDtypeStruct((nt, n), F32),
        compiler_params=_cparams("arbitrary", "arbitrary"),
        name="in_proj",
    )(h, w)


def _softmax_rows(s):
    m = jnp.max(s, axis=-1, keepdims=True)
    e = jnp.exp(s - m)
    return e / jnp.sum(e, axis=-1, keepdims=True)


def _attn_prompt_kernel(q_ref, k0_ref, k1_ref, k2_ref, v0_ref, v1_ref, v2_ref, bias_ref, o_ref):
    g = pl.program_id(1)
    col = lax.broadcasted_iota(jnp.int32, (ATT_ROWS, ATT_KEYS), 1)
    visible = (g * ATT_ROWS - 2 * ATT_ROWS + col) >= 0
    for h in range(N_HEADS):
        sl = slice(h * HEAD_DIM, (h + 1) * HEAD_DIM)
        qh = q_ref[:, sl].astype(BF16)
        kh = jnp.concatenate([k0_ref[:, sl], k1_ref[:, sl], k2_ref[:, sl]], axis=0).astype(BF16)
        vh = jnp.concatenate([v0_ref[:, sl], v1_ref[:, sl], v2_ref[:, sl]], axis=0).astype(BF16)
        s = lax.dot_general(qh, kh, (((1,), (1,)), ((), ())), preferred_element_type=F32)
        s = s * ATT_SCALE + bias_ref[h]
        s = jnp.where(visible, s, NEG_INF)
        p = _softmax_rows(s).astype(BF16)
        o_ref[:, sl] = jnp.dot(p, vh, preferred_element_type=F32).astype(o_ref.dtype)


def _attn_prompt(z, bias, n_batch, seq):
    groups = seq // ATT_ROWS
    blk = (ATT_ROWS, ATT_W)

    def kv_spec(col, back):
        return pl.BlockSpec(blk, lambda b, g: (b * groups + jnp.maximum(g - back, 0), col))

    return pl.pallas_call(
        _attn_prompt_kernel,
        grid=(n_batch, groups),
        in_specs=[
            pl.BlockSpec(blk, lambda b, g: (b * groups + g, 0)),
            kv_spec(1, 2), kv_spec(1, 1), kv_spec(1, 0),
            kv_spec(2, 2), kv_spec(2, 1), kv_spec(2, 0),
            pl.BlockSpec((N_HEADS, ATT_ROWS, ATT_KEYS), lambda b, g: (0, 0, 0)),
        ],
        out_specs=pl.BlockSpec(blk, lambda b, g: (b * groups + g, 0)),
        out_shape=jax.ShapeDtypeStruct((n_batch * seq, ATT_W), BF16),
        compiler_params=_cparams("arbitrary", "arbitrary"),
        name="attn_prompt",
    )(z, z, z, z, z, z, z, bias)


def _attn_sample_kernel(q_ref, kn_ref, vn_ref, kc_ref, vc_ref, bias_ref, o_ref):
    for h in range(N_HEADS):
        sl = slice(h * HEAD_DIM, (h + 1) * HEAD_DIM)
        qh = q_ref[:, sl].astype(BF16)
        kh = jnp.concatenate([kc_ref[:, sl], kn_ref[:, sl]], axis=0).astype(BF16)
        vh = jnp.concatenate([vc_ref[:, sl], vn_ref[:, sl]], axis=0).astype(BF16)
        s = lax.dot_general(qh, kh, (((1,), (1,)), ((), ())), preferred_element_type=F32)
        s = s * ATT_SCALE + bias_ref[h]
        p = _softmax_rows(s).astype(BF16)
        o_ref[:, sl] = jnp.dot(p, vh, preferred_element_type=F32).astype(o_ref.dtype)


def _attn_sample(z, cache_k, cache_v, bias, row0, n_batch):
    base = row0 // CHUNK
    blk = (CHUNK, ATT_W)
    return pl.pallas_call(
        _attn_sample_kernel,
        grid=(n_batch,),
        in_specs=[
            pl.BlockSpec(blk, lambda b: (base + b, 0)),
            pl.BlockSpec(blk, lambda b: (base + b, 1)),
            pl.BlockSpec(blk, lambda b: (base + b, 2)),
            pl.BlockSpec((None, PAST_WINDOW, ATT_W), lambda b: (b, 0, 0)),
            pl.BlockSpec((None, PAST_WINDOW, ATT_W), lambda b: (b, 0, 0)),
            pl.BlockSpec((N_HEADS, CHUNK, BAND), lambda b: (0, 0, 0)),
        ],
        out_specs=pl.BlockSpec(blk, lambda b: (b, 0)),
        out_shape=jax.ShapeDtypeStruct((n_batch * CHUNK, ATT_W), BF16),
        compiler_params=_cparams("arbitrary"),
        name="attn_sample",
    )(z, z, z, cache_k, cache_v, bias)


def _attention_bias(rel_bias):
    rel = PAST_WINDOW + jnp.arange(CHUNK)[:, None] - jnp.arange(BAND)[None, :]
    b64 = rel_bias[jnp.clip(rel, -REL_CLIP, REL_CLIP) + REL_CLIP].astype(F32)
    b64 = jnp.transpose(b64, (2, 0, 1))
    rows = [jnp.pad(b64, ((0, 0), (0, 0), (c * CHUNK, ATT_KEYS - BAND - c * CHUNK)), constant_values=NEG_INF)
            for c in range(ATT_GROUP)]
    return b64, jnp.concatenate(rows, axis=1)


def _ssm_state_kernel(u_ref, wa_ref, wb_ref, sre_ref, sim_ref):
    s = (jnp.dot(u_ref[0], wa_ref[...], preferred_element_type=F32)
         + jnp.dot(u_ref[1], wb_ref[...], preferred_element_type=F32))
    sre_ref[...] = s[:, :LANES]
    sim_ref[...] = s[:, LANES:]


def _ssm_state(u1, wa, wb):
    n_sub = u1.shape[1]
    pairs = SSM_GROUPS // 2
    out = jax.ShapeDtypeStruct((n_sub, SSM_GROUPS * SSM_STATE), F32)
    return pl.pallas_call(
        _ssm_state_kernel,
        grid=(pairs,),
        in_specs=[
            pl.BlockSpec((2, n_sub, SSM_K), lambda j: (j, 0, 0)),
            pl.BlockSpec((None, SSM_K, 2 * LANES), lambda j: (j, 0, 0)),
            pl.BlockSpec((None, SSM_K, 2 * LANES), lambda j: (j, 0, 0)),
        ],
        out_specs=[pl.BlockSpec((n_sub, LANES), lambda j: (0, j))] * 2,
        out_shape=[out, out],
        compiler_params=_cparams("arbitrary"),
        name="ssm_state",
    )(u1, wa, wb)


def _ssm_carry_kernel(first_ref, seq_ref, sre_ref, sim_ref, ar_ref, ai_ref, h0r_ref, h0i_ref,
                      xsr_ref, xsi_ref, xer_ref, xei_ref, st_ref):
    blk = pl.program_id(0)
    steps = sre_ref.shape[0]

    @pl.when(blk == 0)
    def _():
        st_ref[...] = jnp.zeros_like(st_ref)

    ar = ar_ref[...]
    ai = ai_ref[...]

    def body(kk, carry):
        k = blk * steps + kk
        is_first = first_ref[k] == 1
        sq = seq_ref[k]
        xr = jnp.where(is_first, h0r_ref[sq], st_ref[0])
        xi = jnp.where(is_first, h0i_ref[sq], st_ref[1])
        xsr_ref[kk] = xr
        xsi_ref[kk] = xi
        nr = ar * xr - ai * xi + sre_ref[kk]
        ni = ar * xi + ai * xr + sim_ref[kk]
        xer_ref[kk] = nr
        xei_ref[kk] = ni
        st_ref[0] = nr
        st_ref[1] = ni
        return carry

    lax.fori_loop(0, steps, body, 0)


def _ssm_carry(first, seq_id, sre, sim, at_re, at_im, h0_re, h0_im):
    n_sub = sre.shape[0]
    steps = 64
    tile = (8, SSM_GROUPS * SSM_STATE // 8)
    blk = pl.BlockSpec((steps,) + tile, lambda i, *_: (i, 0, 0))
    const = pl.BlockSpec(tile, lambda i, *_: (0, 0))
    tab = pl.BlockSpec((h0_re.shape[0],) + tile, lambda i, *_: (0, 0, 0))
    out = jax.ShapeDtypeStruct((n_sub,) + tile, F32)
    return pl.pallas_call(
        _ssm_carry_kernel,
        grid_spec=pltpu.PrefetchScalarGridSpec(
            num_scalar_prefetch=2,
            grid=(n_sub // steps,),
            in_specs=[blk, blk, const, const, tab, tab],
            out_specs=[blk, blk, blk, blk],
            scratch_shapes=[pltpu.VMEM((2,) + tile, F32)],
        ),
        out_shape=[out, out, out, out],
        compiler_params=_cparams("arbitrary"),
        name="ssm_carry",
    )(first, seq_id, sre.reshape((n_sub,) + tile), sim.reshape((n_sub,) + tile),
      at_re.reshape(tile), at_im.reshape(tile),
      h0_re.reshape((-1,) + tile), h0_im.reshape((-1,) + tile))


def _ssm_out_kernel(u_ref, kt_ref, xr_ref, xi_ref, cr_ref, ci_ref, y_ref):
    y = jnp.dot(u_ref[...], kt_ref[...], preferred_element_type=F32)
    y += jnp.dot(xr_ref[...].astype(BF16), cr_ref[...], preferred_element_type=F32)
    y += jnp.dot(xi_ref[...].astype(BF16), ci_ref[...], preferred_element_type=F32)
    y_ref[...] = y


def _ssm_out(u1, kt, xs_re, xs_im, cp_re, cp_im):
    n_sub = u1.shape[1]
    return pl.pallas_call(
        _ssm_out_kernel,
        grid=(SSM_GROUPS,),
        in_specs=[
            pl.BlockSpec((None, n_sub, SSM_K), lambda g: (g, 0, 0)),
            pl.BlockSpec((None, SSM_K, SSM_K), lambda g: (g, 0, 0)),
            pl.BlockSpec((n_sub, LANES), lambda g: (0, g // 2)),
            pl.BlockSpec((n_sub, LANES), lambda g: (0, g // 2)),
            pl.BlockSpec((None, LANES, SSM_K), lambda g: (g, 0, 0)),
            pl.BlockSpec((None, LANES, SSM_K), lambda g: (g, 0, 0)),
        ],
        out_specs=pl.BlockSpec((None, n_sub, SSM_K), lambda g: (g, 0, 0)),
        out_shape=jax.ShapeDtypeStruct((SSM_GROUPS, n_sub, SSM_K), F32),
        compiler_params=_cparams("arbitrary"),
        name="ssm_out",
    )(u1, kt, xs_re, xs_im, cp_re, cp_im)


def _ssm_matrices(a_re, a_im, log_step, b_re, b_im, c_re, c_im):
    hp = lax.Precision.HIGHEST
    G, P, T = SSM_GROUPS, SSM_STATE, SSM_T
    step = jnp.exp(log_step.astype(F32))[:, None]
    mag = jnp.exp(a_re * step)
    ang = a_im * step
    ab_re = mag * jnp.cos(ang)
    ab_im = mag * jnp.sin(ang)
    den = a_re * a_re + a_im * a_im
    n_re = ab_re - 1.0
    f_re = (n_re * a_re + ab_im * a_im) / den
    f_im = (ab_im * a_re - n_re * a_im) / den
    bb_re = f_re[..., None] * b_re - f_im[..., None] * b_im
    bb_im = f_re[..., None] * b_im + f_im[..., None] * b_re
    pr, pi = [jnp.ones_like(ab_re)], [jnp.zeros_like(ab_re)]
    for _ in range(T):
        pr.append(pr[-1] * ab_re - pi[-1] * ab_im)
        pi.append(pr[-2] * ab_im + pi[-1] * ab_re)
    pw_re = jnp.stack(pr)
    pw_im = jnp.stack(pi)
    ab_r = pw_re[:T, :, :, None] * bb_re[None] - pw_im[:T, :, :, None] * bb_im[None]
    ab_i = pw_re[:T, :, :, None] * bb_im[None] + pw_im[:T, :, :, None] * bb_re[None]
    taps = (jnp.einsum('gcp,tgpd->tgcd', c_re, ab_r, precision=hp)
            - jnp.einsum('gcp,tgpd->tgcd', c_im, ab_i, precision=hp))
    lag = jnp.arange(T)[None, :] - jnp.arange(T)[:, None]
    kt = jnp.where((lag >= 0)[:, :, None, None, None], taps[jnp.clip(lag, 0, T - 1)], 0.0)
    kt = jnp.transpose(kt, (2, 0, 4, 1, 3)).reshape(G, SSM_K, SSM_K)
    m_re = jnp.transpose(ab_r[::-1], (1, 0, 3, 2)).reshape(G, SSM_K, P)
    m_im = jnp.transpose(ab_i[::-1], (1, 0, 3, 2)).reshape(G, SSM_K, P)
    zeros = jnp.zeros_like(m_re)
    wa = jnp.concatenate([m_re, zeros, m_im, zeros], axis=-1)[0::2]
    wb = jnp.concatenate([zeros, m_re, zeros, m_im], axis=-1)[1::2]
    ca_re = c_re[None] * jnp.transpose(pw_re[1:], (0, 1, 2))[:, :, None, :] \
        - c_im[None] * pw_im[1:][:, :, None, :]
    ca_im = c_re[None] * pw_im[1:][:, :, None, :] + c_im[None] * pw_re[1:][:, :, None, :]
    cp_re = jnp.transpose(ca_re, (1, 3, 0, 2)).reshape(G, P, SSM_K)
    cp_im = -jnp.transpose(ca_im, (1, 3, 0, 2)).reshape(G, P, SSM_K)
    zc = jnp.zeros_like(cp_re)
    even = (jnp.arange(G) % 2 == 0)[:, None, None]
    cp_re = jnp.where(even, jnp.concatenate([cp_re, zc], axis=1), jnp.concatenate([zc, cp_re], axis=1))
    cp_im = jnp.where(even, jnp.concatenate([cp_im, zc], axis=1), jnp.concatenate([zc, cp_im], axis=1))
    return dict(kt=kt.astype(BF16), wa=wa.astype(BF16), wb=wb.astype(BF16),
                cp_re=cp_re.astype(BF16), cp_im=cp_im.astype(BF16),
                at_re=pw_re[T], at_im=pw_im[T])


def _ssm(z, mats, first, seq_id, h0_re, h0_im):
    nt = z.shape[0]
    n_sub = nt // SSM_T
    u = z[:, 3 * ATT_W:3 * ATT_W + SSM_W].astype(BF16)
    u1 = jnp.transpose(u.reshape(n_sub, SSM_T, SSM_GROUPS, SSM_GROUP), (2, 0, 1, 3)).reshape(SSM_GROUPS, n_sub, SSM_K)
    sre, sim = _ssm_state(u1, mats['wa'], mats['wb'])
    xsr, xsi, xer, xei = _ssm_carry(first, seq_id, sre, sim, mats['at_re'], mats['at_im'], h0_re, h0_im)
    width = SSM_GROUPS * SSM_STATE
    y1 = _ssm_out(u1, mats['kt'], xsr.reshape(n_sub, width), xsi.reshape(n_sub, width), mats['cp_re'], mats['cp_im'])
    y = jnp.transpose(y1.reshape(SSM_GROUPS, n_sub, SSM_T, SSM_GROUP), (1, 2, 0, 3)).reshape(nt, SSM_W)
    return y, xer.reshape(n_sub, SSM_GROUPS, SSM_STATE), xei.reshape(n_sub, SSM_GROUPS, SSM_STATE)


def _gelu_tanh(x):
    return 0.5 * x * (1.0 + jnp.tanh(math.sqrt(2.0 / math.pi) * (x + 0.044715 * (x * x * x))))


def _glu_kernel(y_ref, u_ref, d_ref, w_ref, o_ref, wbf_ref):
    @pl.when(pl.program_id(0) == 0)
    def _():
        wbf_ref[...] = w_ref[...].astype(BF16)

    ys = _gelu_tanh(y_ref[...] + d_ref[...] * u_ref[...])
    t = jnp.dot(ys.astype(BF16), wbf_ref[...], preferred_element_type=F32)
    o_ref[...] = (ys * jax.nn.sigmoid(t)).astype(o_ref.dtype)


def _glu(y, z, d, w_glu):
    nt = y.shape[0]
    tm = 512
    return pl.pallas_call(
        _glu_kernel,
        grid=(nt // tm,),
        in_specs=[
            pl.BlockSpec((tm, SSM_W), lambda i: (i, 0)),
            pl.BlockSpec((tm, SSM_W), lambda i: (i, 3 * ATT_W // SSM_W)),
            pl.BlockSpec((1, SSM_W), lambda i: (0, 0)),
            pl.BlockSpec((SSM_W, SSM_W), lambda i: (0, 0)),
        ],
        out_specs=pl.BlockSpec((tm, SSM_W), lambda i: (i, 0)),
        out_shape=jax.ShapeDtypeStruct((nt, SSM_W), BF16),
        scratch_shapes=[pltpu.VMEM((SSM_W, SSM_W), BF16)],
        compiler_params=_cparams("arbitrary"),
        name="glu",
    )(y, z, d.reshape(1, SSM_W), w_glu)


def _merge_kernel(o_ref, y_ref, ga_ref, gs_ref, wa_ref, ws_ref, m_ref):
    a = jnp.dot(o_ref[...], wa_ref[...].astype(BF16), preferred_element_type=F32)
    s = jnp.dot(y_ref[...], ws_ref[...].astype(BF16), preferred_element_type=F32)
    m_ref[...] = (jax.nn.sigmoid(ga_ref[...]) * a + jax.nn.sigmoid(gs_ref[...]) * s).astype(m_ref.dtype)


def _merge(o, ys2, z, w_ba, w_bs):
    nt = o.shape[0]
    tm, tn = 1024, 512
    ga0 = (3 * ATT_W + SSM_W) // tn
    gs0 = ga0 + D_MODEL // tn
    return pl.pallas_call(
        _merge_kernel,
        grid=(nt // tm, D_MODEL // tn),
        in_specs=[
            pl.BlockSpec((tm, ATT_W), lambda i, j: (i, 0)),
            pl.BlockSpec((tm, SSM_W), lambda i, j: (i, 0)),
            pl.BlockSpec((tm, tn), lambda i, j: (i, ga0 + j)),
            pl.BlockSpec((tm, tn), lambda i, j: (i, gs0 + j)),
            pl.BlockSpec((ATT_W, tn), lambda i, j: (0, j)),
            pl.BlockSpec((SSM_W, tn), lambda i, j: (0, j)),
        ],
        out_specs=pl.BlockSpec((tm, tn), lambda i, j: (i, j)),
        out_shape=jax.ShapeDtypeStruct((nt, D_MODEL), BF16),
        compiler_params=_cparams("arbitrary", "arbitrary"),
        name="merge",
    )(o, ys2, z, z, w_ba, w_bs)


def _proj_residual_kernel(a_ref, w_ref, x_ref, g_ref, o_ref):
    r = jnp.dot(a_ref[...], w_ref[...].astype(BF16), preferred_element_type=F32)
    tm, tn = r.shape
    r3 = r.reshape(tm // CHUNK, CHUNK, tn) * g_ref[...]
    o_ref[...] = x_ref[...] + r3.reshape(tm, tn)


def _proj_residual(a, w, x, gate_tab):
    nt, k = a.shape
    tm, tn = 1024, 512
    cpt = tm // CHUNK
    return pl.pallas_call(
        _proj_residual_kernel,
        grid=(nt // tm, D_MODEL // tn),
        in_specs=[
            pl.BlockSpec((tm, k), lambda i, j: (i, 0)),
            pl.BlockSpec((k, tn), lambda i, j: (0, j)),
            pl.BlockSpec((tm, tn), lambda i, j: (i, j)),
            pl.BlockSpec((cpt, 1, tn), lambda i, j: (i, 0, j)),
        ],
        out_specs=pl.BlockSpec((tm, tn), lambda i, j: (i, j)),
        out_shape=jax.ShapeDtypeStruct((nt, D_MODEL), F32),
        compiler_params=_cparams("arbitrary", "arbitrary"),
        name="proj_residual",
    )(a, w, x, gate_tab)


def _ffn_kernel(h_ref, wg_ref, wu_ref, wd_ref, gates_ref, o_ref, *, gated):
    e = pl.program_id(1)
    f = pl.program_id(2)

    @pl.when((e == 0) & (f == 0))
    def _():
        o_ref[...] = jnp.zeros_like(o_ref)

    h = h_ref[...]
    hg = jnp.dot(h, wg_ref[...].astype(BF16), preferred_element_type=F32)
    hu = jnp.dot(h, wu_ref[...].astype(BF16), preferred_element_type=F32)
    a = (hg * jax.nn.sigmoid(hg)) * hu
    if gated:
        gates = gates_ref[...]
        lane = lax.broadcasted_iota(jnp.int32, gates.shape, 1)
        a = a * jnp.sum(jnp.where(lane == e, gates, 0.0), axis=-1, keepdims=True)
    o_ref[...] += jnp.dot(a.astype(BF16), wd_ref[...].astype(BF16), preferred_element_type=F32)


def _ffn(h, w_gate, w_up, w_down, gates):
    nt = h.shape[0]
    n_exp, _, d_ff = w_gate.shape
    tm, tf = 1024, 256
    gated = gates is not None
    if not gated:
        gates = jnp.ones((nt, LANES), F32)
    return pl.pallas_call(
        functools.partial(_ffn_kernel, gated=gated),
        grid=(nt // tm, n_exp, d_ff // tf),
        in_specs=[
            pl.BlockSpec((tm, D_MODEL), lambda i, e, f: (i, 0)),
            pl.BlockSpec((None, D_MODEL, tf), lambda i, e, f: (e, 0, f)),
            pl.BlockSpec((None, D_MODEL, tf), lambda i, e, f: (e, 0, f)),
            pl.BlockSpec((None, tf, D_MODEL), lambda i, e, f: (e, f, 0)),
            pl.BlockSpec((tm, LANES), lambda i, e, f: (i, 0)),
        ],
        out_specs=pl.BlockSpec((tm, D_MODEL), lambda i, e, f: (i, 0)),
        out_shape=jax.ShapeDtypeStruct((nt, D_MODEL), F32),
        compiler_params=_cparams("arbitrary", "arbitrary", "arbitrary"),
        name="ffn",
    )(h, w_gate, w_up, w_down, gates)


def _residual_kernel(x_ref, r_ref, g_ref, o_ref):
    r = r_ref[...]
    tm, n = r.shape
    o_ref[...] = x_ref[...] + (r.reshape(tm // CHUNK, CHUNK, n) * g_ref[...]).reshape(tm, n)


def _residual(x, r, gate_tab):
    nt = x.shape[0]
    tm = 512
    blk = pl.BlockSpec((tm, D_MODEL), lambda i: (i, 0))
    return pl.pallas_call(
        _residual_kernel, grid=(nt // tm,),
        in_specs=[blk, blk, pl.BlockSpec((tm // CHUNK, 1, D_MODEL), lambda i: (i, 0, 0))],
        out_specs=blk, out_shape=jax.ShapeDtypeStruct((nt, D_MODEL), F32),
        compiler_params=_cparams("arbitrary"), name="residual",
    )(x, r, gate_tab)


def _final_norm_kernel(x_ref, g_ref, o_ref):
    x = x_ref[...]
    o_ref[...] = x * lax.rsqrt(jnp.mean(x * x, axis=-1, keepdims=True) + EPS) * g_ref[...]


def _final_norm(x, g):
    nt = x.shape[0]
    tm = 512
    blk = pl.BlockSpec((tm, D_MODEL), lambda i: (i, 0))
    return pl.pallas_call(
        _final_norm_kernel, grid=(nt // tm,),
        in_specs=[blk, pl.BlockSpec((1, D_MODEL), lambda i: (0, 0))],
        out_specs=blk, out_shape=jax.ShapeDtypeStruct((nt, D_MODEL), F32),
        compiler_params=_cparams("arbitrary"), name="final_norm",
    )(x, g.reshape(1, D_MODEL))


def kernel(x_prompt, x_sample, cache_k, cache_v, state_ssm_re, state_ssm_im, c_prompt, c_sample, w_ada, b_ada, g_norm_mix, g_norm_ffn, g_norm_final, w_in, rel_bias, ssm_a_re, ssm_a_im, ssm_log_step, ssm_b_re, ssm_b_im, ssm_c_re, ssm_c_im, ssm_d, w_ssm_glu, w_branch_attn, w_branch_ssm, w_out, w_ffn_gate, w_ffn_up, w_ffn_down, w_router, b_router, w_exp_gate, w_exp_up, w_exp_down):
    n_pb, seq = x_prompt.shape[0], x_prompt.shape[1]
    n_sb, dec = x_sample.shape[0], x_sample.shape[1]
    assert dec == CHUNK and cache_k.shape[2] == PAST_WINDOW and seq % ATT_ROWS == 0
    np_tok = n_pb * seq
    nt = np_tok + n_sb * dec
    n_batch = n_pb + n_sb

    x = jnp.concatenate([x_prompt.reshape(np_tok, D_MODEL), x_sample.reshape(n_sb * dec, D_MODEL)], axis=0)

    c_all = jnp.concatenate([c_prompt, c_sample], axis=0)
    c_pad = jnp.pad(c_all, ((0, -n_batch % 8), (0, 0)))
    ada = _ada(c_pad, w_ada, b_ada)
    ada_p = jnp.broadcast_to(ada[:, :n_pb, None, :], (DEPTH, n_pb, seq // CHUNK, 6 * D_MODEL))
    ada_chunks = jnp.concatenate([ada_p.reshape(DEPTH, np_tok // CHUNK, 6 * D_MODEL), ada[:, n_pb:n_batch]], axis=1)
    ada_chunks = ada_chunks.reshape(DEPTH, nt // CHUNK, 6, 1, D_MODEL)

    sub_p, sub_s = seq // SSM_T, dec // SSM_T
    seq_id = jnp.concatenate([jnp.repeat(jnp.arange(n_pb), sub_p), n_pb + jnp.repeat(jnp.arange(n_sb), sub_s)])
    seq_id = seq_id.astype(jnp.int32)
    first = jnp.concatenate([jnp.arange(n_pb * sub_p) % sub_p == 0, jnp.arange(n_sb * sub_s) % sub_s == 0])
    first = first.astype(jnp.int32)
    last_p = (jnp.arange(n_pb) + 1) * sub_p - 1
    last_s = n_pb * sub_p + (jnp.arange(n_sb) + 1) * sub_s - 1

    outs = {k: [] for k in ('kp', 'vp', 'srp', 'sip', 'ks', 'vs', 'srs', 'sis')}
    for l in range(DEPTH):
        sh1, sc1, g1, sh2, sc2, g2 = (ada_chunks[l, :, i] for i in range(6))

        h = _norm_mod(x, g_norm_mix[l], sh1, sc1)
        z = _in_proj(h, w_in[l])

        bias_s, bias_p = _attention_bias(rel_bias[l])
        o_p = _attn_prompt(z, bias_p, n_pb, seq)
        o_s = _attn_sample(z, cache_k[l].reshape(n_sb, PAST_WINDOW, ATT_W),
                           cache_v[l].reshape(n_sb, PAST_WINDOW, ATT_W), bias_s, np_tok, n_sb)
        o = jnp.concatenate([o_p, o_s], axis=0)

        mats = _ssm_matrices(ssm_a_re[l], ssm_a_im[l], ssm_log_step[l], ssm_b_re[l], ssm_b_im[l],
                             ssm_c_re[l], ssm_c_im[l])
        zeros_h0 = jnp.zeros((n_pb, SSM_GROUPS, SSM_STATE), F32)
        h0_re = jnp.concatenate([zeros_h0, state_ssm_re[l].astype(F32)], axis=0)
        h0_im = jnp.concatenate([zeros_h0, state_ssm_im[l].astype(F32)], axis=0)
        y, xe_re, xe_im = _ssm(z, mats, first, seq_id, h0_re, h0_im)

        ys2 = _glu(y, z, ssm_d[l], w_ssm_glu[l])
        merged = _merge(o, ys2, z, w_branch_attn[l], w_branch_ssm[l])
        x = _proj_residual(merged, w_out[l], x, g1)

        j = l // 2
        if l % 2 == 0:
            h2 = _norm_mod(x, g_norm_ffn[l], sh2, sc2)
            r = _ffn(h2, w_ffn_gate[j][None], w_ffn_up[j][None], w_ffn_down[j][None], None)
        else:
            h2, gates = _norm_mod(x, g_norm_ffn[l], sh2, sc2, router=(w_router[j], b_router[j]))
            r = _ffn(h2, w_exp_gate[j], w_exp_up[j], w_exp_down[j], gates)
        x = _residual(x, r, g2)

        k_all = z[:, ATT_W:2 * ATT_W]
        v_all = z[:, 2 * ATT_W:3 * ATT_W]
        keep = min(PAST_WINDOW, seq)
        outs['kp'].append(k_all[:np_tok].reshape(n_pb, seq, N_HEADS, HEAD_DIM)[:, seq - keep:])
        outs['vp'].append(v_all[:np_tok].reshape(n_pb, seq, N_HEADS, HEAD_DIM)[:, seq - keep:])
        outs['ks'].append(k_all[np_tok:].reshape(n_sb, dec, N_HEADS, HEAD_DIM))
        outs['vs'].append(v_all[np_tok:].reshape(n_sb, dec, N_HEADS, HEAD_DIM))
        outs['srp'].append(xe_re[last_p])
        outs['sip'].append(xe_im[last_p])
        outs['srs'].append(xe_re[last_s])
        outs['sis'].append(xe_im[last_s])

    yn = _final_norm(x, g_norm_final)
    y_prompt = yn[:np_tok].reshape(n_pb, seq, D_MODEL)
    y_sample = yn[np_tok:].reshape(n_sb, dec, D_MODEL)
    st = lambda name: jnp.stack(outs[name])
    return (y_prompt, y_sample, st('kp'), st('vp'), st('srp'), st('sip'),
            st('ks'), st('vs'), st('srs'), st('sis'))
```

```python
import functools
import math

import jax
import jax.numpy as jnp
from jax import lax
from jax.experimental import pallas as pl
from jax.experimental.pallas import tpu as pltpu

F32 = jnp.float32
BF16 = jnp.bfloat16

D_MODEL = 2048
DEPTH = 2
CHUNK = 64
PAST_CHUNKS = 8
PAST_WINDOW = PAST_CHUNKS * CHUNK
BAND = PAST_WINDOW + CHUNK
N_HEADS = 8
HEAD_DIM = 128
ATT_W = N_HEADS * HEAD_DIM
ATT_SCALE = HEAD_DIM ** -0.5
REL_CLIP = 256
SSM_W = 1024
SSM_GROUP = 16
SSM_GROUPS = SSM_W // SSM_GROUP
SSM_STATE = 64
N_EXPERTS = 8
IN_COLS = 3 * ATT_W + SSM_W + 2 * D_MODEL
EPS = 1e-6
NEG_INF = -1e30

LANES = 128
VMEM_LIMIT_BYTES = 56 * 1024 * 1024

SSM_T = 16
SSM_K = SSM_T * SSM_GROUP
ATT_GROUP = 4
ATT_ROWS = ATT_GROUP * CHUNK
ATT_KEYS = 3 * ATT_ROWS

FFN_TM = 768
FFN_TF = 512
GATHER_ROWS = 256

SHIFT1, SCALE1, GATE1, SHIFT2, SCALE2, GATE2 = range(6)


def _cparams(*sem):
    return pltpu.CompilerParams(dimension_semantics=sem, vmem_limit_bytes=VMEM_LIMIT_BYTES)


def _tab_spec(cpt, l, which, width=D_MODEL, col=None):
    if col is None:
        return pl.BlockSpec((None, cpt, None, 1, width), lambda i, *_: (l, i, which, 0, 0))
    return pl.BlockSpec((None, cpt, None, 1, width), lambda i, j, *_: (l, i, which, 0, j))


def _ada_kernel(c_ref, w_ref, b_ref, o_ref):
    c = c_ref[...]
    a = (c * jax.nn.sigmoid(c)).astype(BF16)
    o_ref[...] = jnp.dot(a, w_ref[...].astype(BF16), preferred_element_type=F32) + b_ref[...]


def _ada(c_pad, w_ada, b_ada):
    rows = c_pad.shape[0]
    n = w_ada.shape[-1]
    tn = 1024
    return pl.pallas_call(
        _ada_kernel,
        grid=(DEPTH, n // tn),
        in_specs=[
            pl.BlockSpec((rows, D_MODEL), lambda l, j: (0, 0)),
            pl.BlockSpec((None, D_MODEL, tn), lambda l, j: (l, 0, j)),
            pl.BlockSpec((None, 1, tn), lambda l, j: (l, 0, j)),
        ],
        out_specs=pl.BlockSpec((None, rows, tn), lambda l, j: (l, 0, j)),
        out_shape=jax.ShapeDtypeStruct((DEPTH, rows, n), F32),
        compiler_params=_cparams("arbitrary", "arbitrary"),
        name="ada",
    )(c_pad, w_ada, b_ada.reshape(DEPTH, 1, n))


def _modulated(x_ref, g_ref, sh_ref, sc_ref):
    x = x_ref[...]
    tm = x.shape[0]
    y = x * lax.rsqrt(jnp.mean(x * x, axis=-1, keepdims=True) + EPS) * g_ref[...]
    y3 = y.reshape(tm // CHUNK, CHUNK, D_MODEL)
    h = y3 * (1.0 + sc_ref[...]) + sh_ref[...]
    return h.reshape(tm, D_MODEL)


def _norm_mod_kernel(x_ref, g_ref, sh_ref, sc_ref, h_ref):
    h_ref[...] = _modulated(x_ref, g_ref, sh_ref, sc_ref).astype(BF16)


def _split_bf16(a):
    hi = a.astype(BF16)
    lo = (a - hi.astype(F32)).astype(BF16)
    return hi, lo


def _norm_mod_router_kernel(x_ref, g_ref, sh_ref, sc_ref, wr_ref, br_ref, h_ref, route_ref):
    h = _modulated(x_ref, g_ref, sh_ref, sc_ref)
    h_ref[...] = h
    h_hi, h_lo = _split_bf16(h)
    w_hi, w_lo = _split_bf16(wr_ref[...])
    dot = functools.partial(jnp.dot, preferred_element_type=F32)
    logits = (dot(h_hi, w_hi) + (dot(h_hi, w_lo) + dot(h_lo, w_hi)) + dot(h_lo, w_lo)) + br_ref[...]
    lane = lax.broadcasted_iota(jnp.int32, logits.shape, 1).astype(F32)
    lg = jnp.where(lane < N_EXPERTS, logits, -jnp.inf)
    m1 = jnp.max(lg, axis=-1, keepdims=True)
    i1 = jnp.min(jnp.where(lg == m1, lane, float(LANES)), axis=-1, keepdims=True)
    lg2 = jnp.where(lane == i1, -jnp.inf, lg)
    m2 = jnp.max(lg2, axis=-1, keepdims=True)
    i2 = jnp.min(jnp.where(lg2 == m2, lane, float(LANES)), axis=-1, keepdims=True)
    e2 = jnp.exp(m2 - m1)
    den = 1.0 + e2
    route = jnp.where(lane == 0.0, i1, jnp.where(lane == 1.0, i2, jnp.where(lane == 2.0, 1.0 / den, e2 / den)))
    route_ref[...] = jnp.where(lane < 4.0, route, 0.0)


def _norm_mod(x, g_all, tab, l, which_shift, which_scale, router=None):
    nt = x.shape[0]
    tm = 512
    cpt = tm // CHUNK
    in_specs = [
        pl.BlockSpec((tm, D_MODEL), lambda i: (i, 0)),
        pl.BlockSpec((None, 1, D_MODEL), lambda i: (l, 0, 0)),
        _tab_spec(cpt, l, which_shift),
        _tab_spec(cpt, l, which_scale),
    ]
    h_spec = pl.BlockSpec((tm, D_MODEL), lambda i: (i, 0))
    args = [x, g_all.reshape(DEPTH, 1, D_MODEL), tab, tab]
    if router is None:
        return pl.pallas_call(
            _norm_mod_kernel, grid=(nt // tm,), in_specs=in_specs, out_specs=h_spec,
            out_shape=jax.ShapeDtypeStruct((nt, D_MODEL), BF16),
            compiler_params=_cparams("arbitrary"), name="norm_mod",
        )(*args)
    w_r, b_r = router
    w_pad = jnp.zeros((D_MODEL, LANES), F32).at[:, :N_EXPERTS].set(w_r)
    b_pad = jnp.zeros((1, LANES), F32).at[0, :N_EXPERTS].set(b_r)
    in_specs += [
        pl.BlockSpec((D_MODEL, LANES), lambda i: (0, 0)),
        pl.BlockSpec((1, LANES), lambda i: (0, 0)),
    ]
    return pl.pallas_call(
        _norm_mod_router_kernel, grid=(nt // tm,), in_specs=in_specs,
        out_specs=[h_spec, pl.BlockSpec((tm, LANES), lambda i: (i, 0))],
        out_shape=[jax.ShapeDtypeStruct((nt, D_MODEL), F32), jax.ShapeDtypeStruct((nt, LANES), F32)],
        compiler_params=_cparams("arbitrary"), name="norm_mod_router",
    )(*args, w_pad, b_pad)


def _mm_kernel(x_ref, w_ref, o_ref):
    o_ref[...] = jnp.dot(x_ref[...], w_ref[...].astype(BF16), preferred_element_type=F32).astype(o_ref.dtype)


def _in_proj(h, w_all, l):
    nt, k = h.shape
    n = w_all.shape[-1]
    tm, tn = 1024, 512
    return pl.pallas_call(
        _mm_kernel,
        grid=(nt // tm, n // tn),
        in_specs=[
            pl.BlockSpec((tm, k), lambda i, j: (i, 0)),
            pl.BlockSpec((None, k, tn), lambda i, j: (l, 0, j)),
        ],
        out_specs=pl.BlockSpec((tm, tn), lambda i, j: (i, j)),
        out_shape=jax.ShapeDtypeStruct((nt, n), F32),
        compiler_params=_cparams("arbitrary", "arbitrary"),
        name="in_proj",
    )(h, w_all)


def _softmax_rows(s):
    m = jnp.max(s, axis=-1, keepdims=True)
    e = jnp.exp(s - m)
    return e / jnp.sum(e, axis=-1, keepdims=True)


def _attn_prompt_kernel(q_ref, k0_ref, k1_ref, k2_ref, v0_ref, v1_ref, v2_ref, bias_ref, o_ref):
    g = pl.program_id(1)
    col = lax.broadcasted_iota(jnp.int32, (ATT_ROWS, ATT_KEYS), 1)
    visible = (g * ATT_ROWS - 2 * ATT_ROWS + col) >= 0
    for h in range(N_HEADS):
        sl = slice(h * HEAD_DIM, (h + 1) * HEAD_DIM)
        qh = q_ref[:, sl].astype(BF16)
        kh = jnp.concatenate([k0_ref[:, sl], k1_ref[:, sl], k2_ref[:, sl]], axis=0).astype(BF16)
        vh = jnp.concatenate([v0_ref[:, sl], v1_ref[:, sl], v2_ref[:, sl]], axis=0).astype(BF16)
        s = lax.dot_general(qh, kh, (((1,), (1,)), ((), ())), preferred_element_type=F32)
        s = s * ATT_SCALE + bias_ref[h]
        s = jnp.where(visible, s, NEG_INF)
        p = _softmax_rows(s).astype(BF16)
        o_ref[:, sl] = jnp.dot(p, vh, preferred_element_type=F32).astype(o_ref.dtype)


def _attn_prompt(z, bias, n_batch, seq):
    groups = seq // ATT_ROWS
    blk = (ATT_ROWS, ATT_W)

    def kv_spec(col, back):
        return pl.BlockSpec(blk, lambda b, g: (b * groups + jnp.maximum(g - back, 0), col))

    return pl.pallas_call(
        _attn_prompt_kernel,
        grid=(n_batch, groups),
        in_specs=[
            pl.BlockSpec(blk, lambda b, g: (b * groups + g, 0)),
            kv_spec(1, 2), kv_spec(1, 1), kv_spec(1, 0),
            kv_spec(2, 2), kv_spec(2, 1), kv_spec(2, 0),
            pl.BlockSpec((N_HEADS, ATT_ROWS, ATT_KEYS), lambda b, g: (0, 0, 0)),
        ],
        out_specs=pl.BlockSpec(blk, lambda b, g: (b * groups + g, 0)),
        out_shape=jax.ShapeDtypeStruct((n_batch * seq, ATT_W), BF16),
        compiler_params=_cparams("arbitrary", "arbitrary"),
        name="attn_prompt",
    )(z, z, z, z, z, z, z, bias)


def _attn_sample_kernel(q_ref, kn_ref, vn_ref, kc_ref, vc_ref, bias_ref, o_ref):
    for h in range(N_HEADS):
        sl = slice(h * HEAD_DIM, (h + 1) * HEAD_DIM)
        qh = q_ref[:, sl].astype(BF16)
        kh = jnp.concatenate([kc_ref[:, sl], kn_ref[:, sl]], axis=0).astype(BF16)
        vh = jnp.concatenate([vc_ref[:, sl], vn_ref[:, sl]], axis=0).astype(BF16)
        s = lax.dot_general(qh, kh, (((1,), (1,)), ((), ())), preferred_element_type=F32)
        s = s * ATT_SCALE + bias_ref[h]
        p = _softmax_rows(s).astype(BF16)
        o_ref[:, sl] = jnp.dot(p, vh, preferred_element_type=F32).astype(o_ref.dtype)


def _attn_sample(z, cache_k, cache_v, l, bias, row0, n_batch):
    base = row0 // CHUNK
    blk = (CHUNK, ATT_W)
    cache_spec = pl.BlockSpec((None, None, PAST_WINDOW, ATT_W), lambda b: (l, b, 0, 0))
    return pl.pallas_call(
        _attn_sample_kernel,
        grid=(n_batch,),
        in_specs=[
            pl.BlockSpec(blk, lambda b: (base + b, 0)),
            pl.BlockSpec(blk, lambda b: (base + b, 1)),
            pl.BlockSpec(blk, lambda b: (base + b, 2)),
            cache_spec, cache_spec,
            pl.BlockSpec((N_HEADS, CHUNK, BAND), lambda b: (0, 0, 0)),
        ],
        out_specs=pl.BlockSpec(blk, lambda b: (b, 0)),
        out_shape=jax.ShapeDtypeStruct((n_batch * CHUNK, ATT_W), BF16),
        compiler_params=_cparams("arbitrary"),
        name="attn_sample",
    )(z, z, z, cache_k, cache_v, bias)


def _attention_bias(rel_bias):
    rel = PAST_WINDOW + jnp.arange(CHUNK)[:, None] - jnp.arange(BAND)[None, :]
    b64 = rel_bias[jnp.clip(rel, -REL_CLIP, REL_CLIP) + REL_CLIP].astype(F32)
    b64 = jnp.transpose(b64, (2, 0, 1))
    rows = [jnp.pad(b64, ((0, 0), (0, 0), (c * CHUNK, ATT_KEYS - BAND - c * CHUNK)), constant_values=NEG_INF)
            for c in range(ATT_GROUP)]
    return b64, jnp.concatenate(rows, axis=1)


def _ssm_state_kernel(u_ref, wa_ref, wb_ref, sre_ref, sim_ref):
    s = (jnp.dot(u_ref[0], wa_ref[...], preferred_element_type=F32)
         + jnp.dot(u_ref[1], wb_ref[...], preferred_element_type=F32))
    sre_ref[...] = s[:, :LANES]
    sim_ref[...] = s[:, LANES:]


def _ssm_state(u1, wa, wb):
    n_sub = u1.shape[1]
    pairs = SSM_GROUPS // 2
    out = jax.ShapeDtypeStruct((n_sub, SSM_GROUPS * SSM_STATE), F32)
    return pl.pallas_call(
        _ssm_state_kernel,
        grid=(pairs,),
        in_specs=[
            pl.BlockSpec((2, n_sub, SSM_K), lambda j: (j, 0, 0)),
            pl.BlockSpec((None, SSM_K, 2 * LANES), lambda j: (j, 0, 0)),
            pl.BlockSpec((None, SSM_K, 2 * LANES), lambda j: (j, 0, 0)),
        ],
        out_specs=[pl.BlockSpec((n_sub, LANES), lambda j: (0, j))] * 2,
        out_shape=[out, out],
        compiler_params=_cparams("arbitrary"),
        name="ssm_state",
    )(u1, wa, wb)


def _ssm_carry_kernel(first_ref, seq_ref, sre_ref, sim_ref, ar_ref, ai_ref, h0r_ref, h0i_ref,
                      xsr_ref, xsi_ref, xer_ref, xei_ref, st_ref):
    blk = pl.program_id(0)
    steps = sre_ref.shape[0]

    @pl.when(blk == 0)
    def _():
        st_ref[...] = jnp.zeros_like(st_ref)

    ar = ar_ref[...]
    ai = ai_ref[...]

    def body(kk, carry):
        k = blk * steps + kk
        is_first = first_ref[k] == 1
        sq = seq_ref[k]
        xr = jnp.where(is_first, h0r_ref[sq], st_ref[0])
        xi = jnp.where(is_first, h0i_ref[sq], st_ref[1])
        xsr_ref[kk] = xr
        xsi_ref[kk] = xi
        nr = ar * xr - ai * xi + sre_ref[kk]
        ni = ar * xi + ai * xr + sim_ref[kk]
        xer_ref[kk] = nr
        xei_ref[kk] = ni
        st_ref[0] = nr
        st_ref[1] = ni
        return carry

    lax.fori_loop(0, steps, body, 0)


def _ssm_carry(first, seq_id, sre, sim, at_re, at_im, h0_re, h0_im):
    n_sub = sre.shape[0]
    steps = 64
    tile = (8, SSM_GROUPS * SSM_STATE // 8)
    blk = pl.BlockSpec((steps,) + tile, lambda i, *_: (i, 0, 0))
    const = pl.BlockSpec(tile, lambda i, *_: (0, 0))
    tab = pl.BlockSpec((h0_re.shape[0],) + tile, lambda i, *_: (0, 0, 0))
    out = jax.ShapeDtypeStruct((n_sub,) + tile, F32)
    return pl.pallas_call(
        _ssm_carry_kernel,
        grid_spec=pltpu.PrefetchScalarGridSpec(
            num_scalar_prefetch=2,
            grid=(n_sub // steps,),
            in_specs=[blk, blk, const, const, tab, tab],
            out_specs=[blk, blk, blk, blk],
            scratch_shapes=[pltpu.VMEM((2,) + tile, F32)],
        ),
        out_shape=[out, out, out, out],
        compiler_params=_cparams("arbitrary"),
        name="ssm_carry",
    )(first, seq_id, sre.reshape((n_sub,) + tile), sim.reshape((n_sub,) + tile),
      at_re.reshape(tile), at_im.reshape(tile),
      h0_re.reshape((-1,) + tile), h0_im.reshape((-1,) + tile))


def _ssm_out_kernel(u_ref, kt_ref, xr_ref, xi_ref, cr_ref, ci_ref, y_ref):
    y = jnp.dot(u_ref[...], kt_ref[...], preferred_element_type=F32)
    y += jnp.dot(xr_ref[...].astype(BF16), cr_ref[...], preferred_element_type=F32)
    y += jnp.dot(xi_ref[...].astype(BF16), ci_ref[...], preferred_element_type=F32)
    y_ref[...] = y


def _ssm_out(u1, kt, xs_re, xs_im, cp_re, cp_im):
    n_sub = u1.shape[1]
    return pl.pallas_call(
        _ssm_out_kernel,
        grid=(SSM_GROUPS,),
        in_specs=[
            pl.BlockSpec((None, n_sub, SSM_K), lambda g: (g, 0, 0)),
            pl.BlockSpec((None, SSM_K, SSM_K), lambda g: (g, 0, 0)),
            pl.BlockSpec((n_sub, LANES), lambda g: (0, g // 2)),
            pl.BlockSpec((n_sub, LANES), lambda g: (0, g // 2)),
            pl.BlockSpec((None, LANES, SSM_K), lambda g: (g, 0, 0)),
            pl.BlockSpec((None, LANES, SSM_K), lambda g: (g, 0, 0)),
        ],
        out_specs=pl.BlockSpec((None, n_sub, SSM_K), lambda g: (g, 0, 0)),
        out_shape=jax.ShapeDtypeStruct((SSM_GROUPS, n_sub, SSM_K), F32),
        compiler_params=_cparams("arbitrary"),
        name="ssm_out",
    )(u1, kt, xs_re, xs_im, cp_re, cp_im)


def _ssm_matrices(a_re, a_im, log_step, b_re, b_im, c_re, c_im):
    hp = lax.Precision.HIGHEST
    G, P, T = SSM_GROUPS, SSM_STATE, SSM_T
    step = jnp.exp(log_step.astype(F32))[:, None]
    mag = jnp.exp(a_re * step)
    ang = a_im * step
    ab_re = mag * jnp.cos(ang)
    ab_im = mag * jnp.sin(ang)
    den = a_re * a_re + a_im * a_im
    n_re = ab_re - 1.0
    f_re = (n_re * a_re + ab_im * a_im) / den
    f_im = (ab_im * a_re - n_re * a_im) / den
    bb_re = f_re[..., None] * b_re - f_im[..., None] * b_im
    bb_im = f_re[..., None] * b_im + f_im[..., None] * b_re
    pr, pi = [jnp.ones_like(ab_re)], [jnp.zeros_like(ab_re)]
    for _ in range(T):
        pr.append(pr[-1] * ab_re - pi[-1] * ab_im)
        pi.append(pr[-2] * ab_im + pi[-1] * ab_re)
    pw_re = jnp.stack(pr)
    pw_im = jnp.stack(pi)
    ab_r = pw_re[:T, :, :, None] * bb_re[None] - pw_im[:T, :, :, None] * bb_im[None]
    ab_i = pw_re[:T, :, :, None] * bb_im[None] + pw_im[:T, :, :, None] * bb_re[None]
    taps = (jnp.einsum('gcp,tgpd->tgcd', c_re, ab_r, precision=hp)
            - jnp.einsum('gcp,tgpd->tgcd', c_im, ab_i, precision=hp))
    zero_tap = jnp.zeros_like(taps[0])
    kt = jnp.stack([jnp.stack([taps[t - s] if t >= s else zero_tap for t in range(T)]) for s in range(T)])
    kt = jnp.transpose(kt, (2, 0, 4, 1, 3)).reshape(G, SSM_K, SSM_K)
    m_re = jnp.transpose(ab_r[::-1], (1, 0, 3, 2)).reshape(G, SSM_K, P)
    m_im = jnp.transpose(ab_i[::-1], (1, 0, 3, 2)).reshape(G, SSM_K, P)
    zeros = jnp.zeros_like(m_re)
    wa = jnp.concatenate([m_re, zeros, m_im, zeros], axis=-1)[0::2]
    wb = jnp.concatenate([zeros, m_re, zeros, m_im], axis=-1)[1::2]
    ca_re = c_re[None] * pw_re[1:][:, :, None, :] - c_im[None] * pw_im[1:][:, :, None, :]
    ca_im = c_re[None] * pw_im[1:][:, :, None, :] + c_im[None] * pw_re[1:][:, :, None, :]
    cp_re = jnp.transpose(ca_re, (1, 3, 0, 2)).reshape(G, P, SSM_K)
    cp_im = -jnp.transpose(ca_im, (1, 3, 0, 2)).reshape(G, P, SSM_K)
    zc = jnp.zeros_like(cp_re)
    even = (jnp.arange(G) % 2 == 0)[:, None, None]
    cp_re = jnp.where(even, jnp.concatenate([cp_re, zc], axis=1), jnp.concatenate([zc, cp_re], axis=1))
    cp_im = jnp.where(even, jnp.concatenate([cp_im, zc], axis=1), jnp.concatenate([zc, cp_im], axis=1))
    return dict(kt=kt.astype(BF16), wa=wa.astype(BF16), wb=wb.astype(BF16),
                cp_re=cp_re.astype(BF16), cp_im=cp_im.astype(BF16),
                at_re=pw_re[T], at_im=pw_im[T])


def _ssm(z, mats, first, seq_id, h0_re, h0_im):
    nt = z.shape[0]
    n_sub = nt // SSM_T
    u = z[:, 3 * ATT_W:3 * ATT_W + SSM_W].astype(BF16)
    u1 = jnp.transpose(u.reshape(n_sub, SSM_T, SSM_GROUPS, SSM_GROUP), (2, 0, 1, 3)).reshape(SSM_GROUPS, n_sub, SSM_K)
    sre, sim = _ssm_state(u1, mats['wa'], mats['wb'])
    xsr, xsi, xer, xei = _ssm_carry(first, seq_id, sre, sim, mats['at_re'], mats['at_im'], h0_re, h0_im)
    width = SSM_GROUPS * SSM_STATE
    y1 = _ssm_out(u1, mats['kt'], xsr.reshape(n_sub, width), xsi.reshape(n_sub, width), mats['cp_re'], mats['cp_im'])
    y = jnp.transpose(y1.reshape(SSM_GROUPS, n_sub, SSM_T, SSM_GROUP), (1, 2, 0, 3)).reshape(nt, SSM_W)
    return y, xer.reshape(n_sub, SSM_GROUPS, SSM_STATE), xei.reshape(n_sub, SSM_GROUPS, SSM_STATE)


def _gelu_tanh(x):
    return 0.5 * x * (1.0 + jnp.tanh(math.sqrt(2.0 / math.pi) * (x + 0.044715 * (x * x * x))))


def _glu_kernel(y_ref, u_ref, d_ref, w_ref, o_ref, wbf_ref):
    @pl.when(pl.program_id(0) == 0)
    def _():
        wbf_ref[...] = w_ref[...].astype(BF16)

    ys = _gelu_tanh(y_ref[...] + d_ref[...] * u_ref[...])
    t = jnp.dot(ys.astype(BF16), wbf_ref[...], preferred_element_type=F32)
    o_ref[...] = (ys * jax.nn.sigmoid(t)).astype(o_ref.dtype)


def _glu(y, z, d_all, w_all, l):
    nt = y.shape[0]
    tm = 512
    return pl.pallas_call(
        _glu_kernel,
        grid=(nt // tm,),
        in_specs=[
            pl.BlockSpec((tm, SSM_W), lambda i: (i, 0)),
            pl.BlockSpec((tm, SSM_W), lambda i: (i, 3 * ATT_W // SSM_W)),
            pl.BlockSpec((None, 1, SSM_W), lambda i: (l, 0, 0)),
            pl.BlockSpec((None, SSM_W, SSM_W), lambda i: (l, 0, 0)),
        ],
        out_specs=pl.BlockSpec((tm, SSM_W), lambda i: (i, 0)),
        out_shape=jax.ShapeDtypeStruct((nt, SSM_W), BF16),
        scratch_shapes=[pltpu.VMEM((SSM_W, SSM_W), BF16)],
        compiler_params=_cparams("arbitrary"),
        name="glu",
    )(y, z, d_all.reshape(DEPTH, 1, SSM_W), w_all)


def _merge_kernel(o_ref, y_ref, ga_ref, gs_ref, wa_ref, ws_ref, m_ref):
    a = jnp.dot(o_ref[...], wa_ref[...].astype(BF16), preferred_element_type=F32)
    s = jnp.dot(y_ref[...], ws_ref[...].astype(BF16), preferred_element_type=F32)
    m_ref[...] = (jax.nn.sigmoid(ga_ref[...]) * a + jax.nn.sigmoid(gs_ref[...]) * s).astype(m_ref.dtype)


def _merge(o_p, o_s, ys2, z, w_ba_all, w_bs_all, l):
    nt = ys2.shape[0]
    tm, tn = 1024, 512
    ga0 = (3 * ATT_W + SSM_W) // tn
    gs0 = ga0 + D_MODEL // tn
    p_tiles = o_p.shape[0] // tm
    s_tiles = o_s.shape[0] // tm
    assert p_tiles * tm == o_p.shape[0] and s_tiles * tm == o_s.shape[0]

    def kern(op_ref, os_ref, *rest):
        i = pl.program_id(0)

        @pl.when(i < p_tiles)
        def _():
            _merge_kernel(op_ref, *rest)

        @pl.when(i >= p_tiles)
        def _():
            _merge_kernel(os_ref, *rest)

    return pl.pallas_call(
        kern,
        grid=(nt // tm, D_MODEL // tn),
        in_specs=[
            pl.BlockSpec((tm, ATT_W), lambda i, j: (jnp.minimum(i, p_tiles - 1), 0)),
            pl.BlockSpec((tm, ATT_W), lambda i, j: (jnp.maximum(i - p_tiles, 0), 0)),
            pl.BlockSpec((tm, SSM_W), lambda i, j: (i, 0)),
            pl.BlockSpec((tm, tn), lambda i, j: (i, ga0 + j)),
            pl.BlockSpec((tm, tn), lambda i, j: (i, gs0 + j)),
            pl.BlockSpec((None, ATT_W, tn), lambda i, j: (l, 0, j)),
            pl.BlockSpec((None, SSM_W, tn), lambda i, j: (l, 0, j)),
        ],
        out_specs=pl.BlockSpec((tm, tn), lambda i, j: (i, j)),
        out_shape=jax.ShapeDtypeStruct((nt, D_MODEL), BF16),
        compiler_params=_cparams("arbitrary", "arbitrary"),
        name="merge",
    )(o_p, o_s, ys2, z, z, w_ba_all, w_bs_all)


def _proj_residual_kernel(a_ref, w_ref, x_ref, g_ref, o_ref):
    r = jnp.dot(a_ref[...], w_ref[...].astype(BF16), preferred_element_type=F32)
    tm, tn = r.shape
    r3 = r.reshape(tm // CHUNK, CHUNK, tn) * g_ref[...]
    o_ref[...] = x_ref[...] + r3.reshape(tm, tn)


def _proj_residual(a, w_all, x, tab, l, which_gate):
    nt, k = a.shape
    tm, tn = 1024, 512
    cpt = tm // CHUNK
    return pl.pallas_call(
        _proj_residual_kernel,
        grid=(nt // tm, D_MODEL // tn),
        in_specs=[
            pl.BlockSpec((tm, k), lambda i, j: (i, 0)),
            pl.BlockSpec((None, k, tn), lambda i, j: (l, 0, j)),
            pl.BlockSpec((tm, tn), lambda i, j: (i, j)),
            _tab_spec(cpt, l, which_gate, width=tn, col=True),
        ],
        out_specs=pl.BlockSpec((tm, tn), lambda i, j: (i, j)),
        out_shape=jax.ShapeDtypeStruct((nt, D_MODEL), F32),
        compiler_params=_cparams("arbitrary", "arbitrary"),
        name="proj_residual",
    )(a, w_all, x, tab)


def _swiglu_step(h, wg_ref, wu_ref, wd_ref, row_gate=None):
    hg = jnp.dot(h, wg_ref[...].astype(BF16), preferred_element_type=F32)
    hu = jnp.dot(h, wu_ref[...].astype(BF16), preferred_element_type=F32)
    a = (hg * jax.nn.sigmoid(hg)) * hu
    if row_gate is not None:
        a = a * row_gate
    return jnp.dot(a.astype(BF16), wd_ref[...].astype(BF16), preferred_element_type=F32)


def _ffn_kernel(h_ref, wg_ref, wu_ref, wd_ref, o_ref):
    @pl.when(pl.program_id(1) == 0)
    def _():
        o_ref[...] = jnp.zeros_like(o_ref)

    o_ref[...] += _swiglu_step(h_ref[...], wg_ref, wu_ref, wd_ref)


def _ffn(h, w_gate, w_up, w_down, j):
    nt = h.shape[0]
    d_ff = w_gate.shape[-1]
    tm, tf = FFN_TM, FFN_TF
    return pl.pallas_call(
        _ffn_kernel,
        grid=(nt // tm, d_ff // tf),
        in_specs=[
            pl.BlockSpec((tm, D_MODEL), lambda i, f: (i, 0)),
            pl.BlockSpec((None, D_MODEL, tf), lambda i, f: (j, 0, f)),
            pl.BlockSpec((None, D_MODEL, tf), lambda i, f: (j, 0, f)),
            pl.BlockSpec((None, tf, D_MODEL), lambda i, f: (j, f, 0)),
        ],
        out_specs=pl.BlockSpec((tm, D_MODEL), lambda i, f: (i, 0)),
        out_shape=jax.ShapeDtypeStruct((nt, D_MODEL), F32),
        compiler_params=_cparams("arbitrary", "arbitrary"),
        name="ffn",
    )(h, w_gate, w_up, w_down)


def _moe_plan(route, nt):
    tm = FFN_TM
    n_assign = 2 * nt
    n_tiles = (n_assign + N_EXPERTS * (tm - 1)) // tm
    e_flat = jnp.concatenate([route[:, 0], route[:, 1]]).astype(jnp.int32)
    w_flat = jnp.concatenate([route[:, 2], route[:, 3]])
    onehot = (e_flat[:, None] == jnp.arange(N_EXPERTS, dtype=jnp.int32)[None, :]).astype(jnp.int32)
    csum = jnp.cumsum(onehot, axis=0)
    counts = csum[-1]
    rank = jnp.sum((csum - onehot) * onehot, axis=1)
    tiles_per = (counts + tm - 1) // tm
    tile_end = jnp.cumsum(tiles_per)
    offsets = (tile_end - tiles_per) * tm
    dest = jnp.sum(onehot * offsets[None, :], axis=1) + rank
    tidx = jnp.arange(n_tiles, dtype=jnp.int32)
    tile_expert = jnp.minimum(jnp.sum((tidx[:, None] >= tile_end[None, :]).astype(jnp.int32), axis=1), N_EXPERTS - 1)
    tile_valid = (tidx < tile_end[-1]).astype(jnp.int32)
    tok = jnp.concatenate([jnp.arange(nt, dtype=jnp.int32)] * 2)
    src_token = jnp.zeros((n_tiles * tm,), jnp.int32).at[dest].set(tok)
    slot_gate = jnp.zeros((n_tiles * tm,), F32).at[dest].set(w_flat)
    return dict(dest1=dest[:nt], dest2=dest[nt:], src_token=src_token, slot_gate=slot_gate[:, None],
                tile_expert=tile_expert.astype(jnp.int32), tile_valid=tile_valid, n_tiles=n_tiles)


def _row_copy(src_ref, row, dst_ref, r, sem):
    return pltpu.make_async_copy(src_ref.at[pl.ds(row, 1)], dst_ref.at[pl.ds(r, 1)], sem)


def _gather_into(idx_ref, base, src_ref, dst_ref, sem):
    rows = dst_ref.shape[0]

    def start(r, c):
        _row_copy(src_ref, idx_ref[base + r], dst_ref, r, sem).start()
        return c

    lax.fori_loop(0, rows, start, 0)


def _gather_wait(idx_ref, base, src_ref, dst_ref, sem):
    rows = dst_ref.shape[0]

    def wait(r, c):
        _row_copy(src_ref, idx_ref[base + r], dst_ref, r, sem).wait()
        return c

    lax.fori_loop(0, rows, wait, 0)


def _gather_rows_kernel(idx_ref, src_ref, o_ref, buf_ref, sem):
    base = pl.program_id(0) * GATHER_ROWS
    _gather_into(idx_ref, base, src_ref, buf_ref, sem)
    _gather_wait(idx_ref, base, src_ref, buf_ref, sem)
    o_ref[...] = buf_ref[...].astype(o_ref.dtype)


def _gather_rows(src, idx, out_dtype):
    n_out = idx.shape[0]
    width = src.shape[1]
    return pl.pallas_call(
        _gather_rows_kernel,
        grid_spec=pltpu.PrefetchScalarGridSpec(
            num_scalar_prefetch=1,
            grid=(n_out // GATHER_ROWS,),
            in_specs=[pl.BlockSpec(memory_space=pl.ANY)],
            out_specs=pl.BlockSpec((GATHER_ROWS, width), lambda i, *_: (i, 0)),
            scratch_shapes=[pltpu.VMEM((GATHER_ROWS, width), src.dtype), pltpu.SemaphoreType.DMA(())],
        ),
        out_shape=jax.ShapeDtypeStruct((n_out, width), out_dtype),
        compiler_params=_cparams("arbitrary"),
        name="moe_gather",
    )(idx, src)


def _moe_ffn_kernel(te_ref, tv_ref, h_ref, wg_ref, wu_ref, wd_ref, gate_ref, o_ref):
    t = pl.program_id(0)

    @pl.when(pl.program_id(1) == 0)
    def _():
        o_ref[...] = jnp.zeros_like(o_ref)

    @pl.when(tv_ref[t] == 1)
    def _():
        o_ref[...] += _swiglu_step(h_ref[...], wg_ref, wu_ref, wd_ref, row_gate=gate_ref[...])


def _moe_ffn(h_sorted, slot_gate, tile_expert, tile_valid, w_gate, w_up, w_down, j):
    d_ff = w_gate.shape[-1]
    tm, tf = FFN_TM, FFN_TF
    n_tiles = h_sorted.shape[0] // tm
    nf = d_ff // tf

    def f_idx(t, f, tv):
        return jnp.where(tv[t] == 1, f, nf - 1)

    return pl.pallas_call(
        _moe_ffn_kernel,
        grid_spec=pltpu.PrefetchScalarGridSpec(
            num_scalar_prefetch=2,
            grid=(n_tiles, nf),
            in_specs=[
                pl.BlockSpec((tm, D_MODEL), lambda t, f, te, tv: (t, 0)),
                pl.BlockSpec((None, None, D_MODEL, tf), lambda t, f, te, tv: (j, te[t], 0, f_idx(t, f, tv))),
                pl.BlockSpec((None, None, D_MODEL, tf), lambda t, f, te, tv: (j, te[t], 0, f_idx(t, f, tv))),
                pl.BlockSpec((None, None, tf, D_MODEL), lambda t, f, te, tv: (j, te[t], f_idx(t, f, tv), 0)),
                pl.BlockSpec((tm, 1), lambda t, f, te, tv: (t, 0)),
            ],
            out_specs=pl.BlockSpec((tm, D_MODEL), lambda t, f, te, tv: (t, 0)),
        ),
        out_shape=jax.ShapeDtypeStruct((n_tiles * tm, D_MODEL), F32),
        compiler_params=_cparams("arbitrary", "arbitrary"),
        name="moe_ffn",
    )(tile_expert, tile_valid, h_sorted, w_gate, w_up, w_down, slot_gate)


def _moe_combine_kernel(d1_ref, d2_ref, y_ref, x_ref, g_ref, o_ref, a_ref, b_ref, sem_a, sem_b):
    base = pl.program_id(0) * GATHER_ROWS
    _gather_into(d1_ref, base, y_ref, a_ref, sem_a)
    _gather_into(d2_ref, base, y_ref, b_ref, sem_b)
    _gather_wait(d1_ref, base, y_ref, a_ref, sem_a)
    _gather_wait(d2_ref, base, y_ref, b_ref, sem_b)
    r = a_ref[...] + b_ref[...]
    tm, n = r.shape
    o_ref[...] = x_ref[...] + (r.reshape(tm // CHUNK, CHUNK, n) * g_ref[...]).reshape(tm, n)


def _moe_combine(y_sorted, dest1, dest2, x, tab, l, which_gate):
    nt = x.shape[0]
    tm = GATHER_ROWS
    blk = pl.BlockSpec((tm, D_MODEL), lambda i, *_: (i, 0))
    return pl.pallas_call(
        _moe_combine_kernel,
        grid_spec=pltpu.PrefetchScalarGridSpec(
            num_scalar_prefetch=2,
            grid=(nt // tm,),
            in_specs=[pl.BlockSpec(memory_space=pl.ANY), blk, _tab_spec(tm // CHUNK, l, which_gate)],
            out_specs=blk,
            scratch_shapes=[pltpu.VMEM((tm, D_MODEL), F32), pltpu.VMEM((tm, D_MODEL), F32),
                            pltpu.SemaphoreType.DMA(()), pltpu.SemaphoreType.DMA(())],
        ),
        out_shape=jax.ShapeDtypeStruct((nt, D_MODEL), F32),
        compiler_params=_cparams("arbitrary"),
        name="moe_combine",
    )(dest1, dest2, y_sorted, x, tab)


def _residual_kernel(x_ref, r_ref, g_ref, o_ref):
    r = r_ref[...]
    tm, n = r.shape
    o_ref[...] = x_ref[...] + (r.reshape(tm // CHUNK, CHUNK, n) * g_ref[...]).reshape(tm, n)


def _residual(x, r, tab, l, which_gate):
    nt = x.shape[0]
    tm = 512
    blk = pl.BlockSpec((tm, D_MODEL), lambda i: (i, 0))
    return pl.pallas_call(
        _residual_kernel, grid=(nt // tm,),
        in_specs=[blk, blk, _tab_spec(tm // CHUNK, l, which_gate)],
        out_specs=blk, out_shape=jax.ShapeDtypeStruct((nt, D_MODEL), F32),
        compiler_params=_cparams("arbitrary"), name="residual",
    )(x, r, tab)


def _final_norm_kernel(x_ref, g_ref, o_ref):
    x = x_ref[...]
    o_ref[...] = x * lax.rsqrt(jnp.mean(x * x, axis=-1, keepdims=True) + EPS) * g_ref[...]


def _final_norm(x, g):
    nt = x.shape[0]
    tm = 512
    blk = pl.BlockSpec((tm, D_MODEL), lambda i: (i, 0))
    return pl.pallas_call(
        _final_norm_kernel, grid=(nt // tm,),
        in_specs=[blk, pl.BlockSpec((1, D_MODEL), lambda i: (0, 0))],
        out_specs=blk, out_shape=jax.ShapeDtypeStruct((nt, D_MODEL), F32),
        compiler_params=_cparams("arbitrary"), name="final_norm",
    )(x, g.reshape(1, D_MODEL))


def kernel(x_prompt, x_sample, cache_k, cache_v, state_ssm_re, state_ssm_im, c_prompt, c_sample, w_ada, b_ada, g_norm_mix, g_norm_ffn, g_norm_final, w_in, rel_bias, ssm_a_re, ssm_a_im, ssm_log_step, ssm_b_re, ssm_b_im, ssm_c_re, ssm_c_im, ssm_d, w_ssm_glu, w_branch_attn, w_branch_ssm, w_out, w_ffn_gate, w_ffn_up, w_ffn_down, w_router, b_router, w_exp_gate, w_exp_up, w_exp_down):
    n_pb, seq = x_prompt.shape[0], x_prompt.shape[1]
    n_sb, dec = x_sample.shape[0], x_sample.shape[1]
    assert dec == CHUNK and cache_k.shape[2] == PAST_WINDOW and seq % ATT_ROWS == 0
    np_tok = n_pb * seq
    nt = np_tok + n_sb * dec
    n_batch = n_pb + n_sb

    x = jnp.concatenate([x_prompt.reshape(np_tok, D_MODEL), x_sample.reshape(n_sb * dec, D_MODEL)], axis=0)

    c_all = jnp.concatenate([c_prompt, c_sample], axis=0)
    c_pad = jnp.pad(c_all, ((0, -n_batch % 8), (0, 0)))
    ada = _ada(c_pad, w_ada, b_ada)
    ada_p = jnp.broadcast_to(ada[:, :n_pb, None, :], (DEPTH, n_pb, seq // CHUNK, 6 * D_MODEL))
    tab = jnp.concatenate([ada_p.reshape(DEPTH, np_tok // CHUNK, 6 * D_MODEL), ada[:, n_pb:n_batch]], axis=1)
    tab = tab.reshape(DEPTH, nt // CHUNK, 6, 1, D_MODEL)

    sub_p, sub_s = seq // SSM_T, dec // SSM_T
    seq_id = jnp.concatenate([jnp.repeat(jnp.arange(n_pb), sub_p), n_pb + jnp.repeat(jnp.arange(n_sb), sub_s)])
    seq_id = seq_id.astype(jnp.int32)
    first = jnp.concatenate([jnp.arange(n_pb * sub_p) % sub_p == 0, jnp.arange(n_sb * sub_s) % sub_s == 0])
    first = first.astype(jnp.int32)
    last_p = (jnp.arange(n_pb) + 1) * sub_p - 1
    last_s = n_pb * sub_p + (jnp.arange(n_sb) + 1) * sub_s - 1

    cache_k4 = cache_k.reshape(DEPTH, n_sb, PAST_WINDOW, ATT_W)
    cache_v4 = cache_v.reshape(DEPTH, n_sb, PAST_WINDOW, ATT_W)

    outs = {k: [] for k in ('kp', 'vp', 'srp', 'sip', 'ks', 'vs', 'srs', 'sis')}
    for l in range(DEPTH):
        h = _norm_mod(x, g_norm_mix, tab, l, SHIFT1, SCALE1)
        z = _in_proj(h, w_in, l)

        bias_s, bias_p = _attention_bias(rel_bias[l])
        o_p = _attn_prompt(z, bias_p, n_pb, seq)
        o_s = _attn_sample(z, cache_k4, cache_v4, l, bias_s, np_tok, n_sb)

        mats = _ssm_matrices(ssm_a_re[l], ssm_a_im[l], ssm_log_step[l], ssm_b_re[l], ssm_b_im[l],
                             ssm_c_re[l], ssm_c_im[l])
        zeros_h0 = jnp.zeros((n_pb, SSM_GROUPS, SSM_STATE), F32)
        h0_re = jnp.concatenate([zeros_h0, state_ssm_re[l].astype(F32)], axis=0)
        h0_im = jnp.concatenate([zeros_h0, state_ssm_im[l].astype(F32)], axis=0)
        y, xe_re, xe_im = _ssm(z, mats, first, seq_id, h0_re, h0_im)

        ys2 = _glu(y, z, ssm_d, w_ssm_glu, l)
        merged = _merge(o_p, o_s, ys2, z, w_branch_attn, w_branch_ssm, l)
        x = _proj_residual(merged, w_out, x, tab, l, GATE1)

        j = l // 2
        if l % 2 == 0:
            h2 = _norm_mod(x, g_norm_ffn, tab, l, SHIFT2, SCALE2)
            r = _ffn(h2, w_ffn_gate, w_ffn_up, w_ffn_down, j)
            x = _residual(x, r, tab, l, GATE2)
        else:
            h2, route = _norm_mod(x, g_norm_ffn, tab, l, SHIFT2, SCALE2, router=(w_router[j], b_router[j]))
            plan = _moe_plan(route, nt)
            h_sorted = _gather_rows(h2, plan['src_token'], BF16)
            y_sorted = _moe_ffn(h_sorted, plan['slot_gate'], plan['tile_expert'], plan['tile_valid'],
                                w_exp_gate, w_exp_up, w_exp_down, j)
            x = _moe_combine(y_sorted, plan['dest1'], plan['dest2'], x, tab, l, GATE2)

        keep = min(PAST_WINDOW, seq)
        kv_p = z[:np_tok, ATT_W:3 * ATT_W].reshape(n_pb, seq, 2, N_HEADS, HEAD_DIM)[:, seq - keep:]
        kv_s = z[np_tok:, ATT_W:3 * ATT_W].reshape(n_sb, dec, 2, N_HEADS, HEAD_DIM)
        outs['kp'].append(kv_p[:, :, 0])
        outs['vp'].append(kv_p[:, :, 1])
        outs['ks'].append(kv_s[:, :, 0])
        outs['vs'].append(kv_s[:, :, 1])
        outs['srp'].append(xe_re[last_p])
        outs['sip'].append(xe_im[last_p])
        outs['srs'].append(xe_re[last_s])
        outs['sis'].append(xe_im[last_s])

    yn = _final_norm(x, g_norm_final)
    y_prompt = yn[:np_tok].reshape(n_pb, seq, D_MODEL)
    y_sample = yn[np_tok:].reshape(n_sb, dec, D_MODEL)
    st = lambda name: jnp.stack(outs[name])
    return (y_prompt, y_sample, st('kp'), st('vp'), st('srp'), st('sip'),
            st('ks'), st('vs'), st('srs'), st('sis'))
```

```python
import functools
import math

import jax
import jax.numpy as jnp
from jax import lax
from jax.experimental import pallas as pl
from jax.experimental.pallas import tpu as pltpu

F32 = jnp.float32
BF16 = jnp.bfloat16

D_MODEL = 2048
DEPTH = 2
CHUNK = 64
PAST_CHUNKS = 8
PAST_WINDOW = PAST_CHUNKS * CHUNK
BAND = PAST_WINDOW + CHUNK
N_HEADS = 8
HEAD_DIM = 128
ATT_W = N_HEADS * HEAD_DIM
ATT_SCALE = HEAD_DIM ** -0.5
REL_CLIP = 256
SSM_W = 1024
SSM_GROUP = 16
SSM_GROUPS = SSM_W // SSM_GROUP
SSM_STATE = 64
N_EXPERTS = 8
IN_COLS = 3 * ATT_W + SSM_W + 2 * D_MODEL
EPS = 1e-6
NEG_INF = -1e30

LANES = 128
VMEM_LIMIT_BYTES = 56 * 1024 * 1024

SSM_T = 16
SSM_OCT = LANES // SSM_GROUP
N_OCT = SSM_GROUPS // SSM_OCT
OCT_K = SSM_T * LANES
OCT_STATE = SSM_OCT * SSM_STATE
ATT_GROUP = 4
ATT_ROWS = ATT_GROUP * CHUNK
ATT_KEYS = 3 * ATT_ROWS

IN_TM, IN_TN = 1024, 512
FFN_TM = 768
FFN_TF = 512
GATHER_ROWS = 256
GATHER_UNROLL = 8

SHIFT1, SCALE1, GATE1, SHIFT2, SCALE2, GATE2 = range(6)


def _cparams(*sem):
    return pltpu.CompilerParams(dimension_semantics=sem, vmem_limit_bytes=VMEM_LIMIT_BYTES)


def _tab_spec(cpt, l, which, width=D_MODEL, col=None):
    if col is None:
        return pl.BlockSpec((None, cpt, None, 1, width), lambda i, *_: (l, i, which, 0, 0))
    return pl.BlockSpec((None, cpt, None, 1, width), lambda i, j, *_: (l, i, which, 0, j))


def _ada_kernel(c_ref, w_ref, b_ref, o_ref):
    c = c_ref[...]
    a = (c * jax.nn.sigmoid(c)).astype(BF16)
    o_ref[...] = jnp.dot(a, w_ref[...].astype(BF16), preferred_element_type=F32) + b_ref[...]


def _ada(c_pad, w_ada, b_ada):
    rows = c_pad.shape[0]
    n = w_ada.shape[-1]
    tn = 1024
    return pl.pallas_call(
        _ada_kernel,
        grid=(DEPTH, n // tn),
        in_specs=[
            pl.BlockSpec((rows, D_MODEL), lambda l, j: (0, 0)),
            pl.BlockSpec((None, D_MODEL, tn), lambda l, j: (l, 0, j)),
            pl.BlockSpec((None, 1, tn), lambda l, j: (l, 0, j)),
        ],
        out_specs=pl.BlockSpec((None, rows, tn), lambda l, j: (l, 0, j)),
        out_shape=jax.ShapeDtypeStruct((DEPTH, rows, n), F32),
        compiler_params=_cparams("arbitrary", "arbitrary"),
        name="ada",
    )(c_pad, w_ada, b_ada.reshape(DEPTH, 1, n))


def _modulated(x_ref, g_ref, sh_ref, sc_ref):
    x = x_ref[...]
    tm = x.shape[0]
    y = x * lax.rsqrt(jnp.mean(x * x, axis=-1, keepdims=True) + EPS) * g_ref[...]
    y3 = y.reshape(tm // CHUNK, CHUNK, D_MODEL)
    h = y3 * (1.0 + sc_ref[...]) + sh_ref[...]
    return h.reshape(tm, D_MODEL)


def _norm_mod_kernel(x_ref, g_ref, sh_ref, sc_ref, h_ref):
    h_ref[...] = _modulated(x_ref, g_ref, sh_ref, sc_ref).astype(BF16)


def _split_bf16(a):
    hi = a.astype(BF16)
    lo = (a - hi.astype(F32)).astype(BF16)
    return hi, lo


def _norm_mod_router_kernel(x_ref, g_ref, sh_ref, sc_ref, wr_ref, br_ref, h_ref, route_ref):
    h = _modulated(x_ref, g_ref, sh_ref, sc_ref)
    h_ref[...] = h
    h_hi, h_lo = _split_bf16(h)
    w_hi, w_lo = _split_bf16(wr_ref[...])
    dot = functools.partial(jnp.dot, preferred_element_type=F32)
    logits = (dot(h_hi, w_hi) + (dot(h_hi, w_lo) + dot(h_lo, w_hi)) + dot(h_lo, w_lo)) + br_ref[...]
    lane = lax.broadcasted_iota(jnp.int32, logits.shape, 1).astype(F32)
    lg = jnp.where(lane < N_EXPERTS, logits, -jnp.inf)
    m1 = jnp.max(lg, axis=-1, keepdims=True)
    i1 = jnp.min(jnp.where(lg == m1, lane, float(LANES)), axis=-1, keepdims=True)
    lg2 = jnp.where(lane == i1, -jnp.inf, lg)
    m2 = jnp.max(lg2, axis=-1, keepdims=True)
    i2 = jnp.min(jnp.where(lg2 == m2, lane, float(LANES)), axis=-1, keepdims=True)
    e2 = jnp.exp(m2 - m1)
    den = 1.0 + e2
    route = jnp.where(lane == 0.0, i1, jnp.where(lane == 1.0, i2, jnp.where(lane == 2.0, 1.0 / den, e2 / den)))
    route_ref[...] = jnp.where(lane < 4.0, route, 0.0)


def _norm_mod(x, g_all, tab, l, which_shift, which_scale, router=None):
    nt = x.shape[0]
    tm = 512
    cpt = tm // CHUNK
    in_specs = [
        pl.BlockSpec((tm, D_MODEL), lambda i: (i, 0)),
        pl.BlockSpec((None, 1, D_MODEL), lambda i: (l, 0, 0)),
        _tab_spec(cpt, l, which_shift),
        _tab_spec(cpt, l, which_scale),
    ]
    h_spec = pl.BlockSpec((tm, D_MODEL), lambda i: (i, 0))
    args = [x, g_all.reshape(DEPTH, 1, D_MODEL), tab, tab]
    if router is None:
        return pl.pallas_call(
            _norm_mod_kernel, grid=(nt // tm,), in_specs=in_specs, out_specs=h_spec,
            out_shape=jax.ShapeDtypeStruct((nt, D_MODEL), BF16),
            compiler_params=_cparams("arbitrary"), name="norm_mod",
        )(*args)
    w_r, b_r = router
    w_pad = jnp.zeros((D_MODEL, LANES), F32).at[:, :N_EXPERTS].set(w_r)
    b_pad = jnp.zeros((1, LANES), F32).at[0, :N_EXPERTS].set(b_r)
    in_specs += [
        pl.BlockSpec((D_MODEL, LANES), lambda i: (0, 0)),
        pl.BlockSpec((1, LANES), lambda i: (0, 0)),
    ]
    return pl.pallas_call(
        _norm_mod_router_kernel, grid=(nt // tm,), in_specs=in_specs,
        out_specs=[h_spec, pl.BlockSpec((tm, LANES), lambda i: (i, 0))],
        out_shape=[jax.ShapeDtypeStruct((nt, D_MODEL), F32), jax.ShapeDtypeStruct((nt, LANES), F32)],
        compiler_params=_cparams("arbitrary"), name="norm_mod_router",
    )(*args, w_pad, b_pad)


def _in_proj_kernel(x_ref, w_ref, z_ref, u3_ref, scr_ref):
    j = pl.program_id(1)
    r = jnp.dot(x_ref[...], w_ref[...].astype(BF16), preferred_element_type=F32)
    z_ref[...] = r
    tm, tn = r.shape
    u_first = 3 * ATT_W // tn
    for part in range(SSM_W // tn):
        @pl.when(j == u_first + part)
        def _():
            for c in range(tn // LANES):
                scr_ref[c] = r[:, c * LANES:(c + 1) * LANES]
            for s in range(SSM_T):
                for c in range(tn // LANES):
                    lo = part * tn + c * LANES
                    rows = scr_ref[c, pl.ds(s, tm // SSM_T, stride=SSM_T), :]
                    u3_ref[s, :, lo:lo + LANES] = rows.astype(BF16)


def _in_proj(h, w_all, l):
    nt, k = h.shape
    n = w_all.shape[-1]
    tm, tn = IN_TM, IN_TN
    return pl.pallas_call(
        _in_proj_kernel,
        grid=(nt // tm, n // tn),
        in_specs=[
            pl.BlockSpec((tm, k), lambda i, j: (i, 0)),
            pl.BlockSpec((None, k, tn), lambda i, j: (l, 0, j)),
        ],
        out_specs=[
            pl.BlockSpec((tm, tn), lambda i, j: (i, j)),
            pl.BlockSpec((SSM_T, tm // SSM_T, SSM_W), lambda i, j: (0, i, 0)),
        ],
        out_shape=[
            jax.ShapeDtypeStruct((nt, n), F32),
            jax.ShapeDtypeStruct((SSM_T, nt // SSM_T, SSM_W), BF16),
        ],
        scratch_shapes=[pltpu.VMEM((tn // LANES, tm, LANES), F32)],
        compiler_params=_cparams("arbitrary", "arbitrary"),
        name="in_proj",
    )(h, w_all)


def _softmax_rows(s):
    m = jnp.max(s, axis=-1, keepdims=True)
    e = jnp.exp(s - m)
    return e / jnp.sum(e, axis=-1, keepdims=True)


def _attn_prompt_kernel(q_ref, k0_ref, k1_ref, k2_ref, v0_ref, v1_ref, v2_ref, b64_ref, o_ref, bias_ref):
    g = pl.program_id(1)

    @pl.when((pl.program_id(0) == 0) & (g == 0))
    def _():
        bias_ref[...] = jnp.full(bias_ref.shape, NEG_INF, F32)
        for c in range(ATT_GROUP):
            bias_ref[:, c * CHUNK:(c + 1) * CHUNK, c * CHUNK:c * CHUNK + BAND] = b64_ref[...]

    col = lax.broadcasted_iota(jnp.int32, (ATT_ROWS, ATT_KEYS), 1)
    visible = (g * ATT_ROWS - 2 * ATT_ROWS + col) >= 0
    for h in range(N_HEADS):
        sl = slice(h * HEAD_DIM, (h + 1) * HEAD_DIM)
        qh = q_ref[:, sl].astype(BF16)
        kh = jnp.concatenate([k0_ref[:, sl], k1_ref[:, sl], k2_ref[:, sl]], axis=0).astype(BF16)
        vh = jnp.concatenate([v0_ref[:, sl], v1_ref[:, sl], v2_ref[:, sl]], axis=0).astype(BF16)
        s = lax.dot_general(qh, kh, (((1,), (1,)), ((), ())), preferred_element_type=F32)
        s = s * ATT_SCALE + bias_ref[h]
        s = jnp.where(visible, s, NEG_INF)
        p = _softmax_rows(s).astype(BF16)
        o_ref[:, sl] = jnp.dot(p, vh, preferred_element_type=F32).astype(o_ref.dtype)


def _attn_prompt(z, b64, n_batch, seq):
    groups = seq // ATT_ROWS
    blk = (ATT_ROWS, ATT_W)

    def kv_spec(col, back):
        return pl.BlockSpec(blk, lambda b, g: (b * groups + jnp.maximum(g - back, 0), col))

    return pl.pallas_call(
        _attn_prompt_kernel,
        grid=(n_batch, groups),
        in_specs=[
            pl.BlockSpec(blk, lambda b, g: (b * groups + g, 0)),
            kv_spec(1, 2), kv_spec(1, 1), kv_spec(1, 0),
            kv_spec(2, 2), kv_spec(2, 1), kv_spec(2, 0),
            pl.BlockSpec((N_HEADS, CHUNK, BAND), lambda b, g: (0, 0, 0)),
        ],
        out_specs=pl.BlockSpec(blk, lambda b, g: (b * groups + g, 0)),
        out_shape=jax.ShapeDtypeStruct((n_batch * seq, ATT_W), BF16),
        scratch_shapes=[pltpu.VMEM((N_HEADS, ATT_ROWS, ATT_KEYS), F32)],
        compiler_params=_cparams("arbitrary", "arbitrary"),
        name="attn_prompt",
    )(z, z, z, z, z, z, z, b64)


def _attn_sample_kernel(q_ref, kn_ref, vn_ref, kc_ref, vc_ref, bias_ref, o_ref):
    for h in range(N_HEADS):
        sl = slice(h * HEAD_DIM, (h + 1) * HEAD_DIM)
        qh = q_ref[:, sl].astype(BF16)
        kh = jnp.concatenate([kc_ref[:, h, :], kn_ref[:, sl]], axis=0).astype(BF16)
        vh = jnp.concatenate([vc_ref[:, h, :], vn_ref[:, sl]], axis=0).astype(BF16)
        s = lax.dot_general(qh, kh, (((1,), (1,)), ((), ())), preferred_element_type=F32)
        s = s * ATT_SCALE + bias_ref[h]
        p = _softmax_rows(s).astype(BF16)
        o_ref[:, sl] = jnp.dot(p, vh, preferred_element_type=F32).astype(o_ref.dtype)


def _attn_sample(z, cache_k, cache_v, l, b64, row0, n_batch):
    base = row0 // CHUNK
    blk = (CHUNK, ATT_W)
    cache_spec = pl.BlockSpec((None, None, PAST_WINDOW, N_HEADS, HEAD_DIM), lambda b: (l, b, 0, 0, 0))
    return pl.pallas_call(
        _attn_sample_kernel,
        grid=(n_batch,),
        in_specs=[
            pl.BlockSpec(blk, lambda b: (base + b, 0)),
            pl.BlockSpec(blk, lambda b: (base + b, 1)),
            pl.BlockSpec(blk, lambda b: (base + b, 2)),
            cache_spec, cache_spec,
            pl.BlockSpec((N_HEADS, CHUNK, BAND), lambda b: (0, 0, 0)),
        ],
        out_specs=pl.BlockSpec(blk, lambda b: (b, 0)),
        out_shape=jax.ShapeDtypeStruct((n_batch * CHUNK, ATT_W), BF16),
        compiler_params=_cparams("arbitrary"),
        name="attn_sample",
    )(z, z, z, cache_k, cache_v, b64)


def _attention_bias(rel_bias):
    rev = rel_bias[::-1].astype(F32)
    flat = PAST_WINDOW - REL_CLIP + CHUNK - 1
    vec = jnp.concatenate([jnp.broadcast_to(rev[:1], (flat, N_HEADS)), rev[:BAND + CHUNK - 1 - flat]], axis=0)
    rows = [vec[CHUNK - 1 - qi:CHUNK - 1 - qi + BAND] for qi in range(CHUNK)]
    return jnp.transpose(jnp.stack(rows), (2, 0, 1))


def _octet_lhs(u3_ref):
    return jnp.concatenate([u3_ref[s] for s in range(SSM_T)], axis=1)


def _ssm_state_kernel(u3_ref, w_ref, sre_ref, sim_ref):
    s = jnp.dot(_octet_lhs(u3_ref), w_ref[...], preferred_element_type=F32)
    sre_ref[...] = s[:, :OCT_STATE]
    sim_ref[...] = s[:, OCT_STATE:]


def _ssm_state(u3, wst):
    n_sub = u3.shape[1]
    out = jax.ShapeDtypeStruct((N_OCT, n_sub, OCT_STATE), F32)
    return pl.pallas_call(
        _ssm_state_kernel,
        grid=(N_OCT,),
        in_specs=[
            pl.BlockSpec((SSM_T, n_sub, LANES), lambda j: (0, 0, j)),
            pl.BlockSpec((None, OCT_K, 2 * OCT_STATE), lambda j: (j, 0, 0)),
        ],
        out_specs=[pl.BlockSpec((None, n_sub, OCT_STATE), lambda j: (j, 0, 0))] * 2,
        out_shape=[out, out],
        compiler_params=_cparams("arbitrary"),
        name="ssm_state",
    )(u3, wst)


def _ssm_carry_kernel(first_ref, seq_ref, sre_ref, sim_ref, ar_ref, ai_ref, h0r_ref, h0i_ref,
                      xsr_ref, xsi_ref, xer_ref, xei_ref, st_ref):
    blk = pl.program_id(0)
    steps = sre_ref.shape[1]

    @pl.when(blk == 0)
    def _():
        st_ref[...] = jnp.zeros_like(st_ref)

    ar = ar_ref[...]
    ai = ai_ref[...]

    def body(kk, carry):
        k = blk * steps + kk
        is_first = first_ref[k] == 1
        sq = seq_ref[k]
        xr = jnp.where(is_first, h0r_ref[sq], st_ref[0])
        xi = jnp.where(is_first, h0i_ref[sq], st_ref[1])
        xsr_ref[:, kk, :] = xr
        xsi_ref[:, kk, :] = xi
        nr = ar * xr - ai * xi + sre_ref[:, kk, :]
        ni = ar * xi + ai * xr + sim_ref[:, kk, :]
        xer_ref[:, kk, :] = nr
        xei_ref[:, kk, :] = ni
        st_ref[0] = nr
        st_ref[1] = ni
        return carry

    lax.fori_loop(0, steps, body, 0)


def _ssm_carry(first, seq_id, sre, sim, at_re, at_im, h0_re, h0_im):
    n_sub = sre.shape[1]
    steps = 64
    tile = (N_OCT, OCT_STATE)
    blk = pl.BlockSpec((N_OCT, steps, OCT_STATE), lambda i, *_: (0, i, 0))
    const = pl.BlockSpec(tile, lambda i, *_: (0, 0))
    tab = pl.BlockSpec((h0_re.shape[0],) + tile, lambda i, *_: (0, 0, 0))
    out = jax.ShapeDtypeStruct((N_OCT, n_sub, OCT_STATE), F32)
    return pl.pallas_call(
        _ssm_carry_kernel,
        grid_spec=pltpu.PrefetchScalarGridSpec(
            num_scalar_prefetch=2,
            grid=(n_sub // steps,),
            in_specs=[blk, blk, const, const, tab, tab],
            out_specs=[blk, blk, blk, blk],
            scratch_shapes=[pltpu.VMEM((2,) + tile, F32)],
        ),
        out_shape=[out, out, out, out],
        compiler_params=_cparams("arbitrary"),
        name="ssm_carry",
    )(first, seq_id, sre, sim, at_re.reshape(tile), at_im.reshape(tile),
      h0_re.reshape((-1,) + tile), h0_im.reshape((-1,) + tile))


def _ssm_out_kernel(u3_ref, kt_ref, xr_ref, xi_ref, cr_ref, ci_ref, y3_ref):
    y = jnp.dot(_octet_lhs(u3_ref), kt_ref[...], preferred_element_type=F32)
    y += jnp.dot(xr_ref[...].astype(BF16), cr_ref[...], preferred_element_type=F32)
    y += jnp.dot(xi_ref[...].astype(BF16), ci_ref[...], preferred_element_type=F32)
    for t in range(SSM_T):
        y3_ref[t] = y[:, t * LANES:(t + 1) * LANES]


def _ssm_out(u3, kt, xs_re, xs_im, cp_re, cp_im):
    n_sub = u3.shape[1]
    u_spec = pl.BlockSpec((SSM_T, n_sub, LANES), lambda j: (0, 0, j))
    x_spec = pl.BlockSpec((None, n_sub, OCT_STATE), lambda j: (j, 0, 0))
    c_spec = pl.BlockSpec((None, OCT_STATE, OCT_K), lambda j: (j, 0, 0))
    return pl.pallas_call(
        _ssm_out_kernel,
        grid=(N_OCT,),
        in_specs=[u_spec, pl.BlockSpec((None, OCT_K, OCT_K), lambda j: (j, 0, 0)), x_spec, x_spec, c_spec, c_spec],
        out_specs=u_spec,
        out_shape=jax.ShapeDtypeStruct((SSM_T, n_sub, SSM_W), F32),
        compiler_params=_cparams("arbitrary"),
        name="ssm_out",
    )(u3, kt, xs_re, xs_im, cp_re, cp_im)


def _ssm_matrices(a_re, a_im, log_step, b_re, b_im, c_re, c_im):
    hp = lax.Precision.HIGHEST
    G, P, T, O = SSM_GROUPS, SSM_STATE, SSM_T, SSM_OCT
    step = jnp.exp(log_step.astype(F32))[:, None]
    mag = jnp.exp(a_re * step)
    ang = a_im * step
    ab_re = mag * jnp.cos(ang)
    ab_im = mag * jnp.sin(ang)
    den = a_re * a_re + a_im * a_im
    n_re = ab_re - 1.0
    f_re = (n_re * a_re + ab_im * a_im) / den
    f_im = (ab_im * a_re - n_re * a_im) / den
    bb_re = f_re[..., None] * b_re - f_im[..., None] * b_im
    bb_im = f_re[..., None] * b_im + f_im[..., None] * b_re
    pr, pi = [jnp.ones_like(ab_re)], [jnp.zeros_like(ab_re)]
    for _ in range(T):
        pr.append(pr[-1] * ab_re - pi[-1] * ab_im)
        pi.append(pr[-2] * ab_im + pi[-1] * ab_re)
    pw_re = jnp.stack(pr)
    pw_im = jnp.stack(pi)
    ab_r = pw_re[:T, :, :, None] * bb_re[None] - pw_im[:T, :, :, None] * bb_im[None]
    ab_i = pw_re[:T, :, :, None] * bb_im[None] + pw_im[:T, :, :, None] * bb_re[None]
    taps = (jnp.einsum('gcp,tgpd->tgcd', c_re, ab_r, precision=hp)
            - jnp.einsum('gcp,tgpd->tgcd', c_im, ab_i, precision=hp))
    seq = jnp.concatenate([jnp.zeros((T,) + taps.shape[1:], F32), taps, jnp.zeros((1,) + taps.shape[1:], F32)])
    kt = jnp.tile(seq, (T, 1, 1, 1))[:2 * T * T].reshape((T, 2 * T) + taps.shape[1:])[:, T:]
    eye = jnp.eye(O, dtype=F32)
    kt7 = jnp.transpose(kt.reshape(T, T, N_OCT, O, SSM_GROUP, SSM_GROUP), (2, 0, 3, 5, 1, 4))
    kt8 = (kt7[:, :, :, :, :, None, :] * eye[None, None, :, None, None, :, None]).reshape(N_OCT, OCT_K, OCT_K)

    def inject(m):
        m6 = jnp.transpose(m[::-1].reshape(T, N_OCT, O, P, SSM_GROUP), (1, 0, 2, 4, 3))
        return (m6[:, :, :, :, None, :] * eye[None, None, :, None, :, None]).reshape(N_OCT, OCT_K, OCT_STATE)

    wst = jnp.concatenate([inject(ab_r), inject(ab_i)], axis=-1)

    def readout(ca):
        c6 = jnp.transpose(ca.reshape(T, N_OCT, O, SSM_GROUP, P), (1, 2, 4, 0, 3))
        return (c6[:, :, :, :, None, :] * eye[None, :, None, None, :, None]).reshape(N_OCT, OCT_STATE, OCT_K)

    ca_re = c_re[None] * pw_re[1:][:, :, None, :] - c_im[None] * pw_im[1:][:, :, None, :]
    ca_im = c_re[None] * pw_im[1:][:, :, None, :] + c_im[None] * pw_re[1:][:, :, None, :]
    return dict(kt=kt8.astype(BF16), wst=wst.astype(BF16),
                cp_re=readout(ca_re).astype(BF16), cp_im=readout(-ca_im).astype(BF16),
                at_re=pw_re[T], at_im=pw_im[T])


def _ssm(u3, mats, first, seq_id, h0_re, h0_im):
    sre, sim = _ssm_state(u3, mats['wst'])
    xsr, xsi, xer, xei = _ssm_carry(first, seq_id, sre, sim, mats['at_re'], mats['at_im'], h0_re, h0_im)
    y3 = _ssm_out(u3, mats['kt'], xsr, xsi, mats['cp_re'], mats['cp_im'])
    return y3, xer, xei


def _states_at(xe, idx):
    return jnp.transpose(xe[:, idx, :], (1, 0, 2)).reshape(idx.shape[0], SSM_GROUPS, SSM_STATE)


def _gelu_tanh(x):
    return 0.5 * x * (1.0 + jnp.tanh(math.sqrt(2.0 / math.pi) * (x + 0.044715 * (x * x * x))))


def _glu_kernel(y3_ref, u_ref, d_ref, w_ref, o_ref, wbf_ref, scr_ref):
    @pl.when(pl.program_id(0) == 0)
    def _():
        wbf_ref[...] = w_ref[...].astype(BF16)

    rows = y3_ref.shape[1]
    n_col = SSM_W // LANES
    for t in range(SSM_T):
        for c in range(n_col):
            scr_ref[c, pl.ds(t, rows, stride=SSM_T), :] = y3_ref[t, :, c * LANES:(c + 1) * LANES]
    y = jnp.concatenate([scr_ref[c] for c in range(n_col)], axis=1)
    ys = _gelu_tanh(y + d_ref[...] * u_ref[...])
    t = jnp.dot(ys.astype(BF16), wbf_ref[...], preferred_element_type=F32)
    o_ref[...] = (ys * jax.nn.sigmoid(t)).astype(o_ref.dtype)


def _glu(y3, z, d_all, w_all, l):
    nt = z.shape[0]
    tm = 512
    return pl.pallas_call(
        _glu_kernel,
        grid=(nt // tm,),
        in_specs=[
            pl.BlockSpec((SSM_T, tm // SSM_T, SSM_W), lambda i: (0, i, 0)),
            pl.BlockSpec((tm, SSM_W), lambda i: (i, 3 * ATT_W // SSM_W)),
            pl.BlockSpec((None, 1, SSM_W), lambda i: (l, 0, 0)),
            pl.BlockSpec((None, SSM_W, SSM_W), lambda i: (l, 0, 0)),
        ],
        out_specs=pl.BlockSpec((tm, SSM_W), lambda i: (i, 0)),
        out_shape=jax.ShapeDtypeStruct((nt, SSM_W), BF16),
        scratch_shapes=[pltpu.VMEM((SSM_W, SSM_W), BF16), pltpu.VMEM((SSM_W // LANES, tm, LANES), F32)],
        compiler_params=_cparams("arbitrary"),
        name="glu",
    )(y3, z, d_all.reshape(DEPTH, 1, SSM_W), w_all)


def _merge_kernel(o_ref, y_ref, ga_ref, gs_ref, wa_ref, ws_ref, m_ref):
    a = jnp.dot(o_ref[...], wa_ref[...].astype(BF16), preferred_element_type=F32)
    s = jnp.dot(y_ref[...], ws_ref[...].astype(BF16), preferred_element_type=F32)
    m_ref[...] = (jax.nn.sigmoid(ga_ref[...]) * a + jax.nn.sigmoid(gs_ref[...]) * s).astype(m_ref.dtype)


def _merge(o_p, o_s, ys2, z, w_ba_all, w_bs_all, l):
    nt = ys2.shape[0]
    tm, tn = 1024, 512
    ga0 = (3 * ATT_W + SSM_W) // tn
    gs0 = ga0 + D_MODEL // tn
    p_tiles = o_p.shape[0] // tm
    s_tiles = o_s.shape[0] // tm
    assert p_tiles * tm == o_p.shape[0] and s_tiles * tm == o_s.shape[0]

    def kern(op_ref, os_ref, *rest):
        i = pl.program_id(0)

        @pl.when(i < p_tiles)
        def _():
            _merge_kernel(op_ref, *rest)

        @pl.when(i >= p_tiles)
        def _():
            _merge_kernel(os_ref, *rest)

    return pl.pallas_call(
        kern,
        grid=(nt // tm, D_MODEL // tn),
        in_specs=[
            pl.BlockSpec((tm, ATT_W), lambda i, j: (jnp.minimum(i, p_tiles - 1), 0)),
            pl.BlockSpec((tm, ATT_W), lambda i, j: (jnp.maximum(i - p_tiles, 0), 0)),
            pl.BlockSpec((tm, SSM_W), lambda i, j: (i, 0)),
            pl.BlockSpec((tm, tn), lambda i, j: (i, ga0 + j)),
            pl.BlockSpec((tm, tn), lambda i, j: (i, gs0 + j)),
            pl.BlockSpec((None, ATT_W, tn), lambda i, j: (l, 0, j)),
            pl.BlockSpec((None, SSM_W, tn), lambda i, j: (l, 0, j)),
        ],
        out_specs=pl.BlockSpec((tm, tn), lambda i, j: (i, j)),
        out_shape=jax.ShapeDtypeStruct((nt, D_MODEL), BF16),
        compiler_params=_cparams("arbitrary", "arbitrary"),
        name="merge",
    )(o_p, o_s, ys2, z, z, w_ba_all, w_bs_all)


def _proj_residual_kernel(a_ref, w_ref, x_ref, g_ref, o_ref):
    r = jnp.dot(a_ref[...], w_ref[...].astype(BF16), preferred_element_type=F32)
    tm, tn = r.shape
    r3 = r.reshape(tm // CHUNK, CHUNK, tn) * g_ref[...]
    o_ref[...] = x_ref[...] + r3.reshape(tm, tn)


def _proj_residual(a, w_all, x, tab, l, which_gate):
    nt, k = a.shape
    tm, tn = 1024, 512
    cpt = tm // CHUNK
    return pl.pallas_call(
        _proj_residual_kernel,
        grid=(nt // tm, D_MODEL // tn),
        in_specs=[
            pl.BlockSpec((tm, k), lambda i, j: (i, 0)),
            pl.BlockSpec((None, k, tn), lambda i, j: (l, 0, j)),
            pl.BlockSpec((tm, tn), lambda i, j: (i, j)),
            _tab_spec(cpt, l, which_gate, width=tn, col=True),
        ],
        out_specs=pl.BlockSpec((tm, tn), lambda i, j: (i, j)),
        out_shape=jax.ShapeDtypeStruct((nt, D_MODEL), F32),
        compiler_params=_cparams("arbitrary", "arbitrary"),
        name="proj_residual",
    )(a, w_all, x, tab)


def _swiglu_step(h, wg_ref, wu_ref, wd_ref):
    hg = jnp.dot(h, wg_ref[...].astype(BF16), preferred_element_type=F32)
    hu = jnp.dot(h, wu_ref[...].astype(BF16), preferred_element_type=F32)
    a = (hg * jax.nn.sigmoid(hg)) * hu
    return jnp.dot(a.astype(BF16), wd_ref[...].astype(BF16), preferred_element_type=F32)


def _ffn_kernel(h_ref, wg_ref, wu_ref, wd_ref, o_ref):
    @pl.when(pl.program_id(1) == 0)
    def _():
        o_ref[...] = jnp.zeros_like(o_ref)

    o_ref[...] += _swiglu_step(h_ref[...], wg_ref, wu_ref, wd_ref)


def _ffn(h, w_gate, w_up, w_down, j):
    nt = h.shape[0]
    d_ff = w_gate.shape[-1]
    tm, tf = FFN_TM, FFN_TF
    return pl.pallas_call(
        _ffn_kernel,
        grid=(nt // tm, d_ff // tf),
        in_specs=[
            pl.BlockSpec((tm, D_MODEL), lambda i, f: (i, 0)),
            pl.BlockSpec((None, D_MODEL, tf), lambda i, f: (j, 0, f)),
            pl.BlockSpec((None, D_MODEL, tf), lambda i, f: (j, 0, f)),
            pl.BlockSpec((None, tf, D_MODEL), lambda i, f: (j, f, 0)),
        ],
        out_specs=pl.BlockSpec((tm, D_MODEL), lambda i, f: (i, 0)),
        out_shape=jax.ShapeDtypeStruct((nt, D_MODEL), F32),
        compiler_params=_cparams("arbitrary", "arbitrary"),
        name="ffn",
    )(h, w_gate, w_up, w_down)


def _moe_plan(route, nt):
    tm = FFN_TM
    n_assign = 2 * nt
    n_tiles = (n_assign + N_EXPERTS * (tm - 1)) // tm
    e_flat = jnp.concatenate([route[:, 0], route[:, 1]]).astype(jnp.int32)
    onehot = (e_flat[:, None] == jnp.arange(N_EXPERTS, dtype=jnp.int32)[None, :]).astype(jnp.int32)
    csum = jnp.cumsum(onehot, axis=0)
    counts = csum[-1]
    rank = jnp.sum((csum - onehot) * onehot, axis=1)
    tiles_per = (counts + tm - 1) // tm
    tile_end = jnp.cumsum(tiles_per)
    offsets = (tile_end - tiles_per) * tm
    dest = jnp.sum(onehot * offsets[None, :], axis=1) + rank
    tidx = jnp.arange(n_tiles, dtype=jnp.int32)
    tile_expert = jnp.minimum(jnp.sum((tidx[:, None] >= tile_end[None, :]).astype(jnp.int32), axis=1), N_EXPERTS - 1)
    tile_valid = (tidx < tile_end[-1]).astype(jnp.int32)
    tok = jnp.concatenate([jnp.arange(nt, dtype=jnp.int32)] * 2)
    src_token = jnp.zeros((n_tiles * tm,), jnp.int32).at[dest].set(tok)
    return dict(dest1=dest[:nt], dest2=dest[nt:], src_token=src_token,
                tile_expert=tile_expert.astype(jnp.int32), tile_valid=tile_valid)


def _row_copy(src_ref, row, dst_ref, r, sem):
    return pltpu.make_async_copy(src_ref.at[pl.ds(row, 1)], dst_ref.at[pl.ds(r, 1)], sem)


def _gather_start(idx_ref, base, src_ref, dst_ref, sem):
    rows = dst_ref.shape[0]

    def group(gi, c):
        for u in range(GATHER_UNROLL):
            r = gi * GATHER_UNROLL + u
            _row_copy(src_ref, idx_ref[base + r], dst_ref, r, sem).start(priority=u % 2)
        return c

    lax.fori_loop(0, rows // GATHER_UNROLL, group, 0)


def _gather_wait(src_ref, dst_ref, sem):
    pltpu.make_async_copy(src_ref.at[pl.ds(0, dst_ref.shape[0])], dst_ref, sem).wait()


def _gather_rows_kernel(idx_ref, src_ref, o_ref, buf_ref, sem):
    i = pl.program_id(0)
    n = pl.num_programs(0)
    slot = i % 2

    @pl.when(i == 0)
    def _():
        _gather_start(idx_ref, 0, src_ref, buf_ref.at[0], sem.at[0])

    @pl.when(i + 1 < n)
    def _():
        _gather_start(idx_ref, (i + 1) * GATHER_ROWS, src_ref, buf_ref.at[1 - slot], sem.at[1 - slot])

    _gather_wait(src_ref, buf_ref.at[slot], sem.at[slot])
    o_ref[...] = buf_ref[slot].astype(o_ref.dtype)


def _gather_rows(src, idx, out_dtype):
    n_out = idx.shape[0]
    width = src.shape[1]
    return pl.pallas_call(
        _gather_rows_kernel,
        grid_spec=pltpu.PrefetchScalarGridSpec(
            num_scalar_prefetch=1,
            grid=(n_out // GATHER_ROWS,),
            in_specs=[pl.BlockSpec(memory_space=pl.ANY)],
            out_specs=pl.BlockSpec((GATHER_ROWS, width), lambda i, *_: (i, 0)),
            scratch_shapes=[pltpu.VMEM((2, GATHER_ROWS, width), src.dtype), pltpu.SemaphoreType.DMA((2,))],
        ),
        out_shape=jax.ShapeDtypeStruct((n_out, width), out_dtype),
        compiler_params=_cparams("arbitrary"),
        name="moe_gather",
    )(idx, src)


def _moe_ffn_kernel(te_ref, tv_ref, h_ref, wg_ref, wu_ref, wd_ref, o_ref):
    t = pl.program_id(0)

    @pl.when(pl.program_id(1) == 0)
    def _():
        o_ref[...] = jnp.zeros_like(o_ref)

    @pl.when(tv_ref[t] == 1)
    def _():
        o_ref[...] += _swiglu_step(h_ref[...], wg_ref, wu_ref, wd_ref)


def _moe_ffn(h_sorted, tile_expert, tile_valid, w_gate, w_up, w_down, j):
    d_ff = w_gate.shape[-1]
    tm, tf = FFN_TM, FFN_TF
    n_tiles = h_sorted.shape[0] // tm
    nf = d_ff // tf

    def f_idx(t, f, tv):
        return jnp.where(tv[t] == 1, f, nf - 1)

    return pl.pallas_call(
        _moe_ffn_kernel,
        grid_spec=pltpu.PrefetchScalarGridSpec(
            num_scalar_prefetch=2,
            grid=(n_tiles, nf),
            in_specs=[
                pl.BlockSpec((tm, D_MODEL), lambda t, f, te, tv: (t, 0)),
                pl.BlockSpec((None, None, D_MODEL, tf), lambda t, f, te, tv: (j, te[t], 0, f_idx(t, f, tv))),
                pl.BlockSpec((None, None, D_MODEL, tf), lambda t, f, te, tv: (j, te[t], 0, f_idx(t, f, tv))),
                pl.BlockSpec((None, None, tf, D_MODEL), lambda t, f, te, tv: (j, te[t], f_idx(t, f, tv), 0)),
            ],
            out_specs=pl.BlockSpec((tm, D_MODEL), lambda t, f, te, tv: (t, 0)),
        ),
        out_shape=jax.ShapeDtypeStruct((n_tiles * tm, D_MODEL), F32),
        compiler_params=_cparams("arbitrary", "arbitrary"),
        name="moe_ffn",
    )(tile_expert, tile_valid, h_sorted, w_gate, w_up, w_down)


def _moe_combine_kernel(d1_ref, d2_ref, y_ref, x_ref, route_ref, g_ref, o_ref, a_ref, b_ref, sem_a, sem_b):
    i = pl.program_id(0)
    n = pl.num_programs(0)
    slot = i % 2

    def start(step, s):
        _gather_start(d1_ref, step * GATHER_ROWS, y_ref, a_ref.at[s], sem_a.at[s])
        _gather_start(d2_ref, step * GATHER_ROWS, y_ref, b_ref.at[s], sem_b.at[s])

    @pl.when(i == 0)
    def _():
        start(0, 0)

    @pl.when(i + 1 < n)
    def _():
        start(i + 1, 1 - slot)

    _gather_wait(y_ref, a_ref.at[slot], sem_a.at[slot])
    _gather_wait(y_ref, b_ref.at[slot], sem_b.at[slot])
    route = route_ref[...]
    r = route[:, 2:3] * a_ref[slot] + route[:, 3:4] * b_ref[slot]
    tm, n_col = r.shape
    o_ref[...] = x_ref[...] + (r.reshape(tm // CHUNK, CHUNK, n_col) * g_ref[...]).reshape(tm, n_col)


def _moe_combine(y_sorted, dest1, dest2, x, route, tab, l, which_gate):
    nt = x.shape[0]
    tm = GATHER_ROWS
    blk = pl.BlockSpec((tm, D_MODEL), lambda i, *_: (i, 0))
    buf = pltpu.VMEM((2, tm, D_MODEL), F32)
    return pl.pallas_call(
        _moe_combine_kernel,
        grid_spec=pltpu.PrefetchScalarGridSpec(
            num_scalar_prefetch=2,
            grid=(nt // tm,),
            in_specs=[pl.BlockSpec(memory_space=pl.ANY), blk, pl.BlockSpec((tm, LANES), lambda i, *_: (i, 0)),
                      _tab_spec(tm // CHUNK, l, which_gate)],
            out_specs=blk,
            scratch_shapes=[buf, buf, pltpu.SemaphoreType.DMA((2,)), pltpu.SemaphoreType.DMA((2,))],
        ),
        out_shape=jax.ShapeDtypeStruct((nt, D_MODEL), F32),
        compiler_params=_cparams("arbitrary"),
        name="moe_combine",
    )(dest1, dest2, y_sorted, x, route, tab)


def _residual_kernel(x_ref, r_ref, g_ref, o_ref):
    r = r_ref[...]
    tm, n = r.shape
    o_ref[...] = x_ref[...] + (r.reshape(tm // CHUNK, CHUNK, n) * g_ref[...]).reshape(tm, n)


def _residual(x, r, tab, l, which_gate):
    nt = x.shape[0]
    tm = 512
    blk = pl.BlockSpec((tm, D_MODEL), lambda i: (i, 0))
    return pl.pallas_call(
        _residual_kernel, grid=(nt // tm,),
        in_specs=[blk, blk, _tab_spec(tm // CHUNK, l, which_gate)],
        out_specs=blk, out_shape=jax.ShapeDtypeStruct((nt, D_MODEL), F32),
        compiler_params=_cparams("arbitrary"), name="residual",
    )(x, r, tab)


def _final_norm_kernel(x_ref, g_ref, o_ref):
    x = x_ref[...]
    o_ref[...] = x * lax.rsqrt(jnp.mean(x * x, axis=-1, keepdims=True) + EPS) * g_ref[...]


def _final_norm(x, g):
    nt = x.shape[0]
    tm = 512
    blk = pl.BlockSpec((tm, D_MODEL), lambda i: (i, 0))
    return pl.pallas_call(
        _final_norm_kernel, grid=(nt // tm,),
        in_specs=[blk, pl.BlockSpec((1, D_MODEL), lambda i: (0, 0))],
        out_specs=blk, out_shape=jax.ShapeDtypeStruct((nt, D_MODEL), F32),
        compiler_params=_cparams("arbitrary"), name="final_norm",
    )(x, g.reshape(1, D_MODEL))


def kernel(x_prompt, x_sample, cache_k, cache_v, state_ssm_re, state_ssm_im, c_prompt, c_sample, w_ada, b_ada, g_norm_mix, g_norm_ffn, g_norm_final, w_in, rel_bias, ssm_a_re, ssm_a_im, ssm_log_step, ssm_b_re, ssm_b_im, ssm_c_re, ssm_c_im, ssm_d, w_ssm_glu, w_branch_attn, w_branch_ssm, w_out, w_ffn_gate, w_ffn_up, w_ffn_down, w_router, b_router, w_exp_gate, w_exp_up, w_exp_down):
    n_pb, seq = x_prompt.shape[0], x_prompt.shape[1]
    n_sb, dec = x_sample.shape[0], x_sample.shape[1]
    assert dec == CHUNK and cache_k.shape[2] == PAST_WINDOW and seq % ATT_ROWS == 0
    np_tok = n_pb * seq
    nt = np_tok + n_sb * dec
    n_batch = n_pb + n_sb

    x = jnp.concatenate([x_prompt.reshape(np_tok, D_MODEL), x_sample.reshape(n_sb * dec, D_MODEL)], axis=0)

    c_all = jnp.concatenate([c_prompt, c_sample], axis=0)
    c_pad = jnp.pad(c_all, ((0, -n_batch % 8), (0, 0)))
    ada = _ada(c_pad, w_ada, b_ada)
    ada_p = jnp.broadcast_to(ada[:, :n_pb, None, :], (DEPTH, n_pb, seq // CHUNK, 6 * D_MODEL))
    tab = jnp.concatenate([ada_p.reshape(DEPTH, np_tok // CHUNK, 6 * D_MODEL), ada[:, n_pb:n_batch]], axis=1)
    tab = tab.reshape(DEPTH, nt // CHUNK, 6, 1, D_MODEL)

    sub_p, sub_s = seq // SSM_T, dec // SSM_T
    seq_id = jnp.concatenate([jnp.repeat(jnp.arange(n_pb), sub_p), n_pb + jnp.repeat(jnp.arange(n_sb), sub_s)])
    seq_id = seq_id.astype(jnp.int32)
    first = jnp.concatenate([jnp.arange(n_pb * sub_p) % sub_p == 0, jnp.arange(n_sb * sub_s) % sub_s == 0])
    first = first.astype(jnp.int32)
    last_p = (jnp.arange(n_pb) + 1) * sub_p - 1
    last_s = n_pb * sub_p + (jnp.arange(n_sb) + 1) * sub_s - 1
    zeros_h0 = jnp.zeros((n_pb, SSM_GROUPS, SSM_STATE), F32)

    outs = {k: [] for k in ('kp', 'vp', 'srp', 'sip', 'ks', 'vs', 'srs', 'sis')}
    for l in range(DEPTH):
        h = _norm_mod(x, g_norm_mix, tab, l, SHIFT1, SCALE1)
        z, u3 = _in_proj(h, w_in, l)

        b64 = _attention_bias(rel_bias[l])
        o_p = _attn_prompt(z, b64, n_pb, seq)
        o_s = _attn_sample(z, cache_k, cache_v, l, b64, np_tok, n_sb)

        mats = _ssm_matrices(ssm_a_re[l], ssm_a_im[l], ssm_log_step[l], ssm_b_re[l], ssm_b_im[l],
                             ssm_c_re[l], ssm_c_im[l])
        h0_re = jnp.concatenate([zeros_h0, state_ssm_re[l].astype(F32)], axis=0)
        h0_im = jnp.concatenate([zeros_h0, state_ssm_im[l].astype(F32)], axis=0)
        y3, xe_re, xe_im = _ssm(u3, mats, first, seq_id, h0_re, h0_im)

        ys2 = _glu(y3, z, ssm_d, w_ssm_glu, l)
        merged = _merge(o_p, o_s, ys2, z, w_branch_attn, w_branch_ssm, l)
        x = _proj_residual(merged, w_out, x, tab, l, GATE1)

        j = l // 2
        if l % 2 == 0:
            h2 = _norm_mod(x, g_norm_ffn, tab, l, SHIFT2, SCALE2)
            r = _ffn(h2, w_ffn_gate, w_ffn_up, w_ffn_down, j)
            x = _residual(x, r, tab, l, GATE2)
        else:
            h2, route = _norm_mod(x, g_norm_ffn, tab, l, SHIFT2, SCALE2, router=(w_router[j], b_router[j]))
            plan = _moe_plan(route, nt)
            h_sorted = _gather_rows(h2, plan['src_token'], BF16)
            y_sorted = _moe_ffn(h_sorted, plan['tile_expert'], plan['tile_valid'], w_exp_gate, w_exp_up, w_exp_down, j)
            x = _moe_combine(y_sorted, plan['dest1'], plan['dest2'], x, route, tab, l, GATE2)

        keep = min(PAST_WINDOW, seq)
        kv_p = z[:np_tok, ATT_W:3 * ATT_W].reshape(n_pb, seq, 2, N_HEADS, HEAD_DIM)[:, seq - keep:]
        kv_s = z[np_tok:, ATT_W:3 * ATT_W].reshape(n_sb, dec, 2, N_HEADS, HEAD_DIM)
        outs['kp'].append(kv_p[:, :, 0])
        outs['vp'].append(kv_p[:, :, 1])
        outs['ks'].append(kv_s[:, :, 0])
        outs['vs'].append(kv_s[:, :, 1])
        outs['srp'].append(_states_at(xe_re, last_p))
        outs['sip'].append(_states_at(xe_im, last_p))
        outs['srs'].append(_states_at(xe_re, last_s))
        outs['sis'].append(_states_at(xe_im, last_s))

    yn = _final_norm(x, g_norm_final)
    y_prompt = yn[:np_tok].reshape(n_pb, seq, D_MODEL)
    y_sample = yn[np_tok:].reshape(n_sb, dec, D_MODEL)
    st = lambda name: jnp.stack(outs[name])
    return (y_prompt, y_sample, st('kp'), st('vp'), st('srp'), st('sip'),
            st('ks'), st('vs'), st('srs'), st('sis'))
```

```python
import functools
import math

import jax
import jax.numpy as jnp
from jax import lax
from jax.experimental import pallas as pl
from jax.experimental.pallas import tpu as pltpu

F32 = jnp.float32
BF16 = jnp.bfloat16

D_MODEL = 2048
DEPTH = 2
CHUNK = 64
PAST_CHUNKS = 8
PAST_WINDOW = PAST_CHUNKS * CHUNK
BAND = PAST_WINDOW + CHUNK
N_HEADS = 8
HEAD_DIM = 128
ATT_W = N_HEADS * HEAD_DIM
ATT_SCALE = HEAD_DIM ** -0.5
REL_CLIP = 256
SSM_W = 1024
SSM_GROUP = 16
SSM_GROUPS = SSM_W // SSM_GROUP
SSM_STATE = 64
N_EXPERTS = 8
IN_COLS = 3 * ATT_W + SSM_W + 2 * D_MODEL
EPS = 1e-6
NEG_INF = -1e30

LANES = 128
VMEM_LIMIT_BYTES = 56 * 1024 * 1024

SSM_T = 16
SSM_OCT = LANES // SSM_GROUP
N_OCT = SSM_GROUPS // SSM_OCT
OCT_K = SSM_T * LANES
OCT_STATE = SSM_OCT * SSM_STATE
ATT_GROUP = 4
ATT_ROWS = ATT_GROUP * CHUNK
ATT_KEYS = 3 * ATT_ROWS

IN_TM, IN_TN = 1024, 512
FFN_TM = 768
FFN_TF = 512
GATHER_ROWS = 256
GATHER_UNROLL = 8

SHIFT1, SCALE1, GATE1, SHIFT2, SCALE2, GATE2 = range(6)


def _cparams(*sem):
    return pltpu.CompilerParams(dimension_semantics=sem, vmem_limit_bytes=VMEM_LIMIT_BYTES)


def _tab_spec(cpt, l, which, width=D_MODEL, col=None):
    if col is None:
        return pl.BlockSpec((None, cpt, None, 1, width), lambda i, *_: (l, i, which, 0, 0))
    return pl.BlockSpec((None, cpt, None, 1, width), lambda i, j, *_: (l, i, which, 0, j))


def _ada_kernel(c_ref, w_ref, b_ref, o_ref):
    c = c_ref[...]
    a = (c * jax.nn.sigmoid(c)).astype(BF16)
    o_ref[...] = jnp.dot(a, w_ref[...].astype(BF16), preferred_element_type=F32) + b_ref[...]


def _ada(c_pad, w_ada, b_ada):
    rows = c_pad.shape[0]
    n = w_ada.shape[-1]
    tn = 1024
    return pl.pallas_call(
        _ada_kernel,
        grid=(DEPTH, n // tn),
        in_specs=[
            pl.BlockSpec((rows, D_MODEL), lambda l, j: (0, 0)),
            pl.BlockSpec((None, D_MODEL, tn), lambda l, j: (l, 0, j)),
            pl.BlockSpec((None, 1, tn), lambda l, j: (l, 0, j)),
        ],
        out_specs=pl.BlockSpec((None, rows, tn), lambda l, j: (l, 0, j)),
        out_shape=jax.ShapeDtypeStruct((DEPTH, rows, n), F32),
        compiler_params=_cparams("arbitrary", "arbitrary"),
        name="ada",
    )(c_pad, w_ada, b_ada.reshape(DEPTH, 1, n))


def _modulated(x_ref, g_ref, sh_ref, sc_ref):
    return _modulate(x_ref[...], g_ref, sh_ref, sc_ref)


def _modulate(x, g_ref, sh_ref, sc_ref):
    tm = x.shape[0]
    y = x * lax.rsqrt(jnp.mean(x * x, axis=-1, keepdims=True) + EPS) * g_ref[...]
    y3 = y.reshape(tm // CHUNK, CHUNK, D_MODEL)
    h = y3 * (1.0 + sc_ref[...]) + sh_ref[...]
    return h.reshape(tm, D_MODEL)


def _norm_mod_kernel(x_ref, g_ref, sh_ref, sc_ref, h_ref):
    h_ref[...] = _modulated(x_ref, g_ref, sh_ref, sc_ref).astype(BF16)


def _split_bf16(a):
    hi = a.astype(BF16)
    lo = (a - hi.astype(F32)).astype(BF16)
    return hi, lo


def _norm_mod_router_kernel(x_ref, g_ref, sh_ref, sc_ref, wr_ref, br_ref, h_ref, route_ref):
    h = _modulated(x_ref, g_ref, sh_ref, sc_ref)
    for c in range(D_MODEL // LANES):
        h_ref[:, c, :] = h[:, c * LANES:(c + 1) * LANES]
    h_hi, h_lo = _split_bf16(h)
    w_hi, w_lo = _split_bf16(wr_ref[...])
    dot = functools.partial(jnp.dot, preferred_element_type=F32)
    logits = (dot(h_hi, w_hi) + (dot(h_hi, w_lo) + dot(h_lo, w_hi)) + dot(h_lo, w_lo)) + br_ref[...]
    lane = lax.broadcasted_iota(jnp.int32, logits.shape, 1).astype(F32)
    lg = jnp.where(lane < N_EXPERTS, logits, -jnp.inf)
    m1 = jnp.max(lg, axis=-1, keepdims=True)
    i1 = jnp.min(jnp.where(lg == m1, lane, float(LANES)), axis=-1, keepdims=True)
    lg2 = jnp.where(lane == i1, -jnp.inf, lg)
    m2 = jnp.max(lg2, axis=-1, keepdims=True)
    i2 = jnp.min(jnp.where(lg2 == m2, lane, float(LANES)), axis=-1, keepdims=True)
    e2 = jnp.exp(m2 - m1)
    den = 1.0 + e2
    route = jnp.where(lane == 0.0, i1, jnp.where(lane == 1.0, i2, jnp.where(lane == 2.0, 1.0 / den, e2 / den)))
    route_ref[...] = jnp.where(lane < 4.0, route, 0.0)


def _norm_mod(x, g_all, tab, l, which_shift, which_scale, router=None):
    nt = x.shape[0]
    tm = 512
    cpt = tm // CHUNK
    in_specs = [
        pl.BlockSpec((tm, D_MODEL), lambda i: (i, 0)),
        pl.BlockSpec((None, 1, D_MODEL), lambda i: (l, 0, 0)),
        _tab_spec(cpt, l, which_shift),
        _tab_spec(cpt, l, which_scale),
    ]
    h_spec = pl.BlockSpec((tm, D_MODEL), lambda i: (i, 0))
    args = [x, g_all.reshape(DEPTH, 1, D_MODEL), tab, tab]
    if router is None:
        return pl.pallas_call(
            _norm_mod_kernel, grid=(nt // tm,), in_specs=in_specs, out_specs=h_spec,
            out_shape=jax.ShapeDtypeStruct((nt, D_MODEL), BF16),
            compiler_params=_cparams("arbitrary"), name="norm_mod",
        )(*args)
    w_r, b_r = router
    w_pad = jnp.zeros((D_MODEL, LANES), F32).at[:, :N_EXPERTS].set(w_r)
    b_pad = jnp.zeros((1, LANES), F32).at[0, :N_EXPERTS].set(b_r)
    in_specs += [
        pl.BlockSpec((D_MODEL, LANES), lambda i: (0, 0)),
        pl.BlockSpec((1, LANES), lambda i: (0, 0)),
    ]
    n_col = D_MODEL // LANES
    return pl.pallas_call(
        _norm_mod_router_kernel, grid=(nt // tm,), in_specs=in_specs,
        out_specs=[pl.BlockSpec((tm, n_col, LANES), lambda i: (i, 0, 0)), pl.BlockSpec((tm, LANES), lambda i: (i, 0))],
        out_shape=[jax.ShapeDtypeStruct((nt, n_col, LANES), F32), jax.ShapeDtypeStruct((nt, LANES), F32)],
        compiler_params=_cparams("arbitrary"), name="norm_mod_router",
    )(*args, w_pad, b_pad)


def _in_proj_kernel(x_ref, w_ref, z_ref, u3_ref, scr_ref):
    j = pl.program_id(1)
    r = jnp.dot(x_ref[...], w_ref[...].astype(BF16), preferred_element_type=F32)
    z_ref[...] = r
    tm, tn = r.shape
    u_first = 3 * ATT_W // tn
    for part in range(SSM_W // tn):
        @pl.when(j == u_first + part)
        def _():
            for c in range(tn // LANES):
                scr_ref[c] = r[:, c * LANES:(c + 1) * LANES]
            for s in range(SSM_T):
                for c in range(tn // LANES):
                    lo = part * tn + c * LANES
                    rows = scr_ref[c, pl.ds(s, tm // SSM_T, stride=SSM_T), :]
                    u3_ref[s, :, lo:lo + LANES] = rows.astype(BF16)


def _in_proj(h, w_all, l):
    nt, k = h.shape
    n = w_all.shape[-1]
    tm, tn = IN_TM, IN_TN
    return pl.pallas_call(
        _in_proj_kernel,
        grid=(nt // tm, n // tn),
        in_specs=[
            pl.BlockSpec((tm, k), lambda i, j: (i, 0)),
            pl.BlockSpec((None, k, tn), lambda i, j: (l, 0, j)),
        ],
        out_specs=[
            pl.BlockSpec((tm, tn), lambda i, j: (i, j)),
            pl.BlockSpec((SSM_T, tm // SSM_T, SSM_W), lambda i, j: (0, i, 0)),
        ],
        out_shape=[
            jax.ShapeDtypeStruct((nt, n), F32),
            jax.ShapeDtypeStruct((SSM_T, nt // SSM_T, SSM_W), BF16),
        ],
        scratch_shapes=[pltpu.VMEM((tn // LANES, tm, LANES), F32)],
        compiler_params=_cparams("arbitrary", "arbitrary"),
        name="in_proj",
    )(h, w_all)


def _softmax_rows(s):
    m = jnp.max(s, axis=-1, keepdims=True)
    e = jnp.exp(s - m)
    return e / jnp.sum(e, axis=-1, keepdims=True)


def _attn_prompt_kernel(q_ref, k0_ref, k1_ref, k2_ref, v0_ref, v1_ref, v2_ref, b64_ref, o_ref, bias_ref):
    g = pl.program_id(1)

    @pl.when((pl.program_id(0) == 0) & (g == 0))
    def _():
        bias_ref[...] = jnp.full(bias_ref.shape, NEG_INF, F32)
        for c in range(ATT_GROUP):
            bias_ref[:, c * CHUNK:(c + 1) * CHUNK, c * CHUNK:c * CHUNK + BAND] = b64_ref[...]

    col = lax.broadcasted_iota(jnp.int32, (ATT_ROWS, ATT_KEYS), 1)
    visible = (g * ATT_ROWS - 2 * ATT_ROWS + col) >= 0
    for h in range(N_HEADS):
        sl = slice(h * HEAD_DIM, (h + 1) * HEAD_DIM)
        qh = q_ref[:, sl].astype(BF16)
        kh = jnp.concatenate([k0_ref[:, sl], k1_ref[:, sl], k2_ref[:, sl]], axis=0).astype(BF16)
        vh = jnp.concatenate([v0_ref[:, sl], v1_ref[:, sl], v2_ref[:, sl]], axis=0).astype(BF16)
        s = lax.dot_general(qh, kh, (((1,), (1,)), ((), ())), preferred_element_type=F32)
        s = s * ATT_SCALE + bias_ref[h]
        s = jnp.where(visible, s, NEG_INF)
        p = _softmax_rows(s).astype(BF16)
        o_ref[:, sl] = jnp.dot(p, vh, preferred_element_type=F32).astype(o_ref.dtype)


def _attn_prompt(z, b64, n_batch, seq):
    groups = seq // ATT_ROWS
    blk = (ATT_ROWS, ATT_W)

    def kv_spec(col, back):
        return pl.BlockSpec(blk, lambda b, g: (b * groups + jnp.maximum(g - back, 0), col))

    return pl.pallas_call(
        _attn_prompt_kernel,
        grid=(n_batch, groups),
        in_specs=[
            pl.BlockSpec(blk, lambda b, g: (b * groups + g, 0)),
            kv_spec(1, 2), kv_spec(1, 1), kv_spec(1, 0),
            kv_spec(2, 2), kv_spec(2, 1), kv_spec(2, 0),
            pl.BlockSpec((N_HEADS, CHUNK, BAND), lambda b, g: (0, 0, 0)),
        ],
        out_specs=pl.BlockSpec(blk, lambda b, g: (b * groups + g, 0)),
        out_shape=jax.ShapeDtypeStruct((n_batch * seq, ATT_W), BF16),
        scratch_shapes=[pltpu.VMEM((N_HEADS, ATT_ROWS, ATT_KEYS), F32)],
        compiler_params=_cparams("arbitrary", "arbitrary"),
        name="attn_prompt",
    )(z, z, z, z, z, z, z, b64)


def _attn_sample_kernel(q_ref, kn_ref, vn_ref, kc_ref, vc_ref, bias_ref, o_ref):
    for h in range(N_HEADS):
        sl = slice(h * HEAD_DIM, (h + 1) * HEAD_DIM)
        qh = q_ref[:, sl].astype(BF16)
        kh = jnp.concatenate([kc_ref[:, h, :], kn_ref[:, sl]], axis=0).astype(BF16)
        vh = jnp.concatenate([vc_ref[:, h, :], vn_ref[:, sl]], axis=0).astype(BF16)
        s = lax.dot_general(qh, kh, (((1,), (1,)), ((), ())), preferred_element_type=F32)
        s = s * ATT_SCALE + bias_ref[h]
        p = _softmax_rows(s).astype(BF16)
        o_ref[:, sl] = jnp.dot(p, vh, preferred_element_type=F32).astype(o_ref.dtype)


def _attn_sample(z, cache_k, cache_v, l, b64, row0, n_batch):
    base = row0 // CHUNK
    blk = (CHUNK, ATT_W)
    cache_spec = pl.BlockSpec((None, None, PAST_WINDOW, N_HEADS, HEAD_DIM), lambda b: (l, b, 0, 0, 0))
    return pl.pallas_call(
        _attn_sample_kernel,
        grid=(n_batch,),
        in_specs=[
            pl.BlockSpec(blk, lambda b: (base + b, 0)),
            pl.BlockSpec(blk, lambda b: (base + b, 1)),
            pl.BlockSpec(blk, lambda b: (base + b, 2)),
            cache_spec, cache_spec,
            pl.BlockSpec((N_HEADS, CHUNK, BAND), lambda b: (0, 0, 0)),
        ],
        out_specs=pl.BlockSpec(blk, lambda b: (b, 0)),
        out_shape=jax.ShapeDtypeStruct((n_batch * CHUNK, ATT_W), BF16),
        compiler_params=_cparams("arbitrary"),
        name="attn_sample",
    )(z, z, z, cache_k, cache_v, b64)


def _attention_bias(rel_bias):
    rev = rel_bias[::-1].astype(F32)
    flat = PAST_WINDOW - REL_CLIP + CHUNK - 1
    vec = jnp.concatenate([jnp.broadcast_to(rev[:1], (flat, N_HEADS)), rev[:BAND + CHUNK - 1 - flat]], axis=0)
    rows = [vec[CHUNK - 1 - qi:CHUNK - 1 - qi + BAND] for qi in range(CHUNK)]
    return jnp.transpose(jnp.stack(rows), (2, 0, 1))


def _octet_lhs(u3_ref):
    return jnp.concatenate([u3_ref[s] for s in range(SSM_T)], axis=1)


def _ssm_state_kernel(u3_ref, w_ref, sre_ref, sim_ref):
    s = jnp.dot(_octet_lhs(u3_ref), w_ref[...], preferred_element_type=F32)
    sre_ref[...] = s[:, :OCT_STATE]
    sim_ref[...] = s[:, OCT_STATE:]


def _ssm_state(u3, wst):
    n_sub = u3.shape[1]
    out = jax.ShapeDtypeStruct((N_OCT, n_sub, OCT_STATE), F32)
    return pl.pallas_call(
        _ssm_state_kernel,
        grid=(N_OCT,),
        in_specs=[
            pl.BlockSpec((SSM_T, n_sub, LANES), lambda j: (0, 0, j)),
            pl.BlockSpec((None, OCT_K, 2 * OCT_STATE), lambda j: (j, 0, 0)),
        ],
        out_specs=[pl.BlockSpec((None, n_sub, OCT_STATE), lambda j: (j, 0, 0))] * 2,
        out_shape=[out, out],
        compiler_params=_cparams("arbitrary"),
        name="ssm_state",
    )(u3, wst)


def _ssm_carry_kernel(first_ref, seq_ref, sre_ref, sim_ref, ar_ref, ai_ref, h0r_ref, h0i_ref,
                      xsr_ref, xsi_ref, xer_ref, xei_ref, st_ref):
    blk = pl.program_id(0)
    steps = sre_ref.shape[1]

    @pl.when(blk == 0)
    def _():
        st_ref[...] = jnp.zeros_like(st_ref)

    ar = ar_ref[...]
    ai = ai_ref[...]

    def body(kk, carry):
        k = blk * steps + kk
        is_first = first_ref[k] == 1
        sq = seq_ref[k]
        xr = jnp.where(is_first, h0r_ref[sq], st_ref[0])
        xi = jnp.where(is_first, h0i_ref[sq], st_ref[1])
        xsr_ref[:, kk, :] = xr
        xsi_ref[:, kk, :] = xi
        nr = ar * xr - ai * xi + sre_ref[:, kk, :]
        ni = ar * xi + ai * xr + sim_ref[:, kk, :]
        xer_ref[:, kk, :] = nr
        xei_ref[:, kk, :] = ni
        st_ref[0] = nr
        st_ref[1] = ni
        return carry

    lax.fori_loop(0, steps, body, 0)


def _ssm_carry(first, seq_id, sre, sim, at_re, at_im, h0_re, h0_im):
    n_sub = sre.shape[1]
    steps = 64
    tile = (N_OCT, OCT_STATE)
    blk = pl.BlockSpec((N_OCT, steps, OCT_STATE), lambda i, *_: (0, i, 0))
    const = pl.BlockSpec(tile, lambda i, *_: (0, 0))
    tab = pl.BlockSpec((h0_re.shape[0],) + tile, lambda i, *_: (0, 0, 0))
    out = jax.ShapeDtypeStruct((N_OCT, n_sub, OCT_STATE), F32)
    return pl.pallas_call(
        _ssm_carry_kernel,
        grid_spec=pltpu.PrefetchScalarGridSpec(
            num_scalar_prefetch=2,
            grid=(n_sub // steps,),
            in_specs=[blk, blk, const, const, tab, tab],
            out_specs=[blk, blk, blk, blk],
            scratch_shapes=[pltpu.VMEM((2,) + tile, F32)],
        ),
        out_shape=[out, out, out, out],
        compiler_params=_cparams("arbitrary"),
        name="ssm_carry",
    )(first, seq_id, sre, sim, at_re.reshape(tile), at_im.reshape(tile),
      h0_re.reshape((-1,) + tile), h0_im.reshape((-1,) + tile))


def _ssm_out_kernel(u3_ref, taps_ref, xr_ref, xi_ref, cr_ref, ci_ref, y3_ref, kt_ref):
    @pl.when(pl.program_id(0) == 0)
    def _():
        kt_ref[...] = jnp.zeros_like(kt_ref)

    for s in range(SSM_T):
        for t in range(s, SSM_T):
            kt_ref[s * LANES:(s + 1) * LANES, t * LANES:(t + 1) * LANES] = taps_ref[t - s]

    y = jnp.dot(_octet_lhs(u3_ref), kt_ref[...], preferred_element_type=F32)
    y += jnp.dot(xr_ref[...].astype(BF16), cr_ref[...], preferred_element_type=F32)
    y += jnp.dot(xi_ref[...].astype(BF16), ci_ref[...], preferred_element_type=F32)
    for t in range(SSM_T):
        y3_ref[t] = y[:, t * LANES:(t + 1) * LANES]


def _ssm_out(u3, kt, xs_re, xs_im, cp_re, cp_im):
    n_sub = u3.shape[1]
    u_spec = pl.BlockSpec((SSM_T, n_sub, LANES), lambda j: (0, 0, j))
    x_spec = pl.BlockSpec((None, n_sub, OCT_STATE), lambda j: (j, 0, 0))
    c_spec = pl.BlockSpec((None, OCT_STATE, OCT_K), lambda j: (j, 0, 0))
    return pl.pallas_call(
        _ssm_out_kernel,
        grid=(N_OCT,),
        in_specs=[u_spec, pl.BlockSpec((None, SSM_T, LANES, LANES), lambda j: (j, 0, 0, 0)),
                  x_spec, x_spec, c_spec, c_spec],
        out_specs=u_spec,
        out_shape=jax.ShapeDtypeStruct((SSM_T, n_sub, SSM_W), F32),
        scratch_shapes=[pltpu.VMEM((OCT_K, OCT_K), BF16)],
        compiler_params=_cparams("arbitrary"),
        name="ssm_out",
    )(u3, kt, xs_re, xs_im, cp_re, cp_im)


def _ssm_matrices(a_re, a_im, log_step, b_re, b_im, c_re, c_im):
    hp = lax.Precision.HIGHEST
    G, P, T, O = SSM_GROUPS, SSM_STATE, SSM_T, SSM_OCT
    step = jnp.exp(log_step.astype(F32))[:, None]
    mag = jnp.exp(a_re * step)
    ang = a_im * step
    ab_re = mag * jnp.cos(ang)
    ab_im = mag * jnp.sin(ang)
    den = a_re * a_re + a_im * a_im
    n_re = ab_re - 1.0
    f_re = (n_re * a_re + ab_im * a_im) / den
    f_im = (ab_im * a_re - n_re * a_im) / den
    bb_re = f_re[..., None] * b_re - f_im[..., None] * b_im
    bb_im = f_re[..., None] * b_im + f_im[..., None] * b_re
    pr, pi = [jnp.ones_like(ab_re)], [jnp.zeros_like(ab_re)]
    for _ in range(T):
        pr.append(pr[-1] * ab_re - pi[-1] * ab_im)
        pi.append(pr[-2] * ab_im + pi[-1] * ab_re)
    pw_re = jnp.stack(pr)
    pw_im = jnp.stack(pi)
    ab_r = pw_re[:T, :, :, None] * bb_re[None] - pw_im[:T, :, :, None] * bb_im[None]
    ab_i = pw_re[:T, :, :, None] * bb_im[None] + pw_im[:T, :, :, None] * bb_re[None]
    taps = (jnp.einsum('gcp,tgpd->tgcd', c_re, ab_r, precision=hp)
            - jnp.einsum('gcp,tgpd->tgcd', c_im, ab_i, precision=hp))

    def blockdiag(x, a_dim, b_dim):
        rep = jnp.tile(jnp.eye(b_dim, dtype=F32), (1, O))
        row_g = jnp.arange(O * a_dim) // a_dim
        col_h = jnp.arange(O * b_dim) // b_dim
        return jnp.where(row_g[:, None] == col_h[None, :], jnp.dot(x, rep, precision=hp), 0.0)

    tap_rows = jnp.transpose(taps.reshape(T, N_OCT, O, SSM_GROUP, SSM_GROUP), (1, 0, 2, 4, 3))
    tap_blocks = blockdiag(tap_rows.reshape(N_OCT, T, LANES, SSM_GROUP), SSM_GROUP, SSM_GROUP)

    def inject(m):
        m5 = jnp.transpose(m[::-1].reshape(T, N_OCT, O, P, SSM_GROUP), (1, 0, 2, 4, 3))
        return blockdiag(m5.reshape(N_OCT, T, LANES, P), SSM_GROUP, P).reshape(N_OCT, OCT_K, OCT_STATE)

    wst = jnp.concatenate([inject(ab_r), inject(ab_i)], axis=-1)

    def readout(ca):
        c5 = jnp.transpose(ca.reshape(T, N_OCT, O, SSM_GROUP, P), (1, 0, 2, 4, 3))
        blocks = blockdiag(c5.reshape(N_OCT, T, OCT_STATE, SSM_GROUP), P, SSM_GROUP)
        return jnp.transpose(blocks, (0, 2, 1, 3)).reshape(N_OCT, OCT_STATE, OCT_K)

    ca_re = c_re[None] * pw_re[1:][:, :, None, :] - c_im[None] * pw_im[1:][:, :, None, :]
    ca_im = c_re[None] * pw_im[1:][:, :, None, :] + c_im[None] * pw_re[1:][:, :, None, :]
    return dict(taps=tap_blocks.astype(BF16), wst=wst.astype(BF16),
                cp_re=readout(ca_re).astype(BF16), cp_im=readout(-ca_im).astype(BF16),
                at_re=pw_re[T], at_im=pw_im[T])


def _ssm(u3, mats, first, seq_id, h0_re, h0_im):
    sre, sim = _ssm_state(u3, mats['wst'])
    xsr, xsi, xer, xei = _ssm_carry(first, seq_id, sre, sim, mats['at_re'], mats['at_im'], h0_re, h0_im)
    y3 = _ssm_out(u3, mats['taps'], xsr, xsi, mats['cp_re'], mats['cp_im'])
    return y3, xer, xei


def _states_at(xe, idx):
    return jnp.transpose(xe[:, idx, :], (1, 0, 2)).reshape(idx.shape[0], SSM_GROUPS, SSM_STATE)


def _gelu_tanh(x):
    return 0.5 * x * (1.0 + jnp.tanh(math.sqrt(2.0 / math.pi) * (x + 0.044715 * (x * x * x))))


def _glu_kernel(y3_ref, u_ref, d_ref, w_ref, o_ref, wbf_ref, scr_ref):
    @pl.when(pl.program_id(0) == 0)
    def _():
        wbf_ref[...] = w_ref[...].astype(BF16)

    rows = y3_ref.shape[1]
    n_col = SSM_W // LANES
    for t in range(SSM_T):
        for c in range(n_col):
            scr_ref[c, pl.ds(t, rows, stride=SSM_T), :] = y3_ref[t, :, c * LANES:(c + 1) * LANES]
    y = jnp.concatenate([scr_ref[c] for c in range(n_col)], axis=1)
    ys = _gelu_tanh(y + d_ref[...] * u_ref[...])
    t = jnp.dot(ys.astype(BF16), wbf_ref[...], preferred_element_type=F32)
    o_ref[...] = (ys * jax.nn.sigmoid(t)).astype(o_ref.dtype)


def _glu(y3, z, d_all, w_all, l):
    nt = z.shape[0]
    tm = 512
    return pl.pallas_call(
        _glu_kernel,
        grid=(nt // tm,),
        in_specs=[
            pl.BlockSpec((SSM_T, tm // SSM_T, SSM_W), lambda i: (0, i, 0)),
            pl.BlockSpec((tm, SSM_W), lambda i: (i, 3 * ATT_W // SSM_W)),
            pl.BlockSpec((None, 1, SSM_W), lambda i: (l, 0, 0)),
            pl.BlockSpec((None, SSM_W, SSM_W), lambda i: (l, 0, 0)),
        ],
        out_specs=pl.BlockSpec((tm, SSM_W), lambda i: (i, 0)),
        out_shape=jax.ShapeDtypeStruct((nt, SSM_W), BF16),
        scratch_shapes=[pltpu.VMEM((SSM_W, SSM_W), BF16), pltpu.VMEM((SSM_W // LANES, tm, LANES), F32)],
        compiler_params=_cparams("arbitrary"),
        name="glu",
    )(y3, z, d_all.reshape(DEPTH, 1, SSM_W), w_all)


def _merge_kernel(o_ref, y_ref, ga_ref, gs_ref, wa_ref, ws_ref, m_ref):
    a = jnp.dot(o_ref[...], wa_ref[...].astype(BF16), preferred_element_type=F32)
    s = jnp.dot(y_ref[...], ws_ref[...].astype(BF16), preferred_element_type=F32)
    m_ref[...] = (jax.nn.sigmoid(ga_ref[...]) * a + jax.nn.sigmoid(gs_ref[...]) * s).astype(m_ref.dtype)


def _merge(o_p, o_s, ys2, z, w_ba_all, w_bs_all, l):
    nt = ys2.shape[0]
    tm, tn = 1024, 512
    ga0 = (3 * ATT_W + SSM_W) // tn
    gs0 = ga0 + D_MODEL // tn
    p_tiles = o_p.shape[0] // tm
    s_tiles = o_s.shape[0] // tm
    assert p_tiles * tm == o_p.shape[0] and s_tiles * tm == o_s.shape[0]

    def kern(op_ref, os_ref, *rest):
        i = pl.program_id(0)

        @pl.when(i < p_tiles)
        def _():
            _merge_kernel(op_ref, *rest)

        @pl.when(i >= p_tiles)
        def _():
            _merge_kernel(os_ref, *rest)

    return pl.pallas_call(
        kern,
        grid=(nt // tm, D_MODEL // tn),
        in_specs=[
            pl.BlockSpec((tm, ATT_W), lambda i, j: (jnp.minimum(i, p_tiles - 1), 0)),
            pl.BlockSpec((tm, ATT_W), lambda i, j: (jnp.maximum(i - p_tiles, 0), 0)),
            pl.BlockSpec((tm, SSM_W), lambda i, j: (i, 0)),
            pl.BlockSpec((tm, tn), lambda i, j: (i, ga0 + j)),
            pl.BlockSpec((tm, tn), lambda i, j: (i, gs0 + j)),
            pl.BlockSpec((None, ATT_W, tn), lambda i, j: (l, 0, j)),
            pl.BlockSpec((None, SSM_W, tn), lambda i, j: (l, 0, j)),
        ],
        out_specs=pl.BlockSpec((tm, tn), lambda i, j: (i, j)),
        out_shape=jax.ShapeDtypeStruct((nt, D_MODEL), BF16),
        compiler_params=_cparams("arbitrary", "arbitrary"),
        name="merge",
    )(o_p, o_s, ys2, z, z, w_ba_all, w_bs_all)


def _proj_residual_kernel(a_ref, w_ref, x_ref, g_ref, o_ref):
    r = jnp.dot(a_ref[...], w_ref[...].astype(BF16), preferred_element_type=F32)
    tm, tn = r.shape
    r3 = r.reshape(tm // CHUNK, CHUNK, tn) * g_ref[...]
    o_ref[...] = x_ref[...] + r3.reshape(tm, tn)


def _proj_residual(a, w_all, x, tab, l, which_gate):
    nt, k = a.shape
    tm, tn = 1024, 512
    cpt = tm // CHUNK
    return pl.pallas_call(
        _proj_residual_kernel,
        grid=(nt // tm, D_MODEL // tn),
        in_specs=[
            pl.BlockSpec((tm, k), lambda i, j: (i, 0)),
            pl.BlockSpec((None, k, tn), lambda i, j: (l, 0, j)),
            pl.BlockSpec((tm, tn), lambda i, j: (i, j)),
            _tab_spec(cpt, l, which_gate, width=tn, col=True),
        ],
        out_specs=pl.BlockSpec((tm, tn), lambda i, j: (i, j)),
        out_shape=jax.ShapeDtypeStruct((nt, D_MODEL), F32),
        compiler_params=_cparams("arbitrary", "arbitrary"),
        name="proj_residual",
    )(a, w_all, x, tab)


def _swiglu_step(h, wg_ref, wu_ref, wd_ref):
    hg = jnp.dot(h, wg_ref[...].astype(BF16), preferred_element_type=F32)
    hu = jnp.dot(h, wu_ref[...].astype(BF16), preferred_element_type=F32)
    a = (hg * jax.nn.sigmoid(hg)) * hu
    return jnp.dot(a.astype(BF16), wd_ref[...].astype(BF16), preferred_element_type=F32)


def _ffn_kernel(h_ref, wg_ref, wu_ref, wd_ref, o_ref):
    @pl.when(pl.program_id(1) == 0)
    def _():
        o_ref[...] = jnp.zeros_like(o_ref)

    o_ref[...] += _swiglu_step(h_ref[...], wg_ref, wu_ref, wd_ref)


def _ffn(h, w_gate, w_up, w_down, j):
    nt = h.shape[0]
    d_ff = w_gate.shape[-1]
    tm, tf = FFN_TM, FFN_TF
    return pl.pallas_call(
        _ffn_kernel,
        grid=(nt // tm, d_ff // tf),
        in_specs=[
            pl.BlockSpec((tm, D_MODEL), lambda i, f: (i, 0)),
            pl.BlockSpec((None, D_MODEL, tf), lambda i, f: (j, 0, f)),
            pl.BlockSpec((None, D_MODEL, tf), lambda i, f: (j, 0, f)),
            pl.BlockSpec((None, tf, D_MODEL), lambda i, f: (j, f, 0)),
        ],
        out_specs=pl.BlockSpec((tm, D_MODEL), lambda i, f: (i, 0)),
        out_shape=jax.ShapeDtypeStruct((nt, D_MODEL), F32),
        compiler_params=_cparams("arbitrary", "arbitrary"),
        name="ffn",
    )(h, w_gate, w_up, w_down)


def _moe_plan(route, nt):
    tm = FFN_TM
    n_assign = 2 * nt
    n_tiles = (n_assign + N_EXPERTS * (tm - 1)) // tm
    e_flat = jnp.concatenate([route[:, 0], route[:, 1]]).astype(jnp.int32)
    onehot = (e_flat[:, None] == jnp.arange(N_EXPERTS, dtype=jnp.int32)[None, :]).astype(jnp.int32)
    csum = jnp.cumsum(onehot, axis=0)
    counts = csum[-1]
    rank = jnp.sum((csum - onehot) * onehot, axis=1)
    tiles_per = (counts + tm - 1) // tm
    tile_end = jnp.cumsum(tiles_per)
    offsets = (tile_end - tiles_per) * tm
    dest = jnp.sum(onehot * offsets[None, :], axis=1) + rank
    tidx = jnp.arange(n_tiles, dtype=jnp.int32)
    tile_expert = jnp.minimum(jnp.sum((tidx[:, None] >= tile_end[None, :]).astype(jnp.int32), axis=1), N_EXPERTS - 1)
    tile_valid = (tidx < tile_end[-1]).astype(jnp.int32)
    tok = jnp.concatenate([jnp.arange(nt, dtype=jnp.int32)] * 2)
    src_token = jnp.zeros((n_tiles * tm,), jnp.int32).at[dest].set(tok)
    return dict(dest1=dest[:nt], dest2=dest[nt:], src_token=src_token,
                tile_expert=tile_expert.astype(jnp.int32), tile_valid=tile_valid)


def _row_copy(src_ref, row, dst_ref, r, sem):
    return pltpu.make_async_copy(src_ref.at[pl.ds(row, 1)], dst_ref.at[pl.ds(r, 1)], sem)


def _gather_start(idx_ref, base, src_ref, dst_ref, sem):
    rows = dst_ref.shape[0]

    def group(gi, c):
        for u in range(GATHER_UNROLL):
            r = gi * GATHER_UNROLL + u
            _row_copy(src_ref, idx_ref[base + r], dst_ref, r, sem).start(priority=u % 2)
        return c

    lax.fori_loop(0, rows // GATHER_UNROLL, group, 0)


def _gather_wait(src_ref, dst_ref, sem):
    pltpu.make_async_copy(src_ref.at[pl.ds(0, dst_ref.shape[0])], dst_ref, sem).wait()


def _gather_rows_kernel(idx_ref, src_ref, o_ref, buf_ref, sem):
    i = pl.program_id(0)
    n = pl.num_programs(0)
    slot = i % 2

    @pl.when(i == 0)
    def _():
        _gather_start(idx_ref, 0, src_ref, buf_ref.at[0], sem.at[0])

    @pl.when(i + 1 < n)
    def _():
        _gather_start(idx_ref, (i + 1) * GATHER_ROWS, src_ref, buf_ref.at[1 - slot], sem.at[1 - slot])

    _gather_wait(src_ref, buf_ref.at[slot], sem.at[slot])
    for c in range(buf_ref.shape[2]):
        o_ref[:, c * LANES:(c + 1) * LANES] = buf_ref[slot, :, c, :].astype(o_ref.dtype)


def _gather_rows(src, idx, out_dtype):
    n_out = idx.shape[0]
    n_col = src.shape[1]
    width = n_col * LANES
    return pl.pallas_call(
        _gather_rows_kernel,
        grid_spec=pltpu.PrefetchScalarGridSpec(
            num_scalar_prefetch=1,
            grid=(n_out // GATHER_ROWS,),
            in_specs=[pl.BlockSpec(memory_space=pl.ANY)],
            out_specs=pl.BlockSpec((GATHER_ROWS, width), lambda i, *_: (i, 0)),
            scratch_shapes=[pltpu.VMEM((2, GATHER_ROWS, n_col, LANES), src.dtype), pltpu.SemaphoreType.DMA((2,))],
        ),
        out_shape=jax.ShapeDtypeStruct((n_out, width), out_dtype),
        compiler_params=_cparams("arbitrary"),
        name="moe_gather",
    )(idx, src)


def _moe_ffn_kernel(te_ref, tv_ref, h_ref, wg_ref, wu_ref, wd_ref, o_ref):
    t = pl.program_id(0)

    @pl.when(pl.program_id(1) == 0)
    def _():
        o_ref[...] = jnp.zeros_like(o_ref)

    @pl.when(tv_ref[t] == 1)
    def _():
        o_ref[...] += _swiglu_step(h_ref[...], wg_ref, wu_ref, wd_ref)


def _moe_ffn(h_sorted, tile_expert, tile_valid, w_gate, w_up, w_down, j):
    d_ff = w_gate.shape[-1]
    tm, tf = FFN_TM, FFN_TF
    n_tiles = h_sorted.shape[0] // tm
    nf = d_ff // tf

    def f_idx(t, f, tv):
        return jnp.where(tv[t] == 1, f, nf - 1)

    return pl.pallas_call(
        _moe_ffn_kernel,
        grid_spec=pltpu.PrefetchScalarGridSpec(
            num_scalar_prefetch=2,
            grid=(n_tiles, nf),
            in_specs=[
                pl.BlockSpec((tm, D_MODEL), lambda t, f, te, tv: (t, 0)),
                pl.BlockSpec((None, None, D_MODEL, tf), lambda t, f, te, tv: (j, te[t], 0, f_idx(t, f, tv))),
                pl.BlockSpec((None, None, D_MODEL, tf), lambda t, f, te, tv: (j, te[t], 0, f_idx(t, f, tv))),
                pl.BlockSpec((None, None, tf, D_MODEL), lambda t, f, te, tv: (j, te[t], f_idx(t, f, tv), 0)),
            ],
            out_specs=pl.BlockSpec((tm, D_MODEL), lambda t, f, te, tv: (t, 0)),
        ),
        out_shape=jax.ShapeDtypeStruct((n_tiles * tm, D_MODEL), F32),
        compiler_params=_cparams("arbitrary", "arbitrary"),
        name="moe_ffn",
    )(tile_expert, tile_valid, h_sorted, w_gate, w_up, w_down)


def _moe_combine_kernel(d1_ref, d2_ref, y_ref, x_ref, route_ref, g_ref, gn_ref, o_ref, a_ref, b_ref, sem_a, sem_b,
                        *, final_norm):
    i = pl.program_id(0)
    n = pl.num_programs(0)
    slot = i % 2

    def start(step, s):
        _gather_start(d1_ref, step * GATHER_ROWS, y_ref, a_ref.at[s], sem_a.at[s])
        _gather_start(d2_ref, step * GATHER_ROWS, y_ref, b_ref.at[s], sem_b.at[s])

    @pl.when(i == 0)
    def _():
        start(0, 0)

    @pl.when(i + 1 < n)
    def _():
        start(i + 1, 1 - slot)

    _gather_wait(y_ref, a_ref.at[slot], sem_a.at[slot])
    _gather_wait(y_ref, b_ref.at[slot], sem_b.at[slot])
    route = route_ref[...]
    r = route[:, 2:3] * a_ref[slot] + route[:, 3:4] * b_ref[slot]
    tm, n_col = r.shape
    xn = x_ref[...] + (r.reshape(tm // CHUNK, CHUNK, n_col) * g_ref[...]).reshape(tm, n_col)
    if final_norm:
        xn = xn * lax.rsqrt(jnp.mean(xn * xn, axis=-1, keepdims=True) + EPS) * gn_ref[...]
    o_ref[...] = xn


def _moe_combine(y_sorted, dest1, dest2, x, route, tab, l, which_gate, g_final=None):
    final_norm = g_final is not None
    gn = (g_final if final_norm else jnp.ones((D_MODEL,), F32)).reshape(1, D_MODEL)
    nt = x.shape[0]
    tm = GATHER_ROWS
    blk = pl.BlockSpec((tm, D_MODEL), lambda i, *_: (i, 0))
    buf = pltpu.VMEM((2, tm, D_MODEL), F32)
    return pl.pallas_call(
        functools.partial(_moe_combine_kernel, final_norm=final_norm),
        grid_spec=pltpu.PrefetchScalarGridSpec(
            num_scalar_prefetch=2,
            grid=(nt // tm,),
            in_specs=[pl.BlockSpec(memory_space=pl.ANY), blk, pl.BlockSpec((tm, LANES), lambda i, *_: (i, 0)),
                      _tab_spec(tm // CHUNK, l, which_gate), pl.BlockSpec((1, D_MODEL), lambda i, *_: (0, 0))],
            out_specs=blk,
            scratch_shapes=[buf, buf, pltpu.SemaphoreType.DMA((2,)), pltpu.SemaphoreType.DMA((2,))],
        ),
        out_shape=jax.ShapeDtypeStruct((nt, D_MODEL), F32),
        compiler_params=_cparams("arbitrary"),
        name="moe_combine",
    )(dest1, dest2, y_sorted, x, route, tab, gn)


def _residual_kernel(x_ref, r_ref, g_ref, o_ref):
    r = r_ref[...]
    tm, n = r.shape
    o_ref[...] = x_ref[...] + (r.reshape(tm // CHUNK, CHUNK, n) * g_ref[...]).reshape(tm, n)


def _residual(x, r, tab, l, which_gate):
    nt = x.shape[0]
    tm = 512
    blk = pl.BlockSpec((tm, D_MODEL), lambda i: (i, 0))
    return pl.pallas_call(
        _residual_kernel, grid=(nt // tm,),
        in_specs=[blk, blk, _tab_spec(tm // CHUNK, l, which_gate)],
        out_specs=blk, out_shape=jax.ShapeDtypeStruct((nt, D_MODEL), F32),
        compiler_params=_cparams("arbitrary"), name="residual",
    )(x, r, tab)


def _residual_norm_kernel(x_ref, r_ref, gate_ref, g_ref, sh_ref, sc_ref, xo_ref, h_ref):
    r = r_ref[...]
    tm, n = r.shape
    xn = x_ref[...] + (r.reshape(tm // CHUNK, CHUNK, n) * gate_ref[...]).reshape(tm, n)
    xo_ref[...] = xn
    h_ref[...] = _modulate(xn, g_ref, sh_ref, sc_ref).astype(BF16)


def _residual_norm(x, r, tab, l, which_gate, g_all, l_next, which_shift, which_scale):
    nt = x.shape[0]
    tm = 512
    cpt = tm // CHUNK
    blk = pl.BlockSpec((tm, D_MODEL), lambda i: (i, 0))
    return pl.pallas_call(
        _residual_norm_kernel, grid=(nt // tm,),
        in_specs=[blk, blk, _tab_spec(cpt, l, which_gate),
                  pl.BlockSpec((None, 1, D_MODEL), lambda i: (l_next, 0, 0)),
                  _tab_spec(cpt, l_next, which_shift), _tab_spec(cpt, l_next, which_scale)],
        out_specs=[blk, blk],
        out_shape=[jax.ShapeDtypeStruct((nt, D_MODEL), F32), jax.ShapeDtypeStruct((nt, D_MODEL), BF16)],
        compiler_params=_cparams("arbitrary"), name="residual_norm",
    )(x, r, tab, g_all.reshape(DEPTH, 1, D_MODEL), tab, tab)


def _kv_rows_kernel(z_ref, o_ref):
    for h in range(N_HEADS):
        o_ref[:, h, :] = z_ref[:, h * HEAD_DIM:(h + 1) * HEAD_DIM]


def _kv_rows(z, n_pb, seq, n_sb):
    rows = PAST_WINDOW
    per_seq = seq // rows
    s_blocks = n_sb * CHUNK // rows
    assert per_seq * rows == seq and s_blocks * rows == n_sb * CHUNK

    def row_block(r):
        return jnp.where(r < n_pb, (r + 1) * per_seq - 1, n_pb * per_seq + (r - n_pb))

    n_blocks = n_pb + s_blocks
    return pl.pallas_call(
        _kv_rows_kernel, grid=(2, n_blocks),
        in_specs=[pl.BlockSpec((rows, ATT_W), lambda w, r: (row_block(r), 1 + w))],
        out_specs=pl.BlockSpec((None, rows, N_HEADS, HEAD_DIM), lambda w, r: (w, r, 0, 0)),
        out_shape=jax.ShapeDtypeStruct((2, n_blocks * rows, N_HEADS, HEAD_DIM), F32),
        compiler_params=_cparams("arbitrary", "arbitrary"), name="kv_rows",
    )(z)


def _final_norm_kernel(x_ref, g_ref, o_ref):
    x = x_ref[...]
    o_ref[...] = x * lax.rsqrt(jnp.mean(x * x, axis=-1, keepdims=True) + EPS) * g_ref[...]


def _final_norm(x, g):
    nt = x.shape[0]
    tm = 512
    blk = pl.BlockSpec((tm, D_MODEL), lambda i: (i, 0))
    return pl.pallas_call(
        _final_norm_kernel, grid=(nt // tm,),
        in_specs=[blk, pl.BlockSpec((1, D_MODEL), lambda i: (0, 0))],
        out_specs=blk, out_shape=jax.ShapeDtypeStruct((nt, D_MODEL), F32),
        compiler_params=_cparams("arbitrary"), name="final_norm",
    )(x, g.reshape(1, D_MODEL))


def kernel(x_prompt, x_sample, cache_k, cache_v, state_ssm_re, state_ssm_im, c_prompt, c_sample, w_ada, b_ada, g_norm_mix, g_norm_ffn, g_norm_final, w_in, rel_bias, ssm_a_re, ssm_a_im, ssm_log_step, ssm_b_re, ssm_b_im, ssm_c_re, ssm_c_im, ssm_d, w_ssm_glu, w_branch_attn, w_branch_ssm, w_out, w_ffn_gate, w_ffn_up, w_ffn_down, w_router, b_router, w_exp_gate, w_exp_up, w_exp_down):
    n_pb, seq = x_prompt.shape[0], x_prompt.shape[1]
    n_sb, dec = x_sample.shape[0], x_sample.shape[1]
    assert dec == CHUNK and cache_k.shape[2] == PAST_WINDOW and seq % ATT_ROWS == 0
    np_tok = n_pb * seq
    nt = np_tok + n_sb * dec
    n_batch = n_pb + n_sb

    x = jnp.concatenate([x_prompt.reshape(np_tok, D_MODEL), x_sample.reshape(n_sb * dec, D_MODEL)], axis=0)

    c_all = jnp.concatenate([c_prompt, c_sample], axis=0)
    c_pad = jnp.pad(c_all, ((0, -n_batch % 8), (0, 0)))
    ada = _ada(c_pad, w_ada, b_ada)
    ada_p = jnp.broadcast_to(ada[:, :n_pb, None, :], (DEPTH, n_pb, seq // CHUNK, 6 * D_MODEL))
    tab = jnp.concatenate([ada_p.reshape(DEPTH, np_tok // CHUNK, 6 * D_MODEL), ada[:, n_pb:n_batch]], axis=1)
    tab = tab.reshape(DEPTH, nt // CHUNK, 6, 1, D_MODEL)

    sub_p, sub_s = seq // SSM_T, dec // SSM_T
    seq_id = jnp.concatenate([jnp.repeat(jnp.arange(n_pb), sub_p), n_pb + jnp.repeat(jnp.arange(n_sb), sub_s)])
    seq_id = seq_id.astype(jnp.int32)
    first = jnp.concatenate([jnp.arange(n_pb * sub_p) % sub_p == 0, jnp.arange(n_sb * sub_s) % sub_s == 0])
    first = first.astype(jnp.int32)
    last_p = (jnp.arange(n_pb) + 1) * sub_p - 1
    last_s = n_pb * sub_p + (jnp.arange(n_sb) + 1) * sub_s - 1
    zeros_h0 = jnp.zeros((n_pb, SSM_GROUPS, SSM_STATE), F32)

    outs = {k: [] for k in ('kp', 'vp', 'srp', 'sip', 'ks', 'vs', 'srs', 'sis')}
    keep = min(PAST_WINDOW, seq)
    h = _norm_mod(x, g_norm_mix, tab, 0, SHIFT1, SCALE1)
    for l in range(DEPTH):
        last = l == DEPTH - 1
        z, u3 = _in_proj(h, w_in, l)

        b64 = _attention_bias(rel_bias[l])
        o_p = _attn_prompt(z, b64, n_pb, seq)
        o_s = _attn_sample(z, cache_k, cache_v, l, b64, np_tok, n_sb)

        mats = _ssm_matrices(ssm_a_re[l], ssm_a_im[l], ssm_log_step[l], ssm_b_re[l], ssm_b_im[l],
                             ssm_c_re[l], ssm_c_im[l])
        h0_re = jnp.concatenate([zeros_h0, state_ssm_re[l].astype(F32)], axis=0)
        h0_im = jnp.concatenate([zeros_h0, state_ssm_im[l].astype(F32)], axis=0)
        y3, xe_re, xe_im = _ssm(u3, mats, first, seq_id, h0_re, h0_im)

        ys2 = _glu(y3, z, ssm_d, w_ssm_glu, l)
        merged = _merge(o_p, o_s, ys2, z, w_branch_attn, w_branch_ssm, l)
        x = _proj_residual(merged, w_out, x, tab, l, GATE1)

        j = l // 2
        if l % 2 == 0:
            h2 = _norm_mod(x, g_norm_ffn, tab, l, SHIFT2, SCALE2)
            r = _ffn(h2, w_ffn_gate, w_ffn_up, w_ffn_down, j)
            if last:
                yn = _final_norm(_residual(x, r, tab, l, GATE2), g_norm_final)
            else:
                x, h = _residual_norm(x, r, tab, l, GATE2, g_norm_mix, l + 1, SHIFT1, SCALE1)
        else:
            h2, route = _norm_mod(x, g_norm_ffn, tab, l, SHIFT2, SCALE2, router=(w_router[j], b_router[j]))
            plan = _moe_plan(route, nt)
            h_sorted = _gather_rows(h2, plan['src_token'], BF16)
            y_sorted = _moe_ffn(h_sorted, plan['tile_expert'], plan['tile_valid'], w_exp_gate, w_exp_up, w_exp_down, j)
            if last:
                yn = _moe_combine(y_sorted, plan['dest1'], plan['dest2'], x, route, tab, l, GATE2, g_final=g_norm_final)
            else:
                x = _moe_combine(y_sorted, plan['dest1'], plan['dest2'], x, route, tab, l, GATE2)
                h = _norm_mod(x, g_norm_mix, tab, l + 1, SHIFT1, SCALE1)

        kv = _kv_rows(z, n_pb, seq, n_sb)
        outs['kp'].append(kv[0, :n_pb * keep].reshape(n_pb, keep, N_HEADS, HEAD_DIM))
        outs['vp'].append(kv[1, :n_pb * keep].reshape(n_pb, keep, N_HEADS, HEAD_DIM))
        outs['ks'].append(kv[0, n_pb * keep:].reshape(n_sb, dec, N_HEADS, HEAD_DIM))
        outs['vs'].append(kv[1, n_pb * keep:].reshape(n_sb, dec, N_HEADS, HEAD_DIM))
        outs['srp'].append(_states_at(xe_re, last_p))
        outs['sip'].append(_states_at(xe_im, last_p))
        outs['srs'].append(_states_at(xe_re, last_s))
        outs['sis'].append(_states_at(xe_im, last_s))

    y_prompt = yn[:np_tok].reshape(n_pb, seq, D_MODEL)
    y_sample = yn[np_tok:].reshape(n_sb, dec, D_MODEL)
    st = lambda name: jnp.stack(outs[name])
    return (y_prompt, y_sample, st('kp'), st('vp'), st('srp'), st('sip'),
            st('ks'), st('vs'), st('srs'), st('sis'))
```

```python
import functools
import math

import jax
import jax.numpy as jnp
from jax import lax
from jax.experimental import pallas as pl
from jax.experimental.pallas import tpu as pltpu

F32 = jnp.float32
BF16 = jnp.bfloat16

D_MODEL = 2048
DEPTH = 2
CHUNK = 64
PAST_CHUNKS = 8
PAST_WINDOW = PAST_CHUNKS * CHUNK
BAND = PAST_WINDOW + CHUNK
N_HEADS = 8
HEAD_DIM = 128
ATT_W = N_HEADS * HEAD_DIM
ATT_SCALE = HEAD_DIM ** -0.5
REL_CLIP = 256
SSM_W = 1024
SSM_GROUP = 16
SSM_GROUPS = SSM_W // SSM_GROUP
SSM_STATE = 64
N_EXPERTS = 8
IN_COLS = 3 * ATT_W + SSM_W + 2 * D_MODEL
EPS = 1e-6
NEG_INF = -1e30

LANES = 128
VMEM_LIMIT_BYTES = 56 * 1024 * 1024

SSM_T = 16
SSM_OCT = LANES // SSM_GROUP
N_OCT = SSM_GROUPS // SSM_OCT
OCT_K = SSM_T * LANES
OCT_STATE = SSM_OCT * SSM_STATE
GROUP_SHIFT = SSM_GROUP.bit_length() - 1
STATE_SHIFT = SSM_STATE.bit_length() - 1
assert (1 << GROUP_SHIFT, 1 << STATE_SHIFT) == (SSM_GROUP, SSM_STATE)
ATT_GROUP = 4
ATT_ROWS = ATT_GROUP * CHUNK
ATT_KEYS = 3 * ATT_ROWS

IN_TM, IN_TN = 1024, 512
FFN_TM = 768
FFN_TF = 512
GATHER_ROWS = 256
GATHER_UNROLL = 8

SHIFT1, SCALE1, GATE1, SHIFT2, SCALE2, GATE2 = range(6)


def _cparams(*sem):
    return pltpu.CompilerParams(dimension_semantics=sem, vmem_limit_bytes=VMEM_LIMIT_BYTES)


def _tab_spec(cpt, l, which, width=D_MODEL, col=None):
    if col is None:
        return pl.BlockSpec((None, cpt, None, 1, width), lambda i, *_: (l, i, which, 0, 0))
    return pl.BlockSpec((None, cpt, None, 1, width), lambda i, j, *_: (l, i, which, 0, j))


def _ada_kernel(c_ref, w_ref, b_ref, o_ref):
    c = c_ref[...]
    a = (c * jax.nn.sigmoid(c)).astype(BF16)
    o_ref[...] = jnp.dot(a, w_ref[...].astype(BF16), preferred_element_type=F32) + b_ref[...]


def _ada(c_pad, w_ada, b_ada):
    rows = c_pad.shape[0]
    n = w_ada.shape[-1]
    tn = 1024
    return pl.pallas_call(
        _ada_kernel,
        grid=(DEPTH, n // tn),
        in_specs=[
            pl.BlockSpec((rows, D_MODEL), lambda l, j: (0, 0)),
            pl.BlockSpec((None, D_MODEL, tn), lambda l, j: (l, 0, j)),
            pl.BlockSpec((None, 1, tn), lambda l, j: (l, 0, j)),
        ],
        out_specs=pl.BlockSpec((None, rows, tn), lambda l, j: (l, 0, j)),
        out_shape=jax.ShapeDtypeStruct((DEPTH, rows, n), F32),
        compiler_params=_cparams("arbitrary", "arbitrary"),
        name="ada",
    )(c_pad, w_ada, b_ada.reshape(DEPTH, 1, n))


def _modulated(x_ref, g_ref, sh_ref, sc_ref):
    return _modulate(x_ref[...], g_ref, sh_ref, sc_ref)


def _modulate(x, g_ref, sh_ref, sc_ref):
    tm = x.shape[0]
    y = x * lax.rsqrt(jnp.mean(x * x, axis=-1, keepdims=True) + EPS) * g_ref[...]
    y3 = y.reshape(tm // CHUNK, CHUNK, D_MODEL)
    h = y3 * (1.0 + sc_ref[...]) + sh_ref[...]
    return h.reshape(tm, D_MODEL)


def _norm_mod_kernel(x_ref, g_ref, sh_ref, sc_ref, h_ref):
    h_ref[...] = _modulated(x_ref, g_ref, sh_ref, sc_ref).astype(BF16)


def _split_bf16(a):
    hi = a.astype(BF16)
    lo = (a - hi.astype(F32)).astype(BF16)
    return hi, lo


def _norm_mod_router_kernel(x_ref, g_ref, sh_ref, sc_ref, wr_ref, br_ref, h_ref, route_ref):
    h = _modulated(x_ref, g_ref, sh_ref, sc_ref)
    h_ref[...] = h
    h_hi, h_lo = _split_bf16(h)
    w_hi, w_lo = _split_bf16(wr_ref[...])
    dot = functools.partial(jnp.dot, preferred_element_type=F32)
    logits = (dot(h_hi, w_hi) + (dot(h_hi, w_lo) + dot(h_lo, w_hi)) + dot(h_lo, w_lo)) + br_ref[...]
    lane = lax.broadcasted_iota(jnp.int32, logits.shape, 1).astype(F32)
    lg = jnp.where(lane < N_EXPERTS, logits, -jnp.inf)
    m1 = jnp.max(lg, axis=-1, keepdims=True)
    i1 = jnp.min(jnp.where(lg == m1, lane, float(LANES)), axis=-1, keepdims=True)
    lg2 = jnp.where(lane == i1, -jnp.inf, lg)
    m2 = jnp.max(lg2, axis=-1, keepdims=True)
    i2 = jnp.min(jnp.where(lg2 == m2, lane, float(LANES)), axis=-1, keepdims=True)
    e2 = jnp.exp(m2 - m1)
    den = 1.0 + e2
    route = jnp.where(lane == 0.0, i1, jnp.where(lane == 1.0, i2, jnp.where(lane == 2.0, 1.0 / den, e2 / den)))
    route_ref[...] = jnp.where(lane < 4.0, route, 0.0)


def _norm_mod(x, g_all, tab, l, which_shift, which_scale, router=None):
    nt = x.shape[0]
    tm = 512
    cpt = tm // CHUNK
    in_specs = [
        pl.BlockSpec((tm, D_MODEL), lambda i: (i, 0)),
        pl.BlockSpec((None, 1, D_MODEL), lambda i: (l, 0, 0)),
        _tab_spec(cpt, l, which_shift),
        _tab_spec(cpt, l, which_scale),
    ]
    h_spec = pl.BlockSpec((tm, D_MODEL), lambda i: (i, 0))
    args = [x, g_all.reshape(DEPTH, 1, D_MODEL), tab, tab]
    if router is None:
        return pl.pallas_call(
            _norm_mod_kernel, grid=(nt // tm,), in_specs=in_specs, out_specs=h_spec,
            out_shape=jax.ShapeDtypeStruct((nt, D_MODEL), BF16),
            compiler_params=_cparams("arbitrary"), name="norm_mod",
        )(*args)
    w_r, b_r = router
    w_pad = jnp.zeros((D_MODEL, LANES), F32).at[:, :N_EXPERTS].set(w_r)
    b_pad = jnp.zeros((1, LANES), F32).at[0, :N_EXPERTS].set(b_r)
    in_specs += [
        pl.BlockSpec((D_MODEL, LANES), lambda i: (0, 0)),
        pl.BlockSpec((1, LANES), lambda i: (0, 0)),
    ]
    return pl.pallas_call(
        _norm_mod_router_kernel, grid=(nt // tm,), in_specs=in_specs,
        out_specs=[h_spec, pl.BlockSpec((tm, LANES), lambda i: (i, 0))],
        out_shape=[jax.ShapeDtypeStruct((nt, D_MODEL), F32), jax.ShapeDtypeStruct((nt, LANES), F32)],
        compiler_params=_cparams("arbitrary"), name="norm_mod_router",
    )(*args, w_pad, b_pad)


def _in_proj_kernel(x_ref, w_ref, z_ref, u3_ref, scr_ref):
    j = pl.program_id(1)
    r = jnp.dot(x_ref[...], w_ref[...].astype(BF16), preferred_element_type=F32)
    z_ref[...] = r
    tm, tn = r.shape
    u_first = 3 * ATT_W // tn
    for part in range(SSM_W // tn):
        @pl.when(j == u_first + part)
        def _():
            for c in range(tn // LANES):
                scr_ref[c] = r[:, c * LANES:(c + 1) * LANES]
            for s in range(SSM_T):
                for c in range(tn // LANES):
                    lo = part * tn + c * LANES
                    rows = scr_ref[c, pl.ds(s, tm // SSM_T, stride=SSM_T), :]
                    u3_ref[s, :, lo:lo + LANES] = rows.astype(BF16)


def _in_proj(h, w_all, l):
    nt, k = h.shape
    n = w_all.shape[-1]
    tm, tn = IN_TM, IN_TN
    return pl.pallas_call(
        _in_proj_kernel,
        grid=(nt // tm, n // tn),
        in_specs=[
            pl.BlockSpec((tm, k), lambda i, j: (i, 0)),
            pl.BlockSpec((None, k, tn), lambda i, j: (l, 0, j)),
        ],
        out_specs=[
            pl.BlockSpec((tm, tn), lambda i, j: (i, j)),
            pl.BlockSpec((SSM_T, tm // SSM_T, SSM_W), lambda i, j: (0, i, 0)),
        ],
        out_shape=[
            jax.ShapeDtypeStruct((nt, n), F32),
            jax.ShapeDtypeStruct((SSM_T, nt // SSM_T, SSM_W), BF16),
        ],
        scratch_shapes=[pltpu.VMEM((tn // LANES, tm, LANES), F32)],
        compiler_params=_cparams("arbitrary", "arbitrary"),
        name="in_proj",
    )(h, w_all)


def _softmax_rows(s):
    m = jnp.max(s, axis=-1, keepdims=True)
    e = jnp.exp(s - m)
    return e / jnp.sum(e, axis=-1, keepdims=True)


def _attn_prompt_kernel(q_ref, k0_ref, k1_ref, k2_ref, v0_ref, v1_ref, v2_ref, b64_ref, o_ref, bias_ref):
    g = pl.program_id(1)

    @pl.when((pl.program_id(0) == 0) & (g == 0))
    def _():
        bias_ref[...] = jnp.full(bias_ref.shape, NEG_INF, F32)
        for c in range(ATT_GROUP):
            bias_ref[:, c * CHUNK:(c + 1) * CHUNK, c * CHUNK:c * CHUNK + BAND] = b64_ref[...]

    col = lax.broadcasted_iota(jnp.int32, (ATT_ROWS, ATT_KEYS), 1)
    visible = (g * ATT_ROWS - 2 * ATT_ROWS + col) >= 0
    for h in range(N_HEADS):
        sl = slice(h * HEAD_DIM, (h + 1) * HEAD_DIM)
        qh = q_ref[:, sl].astype(BF16)
        kh = jnp.concatenate([k0_ref[:, sl], k1_ref[:, sl], k2_ref[:, sl]], axis=0).astype(BF16)
        vh = jnp.concatenate([v0_ref[:, sl], v1_ref[:, sl], v2_ref[:, sl]], axis=0).astype(BF16)
        s = lax.dot_general(qh, kh, (((1,), (1,)), ((), ())), preferred_element_type=F32)
        s = s * ATT_SCALE + bias_ref[h]
        s = jnp.where(visible, s, NEG_INF)
        p = _softmax_rows(s).astype(BF16)
        o_ref[:, sl] = jnp.dot(p, vh, preferred_element_type=F32).astype(o_ref.dtype)


def _attn_prompt(z, b64, n_batch, seq):
    groups = seq // ATT_ROWS
    blk = (ATT_ROWS, ATT_W)

    def kv_spec(col, back):
        return pl.BlockSpec(blk, lambda b, g: (b * groups + jnp.maximum(g - back, 0), col))

    return pl.pallas_call(
        _attn_prompt_kernel,
        grid=(n_batch, groups),
        in_specs=[
            pl.BlockSpec(blk, lambda b, g: (b * groups + g, 0)),
            kv_spec(1, 2), kv_spec(1, 1), kv_spec(1, 0),
            kv_spec(2, 2), kv_spec(2, 1), kv_spec(2, 0),
            pl.BlockSpec((N_HEADS, CHUNK, BAND), lambda b, g: (0, 0, 0)),
        ],
        out_specs=pl.BlockSpec(blk, lambda b, g: (b * groups + g, 0)),
        out_shape=jax.ShapeDtypeStruct((n_batch * seq, ATT_W), BF16),
        scratch_shapes=[pltpu.VMEM((N_HEADS, ATT_ROWS, ATT_KEYS), F32)],
        compiler_params=_cparams("arbitrary", "arbitrary"),
        name="attn_prompt",
    )(z, z, z, z, z, z, z, b64)


def _attn_sample_kernel(q_ref, kn_ref, vn_ref, kc_ref, vc_ref, bias_ref, o_ref):
    for h in range(N_HEADS):
        sl = slice(h * HEAD_DIM, (h + 1) * HEAD_DIM)
        qh = q_ref[:, sl].astype(BF16)
        kh = jnp.concatenate([kc_ref[:, h, :], kn_ref[:, sl]], axis=0).astype(BF16)
        vh = jnp.concatenate([vc_ref[:, h, :], vn_ref[:, sl]], axis=0).astype(BF16)
        s = lax.dot_general(qh, kh, (((1,), (1,)), ((), ())), preferred_element_type=F32)
        s = s * ATT_SCALE + bias_ref[h]
        p = _softmax_rows(s).astype(BF16)
        o_ref[:, sl] = jnp.dot(p, vh, preferred_element_type=F32).astype(o_ref.dtype)


def _attn_sample(z, cache_k, cache_v, l, b64, row0, n_batch):
    base = row0 // CHUNK
    blk = (CHUNK, ATT_W)
    cache_spec = pl.BlockSpec((None, None, PAST_WINDOW, N_HEADS, HEAD_DIM), lambda b: (l, b, 0, 0, 0))
    return pl.pallas_call(
        _attn_sample_kernel,
        grid=(n_batch,),
        in_specs=[
            pl.BlockSpec(blk, lambda b: (base + b, 0)),
            pl.BlockSpec(blk, lambda b: (base + b, 1)),
            pl.BlockSpec(blk, lambda b: (base + b, 2)),
            cache_spec, cache_spec,
            pl.BlockSpec((N_HEADS, CHUNK, BAND), lambda b: (0, 0, 0)),
        ],
        out_specs=pl.BlockSpec(blk, lambda b: (b, 0)),
        out_shape=jax.ShapeDtypeStruct((n_batch * CHUNK, ATT_W), BF16),
        compiler_params=_cparams("arbitrary"),
        name="attn_sample",
    )(z, z, z, cache_k, cache_v, b64)


def _attention_bias(rel_bias):
    rev = rel_bias[::-1].astype(F32)
    flat = PAST_WINDOW - REL_CLIP + CHUNK - 1
    vec = jnp.concatenate([jnp.broadcast_to(rev[:1], (flat, N_HEADS)), rev[:BAND + CHUNK - 1 - flat]], axis=0)
    rows = [vec[CHUNK - 1 - qi:CHUNK - 1 - qi + BAND] for qi in range(CHUNK)]
    return jnp.transpose(jnp.stack(rows), (2, 0, 1))


def _octet_lhs(u3_ref):
    return jnp.concatenate([u3_ref[s] for s in range(SSM_T)], axis=1)


def _group_mask(rows, cols, row_shift, col_shift):
    r = lax.shift_right_logical(lax.broadcasted_iota(jnp.int32, (rows, cols), 0), row_shift)
    c = lax.shift_right_logical(lax.broadcasted_iota(jnp.int32, (rows, cols), 1), col_shift)
    return r == c


def _blockdiag(x, rep_ref, mask):
    e = jnp.dot(x, rep_ref[...], preferred_element_type=F32)
    return jnp.where(mask, e, 0.0).astype(BF16)


def _ssm_state_kernel(u3_ref, mre_ref, mim_ref, rep_ref, sre_ref, sim_ref, w_ref):
    mask = _group_mask(LANES, OCT_STATE, GROUP_SHIFT, STATE_SHIFT)
    for s in range(SSM_T):
        rows = slice(s * LANES, (s + 1) * LANES)
        w_ref[rows, :OCT_STATE] = _blockdiag(mre_ref[s], rep_ref, mask)
        w_ref[rows, OCT_STATE:] = _blockdiag(mim_ref[s], rep_ref, mask)
    s = jnp.dot(_octet_lhs(u3_ref), w_ref[...], preferred_element_type=F32)
    sre_ref[...] = s[:, :OCT_STATE]
    sim_ref[...] = s[:, OCT_STATE:]


def _ssm_state(u3, m_re, m_im, rep_state):
    n_sub = u3.shape[1]
    out = jax.ShapeDtypeStruct((N_OCT, n_sub, OCT_STATE), F32)
    m_spec = pl.BlockSpec((None, SSM_T, LANES, SSM_STATE), lambda j: (j, 0, 0, 0))
    return pl.pallas_call(
        _ssm_state_kernel,
        grid=(N_OCT,),
        in_specs=[
            pl.BlockSpec((SSM_T, n_sub, LANES), lambda j: (0, 0, j)),
            m_spec, m_spec,
            pl.BlockSpec((SSM_STATE, OCT_STATE), lambda j: (0, 0)),
        ],
        out_specs=[pl.BlockSpec((None, n_sub, OCT_STATE), lambda j: (j, 0, 0))] * 2,
        out_shape=[out, out],
        scratch_shapes=[pltpu.VMEM((OCT_K, 2 * OCT_STATE), BF16)],
        compiler_params=_cparams("arbitrary"),
        name="ssm_state",
    )(u3, m_re, m_im, rep_state)


def _ssm_carry_kernel(first_ref, seq_ref, sre_ref, sim_ref, ar_ref, ai_ref, h0r_ref, h0i_ref,
                      xsr_ref, xsi_ref, xer_ref, xei_ref, st_ref):
    blk = pl.program_id(0)
    steps = sre_ref.shape[1]

    @pl.when(blk == 0)
    def _():
        st_ref[...] = jnp.zeros_like(st_ref)

    ar = ar_ref[...]
    ai = ai_ref[...]

    def body(kk, carry):
        k = blk * steps + kk
        is_first = first_ref[k] == 1
        sq = seq_ref[k]
        xr = jnp.where(is_first, h0r_ref[sq], st_ref[0])
        xi = jnp.where(is_first, h0i_ref[sq], st_ref[1])
        xsr_ref[:, kk, :] = xr
        xsi_ref[:, kk, :] = xi
        nr = ar * xr - ai * xi + sre_ref[:, kk, :]
        ni = ar * xi + ai * xr + sim_ref[:, kk, :]
        xer_ref[:, kk, :] = nr
        xei_ref[:, kk, :] = ni
        st_ref[0] = nr
        st_ref[1] = ni
        return carry

    lax.fori_loop(0, steps, body, 0)


def _ssm_carry(first, seq_id, sre, sim, at_re, at_im, h0_re, h0_im):
    n_sub = sre.shape[1]
    steps = 64
    tile = (N_OCT, OCT_STATE)
    blk = pl.BlockSpec((N_OCT, steps, OCT_STATE), lambda i, *_: (0, i, 0))
    const = pl.BlockSpec(tile, lambda i, *_: (0, 0))
    tab = pl.BlockSpec((h0_re.shape[0],) + tile, lambda i, *_: (0, 0, 0))
    out = jax.ShapeDtypeStruct((N_OCT, n_sub, OCT_STATE), F32)
    return pl.pallas_call(
        _ssm_carry_kernel,
        grid_spec=pltpu.PrefetchScalarGridSpec(
            num_scalar_prefetch=2,
            grid=(n_sub // steps,),
            in_specs=[blk, blk, const, const, tab, tab],
            out_specs=[blk, blk, blk, blk],
            scratch_shapes=[pltpu.VMEM((2,) + tile, F32)],
        ),
        out_shape=[out, out, out, out],
        compiler_params=_cparams("arbitrary"),
        name="ssm_carry",
    )(first, seq_id, sre, sim, at_re.reshape(tile), at_im.reshape(tile),
      h0_re.reshape((-1,) + tile), h0_im.reshape((-1,) + tile))


def _ssm_out_kernel(u3_ref, taps_ref, xr_ref, xi_ref, ccr_ref, cci_ref, rep_ref, y3_ref, kt_ref, cr_ref, ci_ref):
    @pl.when(pl.program_id(0) == 0)
    def _():
        kt_ref[...] = jnp.zeros_like(kt_ref)

    tap_mask = _group_mask(LANES, LANES, GROUP_SHIFT, GROUP_SHIFT)
    for tau in range(SSM_T):
        block = _blockdiag(taps_ref[tau], rep_ref, tap_mask)
        for s in range(SSM_T - tau):
            t = s + tau
            kt_ref[s * LANES:(s + 1) * LANES, t * LANES:(t + 1) * LANES] = block
    c_mask = _group_mask(OCT_STATE, LANES, STATE_SHIFT, GROUP_SHIFT)
    for t in range(SSM_T):
        cols = slice(t * LANES, (t + 1) * LANES)
        cr_ref[:, cols] = _blockdiag(ccr_ref[t], rep_ref, c_mask)
        ci_ref[:, cols] = _blockdiag(cci_ref[t], rep_ref, c_mask)

    y = jnp.dot(_octet_lhs(u3_ref), kt_ref[...], preferred_element_type=F32)
    y += jnp.dot(xr_ref[...].astype(BF16), cr_ref[...], preferred_element_type=F32)
    y += jnp.dot(xi_ref[...].astype(BF16), ci_ref[...], preferred_element_type=F32)
    for t in range(SSM_T):
        y3_ref[t] = y[:, t * LANES:(t + 1) * LANES]


def _ssm_out(u3, taps, xs_re, xs_im, c_re, c_im, rep_group):
    n_sub = u3.shape[1]
    u_spec = pl.BlockSpec((SSM_T, n_sub, LANES), lambda j: (0, 0, j))
    x_spec = pl.BlockSpec((None, n_sub, OCT_STATE), lambda j: (j, 0, 0))
    c_spec = pl.BlockSpec((None, SSM_T, OCT_STATE, SSM_GROUP), lambda j: (j, 0, 0, 0))
    return pl.pallas_call(
        _ssm_out_kernel,
        grid=(N_OCT,),
        in_specs=[u_spec, pl.BlockSpec((None, SSM_T, LANES, SSM_GROUP), lambda j: (j, 0, 0, 0)),
                  x_spec, x_spec, c_spec, c_spec, pl.BlockSpec((SSM_GROUP, LANES), lambda j: (0, 0))],
        out_specs=u_spec,
        out_shape=jax.ShapeDtypeStruct((SSM_T, n_sub, SSM_W), F32),
        scratch_shapes=[pltpu.VMEM((OCT_K, OCT_K), BF16), pltpu.VMEM((OCT_STATE, OCT_K), BF16),
                        pltpu.VMEM((OCT_STATE, OCT_K), BF16)],
        compiler_params=_cparams("arbitrary"),
        name="ssm_out",
    )(u3, taps, xs_re, xs_im, c_re, c_im, rep_group)


def _ssm_matrices(a_re, a_im, log_step, b_re, b_im, c_re, c_im):
    hp = lax.Precision.HIGHEST
    G, P, T, O = SSM_GROUPS, SSM_STATE, SSM_T, SSM_OCT
    step = jnp.exp(log_step.astype(F32))[:, None]
    mag = jnp.exp(a_re * step)
    ang = a_im * step
    ab_re = mag * jnp.cos(ang)
    ab_im = mag * jnp.sin(ang)
    den = a_re * a_re + a_im * a_im
    n_re = ab_re - 1.0
    f_re = (n_re * a_re + ab_im * a_im) / den
    f_im = (ab_im * a_re - n_re * a_im) / den
    bb_re = f_re[..., None] * b_re - f_im[..., None] * b_im
    bb_im = f_re[..., None] * b_im + f_im[..., None] * b_re
    pr, pi = [jnp.ones_like(ab_re)], [jnp.zeros_like(ab_re)]
    for _ in range(T):
        pr.append(pr[-1] * ab_re - pi[-1] * ab_im)
        pi.append(pr[-2] * ab_im + pi[-1] * ab_re)
    pw_re = jnp.stack(pr)
    pw_im = jnp.stack(pi)
    ab_r = pw_re[:T, :, :, None] * bb_re[None] - pw_im[:T, :, :, None] * bb_im[None]
    ab_i = pw_re[:T, :, :, None] * bb_im[None] + pw_im[:T, :, :, None] * bb_re[None]
    taps = (jnp.einsum('gcp,tgpd->tgcd', c_re, ab_r, precision=hp)
            - jnp.einsum('gcp,tgpd->tgcd', c_im, ab_i, precision=hp))

    def per_octet(x, perm, rows, cols):
        return jnp.transpose(x, perm).reshape(N_OCT, T, rows, cols).astype(BF16)

    tap_rows = per_octet(taps.reshape(T, N_OCT, O, SSM_GROUP, SSM_GROUP), (1, 0, 2, 4, 3), LANES, SSM_GROUP)

    def inject(m):
        return per_octet(m[::-1].reshape(T, N_OCT, O, P, SSM_GROUP), (1, 0, 2, 4, 3), LANES, P)

    def readout(ca):
        return per_octet(ca.reshape(T, N_OCT, O, SSM_GROUP, P), (1, 0, 2, 4, 3), OCT_STATE, SSM_GROUP)

    ca_re = c_re[None] * pw_re[1:][:, :, None, :] - c_im[None] * pw_im[1:][:, :, None, :]
    ca_im = c_re[None] * pw_im[1:][:, :, None, :] + c_im[None] * pw_re[1:][:, :, None, :]
    return dict(taps=tap_rows, m_re=inject(ab_r), m_im=inject(ab_i),
                c_re=readout(ca_re), c_im=readout(-ca_im),
                rep_group=jnp.tile(jnp.eye(SSM_GROUP, dtype=BF16), (1, O)),
                rep_state=jnp.tile(jnp.eye(P, dtype=BF16), (1, O)),
                at_re=pw_re[T], at_im=pw_im[T])


def _ssm(u3, mats, first, seq_id, h0_re, h0_im):
    sre, sim = _ssm_state(u3, mats['m_re'], mats['m_im'], mats['rep_state'])
    xsr, xsi, xer, xei = _ssm_carry(first, seq_id, sre, sim, mats['at_re'], mats['at_im'], h0_re, h0_im)
    y3 = _ssm_out(u3, mats['taps'], xsr, xsi, mats['c_re'], mats['c_im'], mats['rep_group'])
    return y3, xer, xei


def _states_at(xe, idx):
    return jnp.transpose(xe[:, idx, :], (1, 0, 2)).reshape(idx.shape[0], SSM_GROUPS, SSM_STATE)


def _gelu_tanh(x):
    return 0.5 * x * (1.0 + jnp.tanh(math.sqrt(2.0 / math.pi) * (x + 0.044715 * (x * x * x))))


def _glu_kernel(y3_ref, u_ref, d_ref, w_ref, o_ref, wbf_ref, scr_ref):
    @pl.when(pl.program_id(0) == 0)
    def _():
        wbf_ref[...] = w_ref[...].astype(BF16)

    rows = y3_ref.shape[1]
    n_col = SSM_W // LANES
    for t in range(SSM_T):
        for c in range(n_col):
            scr_ref[c, pl.ds(t, rows, stride=SSM_T), :] = y3_ref[t, :, c * LANES:(c + 1) * LANES]
    y = jnp.concatenate([scr_ref[c] for c in range(n_col)], axis=1)
    ys = _gelu_tanh(y + d_ref[...] * u_ref[...])
    t = jnp.dot(ys.astype(BF16), wbf_ref[...], preferred_element_type=F32)
    o_ref[...] = (ys * jax.nn.sigmoid(t)).astype(o_ref.dtype)


def _glu(y3, z, d_all, w_all, l):
    nt = z.shape[0]
    tm = 512
    return pl.pallas_call(
        _glu_kernel,
        grid=(nt // tm,),
        in_specs=[
            pl.BlockSpec((SSM_T, tm // SSM_T, SSM_W), lambda i: (0, i, 0)),
            pl.BlockSpec((tm, SSM_W), lambda i: (i, 3 * ATT_W // SSM_W)),
            pl.BlockSpec((None, 1, SSM_W), lambda i: (l, 0, 0)),
            pl.BlockSpec((None, SSM_W, SSM_W), lambda i: (l, 0, 0)),
        ],
        out_specs=pl.BlockSpec((tm, SSM_W), lambda i: (i, 0)),
        out_shape=jax.ShapeDtypeStruct((nt, SSM_W), BF16),
        scratch_shapes=[pltpu.VMEM((SSM_W, SSM_W), BF16), pltpu.VMEM((SSM_W // LANES, tm, LANES), F32)],
        compiler_params=_cparams("arbitrary"),
        name="glu",
    )(y3, z, d_all.reshape(DEPTH, 1, SSM_W), w_all)


def _merge_kernel(o_ref, y_ref, ga_ref, gs_ref, wa_ref, ws_ref, m_ref):
    a = jnp.dot(o_ref[...], wa_ref[...].astype(BF16), preferred_element_type=F32)
    s = jnp.dot(y_ref[...], ws_ref[...].astype(BF16), preferred_element_type=F32)
    m_ref[...] = (jax.nn.sigmoid(ga_ref[...]) * a + jax.nn.sigmoid(gs_ref[...]) * s).astype(m_ref.dtype)


def _merge(o_p, o_s, ys2, z, w_ba_all, w_bs_all, l):
    nt = ys2.shape[0]
    tm, tn = 1024, 512
    ga0 = (3 * ATT_W + SSM_W) // tn
    gs0 = ga0 + D_MODEL // tn
    p_tiles = o_p.shape[0] // tm
    s_tiles = o_s.shape[0] // tm
    assert p_tiles * tm == o_p.shape[0] and s_tiles * tm == o_s.shape[0]

    def kern(op_ref, os_ref, *rest):
        i = pl.program_id(0)

        @pl.when(i < p_tiles)
        def _():
            _merge_kernel(op_ref, *rest)

        @pl.when(i >= p_tiles)
        def _():
            _merge_kernel(os_ref, *rest)

    return pl.pallas_call(
        kern,
        grid=(nt // tm, D_MODEL // tn),
        in_specs=[
            pl.BlockSpec((tm, ATT_W), lambda i, j: (jnp.minimum(i, p_tiles - 1), 0)),
            pl.BlockSpec((tm, ATT_W), lambda i, j: (jnp.maximum(i - p_tiles, 0), 0)),
            pl.BlockSpec((tm, SSM_W), lambda i, j: (i, 0)),
            pl.BlockSpec((tm, tn), lambda i, j: (i, ga0 + j)),
            pl.BlockSpec((tm, tn), lambda i, j: (i, gs0 + j)),
            pl.BlockSpec((None, ATT_W, tn), lambda i, j: (l, 0, j)),
            pl.BlockSpec((None, SSM_W, tn), lambda i, j: (l, 0, j)),
        ],
        out_specs=pl.BlockSpec((tm, tn), lambda i, j: (i, j)),
        out_shape=jax.ShapeDtypeStruct((nt, D_MODEL), BF16),
        compiler_params=_cparams("arbitrary", "arbitrary"),
        name="merge",
    )(o_p, o_s, ys2, z, z, w_ba_all, w_bs_all)


def _proj_residual_kernel(a_ref, w_ref, x_ref, g_ref, o_ref):
    r = jnp.dot(a_ref[...], w_ref[...].astype(BF16), preferred_element_type=F32)
    tm, tn = r.shape
    r3 = r.reshape(tm // CHUNK, CHUNK, tn) * g_ref[...]
    o_ref[...] = x_ref[...] + r3.reshape(tm, tn)


def _proj_residual(a, w_all, x, tab, l, which_gate):
    nt, k = a.shape
    tm, tn = 1024, 512
    cpt = tm // CHUNK
    return pl.pallas_call(
        _proj_residual_kernel,
        grid=(nt // tm, D_MODEL // tn),
        in_specs=[
            pl.BlockSpec((tm, k), lambda i, j: (i, 0)),
            pl.BlockSpec((None, k, tn), lambda i, j: (l, 0, j)),
            pl.BlockSpec((tm, tn), lambda i, j: (i, j)),
            _tab_spec(cpt, l, which_gate, width=tn, col=True),
        ],
        out_specs=pl.BlockSpec((tm, tn), lambda i, j: (i, j)),
        out_shape=jax.ShapeDtypeStruct((nt, D_MODEL), F32),
        compiler_params=_cparams("arbitrary", "arbitrary"),
        name="proj_residual",
    )(a, w_all, x, tab)


def _swiglu_step(h, wg_ref, wu_ref, wd_ref):
    hg = jnp.dot(h, wg_ref[...].astype(BF16), preferred_element_type=F32)
    hu = jnp.dot(h, wu_ref[...].astype(BF16), preferred_element_type=F32)
    a = (hg * jax.nn.sigmoid(hg)) * hu
    return jnp.dot(a.astype(BF16), wd_ref[...].astype(BF16), preferred_element_type=F32)


def _ffn_kernel(h_ref, wg_ref, wu_ref, wd_ref, o_ref):
    @pl.when(pl.program_id(1) == 0)
    def _():
        o_ref[...] = jnp.zeros_like(o_ref)

    o_ref[...] += _swiglu_step(h_ref[...], wg_ref, wu_ref, wd_ref)


def _ffn(h, w_gate, w_up, w_down, j):
    nt = h.shape[0]
    d_ff = w_gate.shape[-1]
    tm, tf = FFN_TM, FFN_TF
    return pl.pallas_call(
        _ffn_kernel,
        grid=(nt // tm, d_ff // tf),
        in_specs=[
            pl.BlockSpec((tm, D_MODEL), lambda i, f: (i, 0)),
            pl.BlockSpec((None, D_MODEL, tf), lambda i, f: (j, 0, f)),
            pl.BlockSpec((None, D_MODEL, tf), lambda i, f: (j, 0, f)),
            pl.BlockSpec((None, tf, D_MODEL), lambda i, f: (j, f, 0)),
        ],
        out_specs=pl.BlockSpec((tm, D_MODEL), lambda i, f: (i, 0)),
        out_shape=jax.ShapeDtypeStruct((nt, D_MODEL), F32),
        compiler_params=_cparams("arbitrary", "arbitrary"),
        name="ffn",
    )(h, w_gate, w_up, w_down)


def _moe_plan(route, nt):
    tm = FFN_TM
    n_assign = 2 * nt
    n_tiles = (n_assign + N_EXPERTS * (tm - 1)) // tm
    e_flat = jnp.concatenate([route[:, 0], route[:, 1]]).astype(jnp.int32)
    onehot = (e_flat[:, None] == jnp.arange(N_EXPERTS, dtype=jnp.int32)[None, :]).astype(jnp.int32)
    csum = jnp.cumsum(onehot, axis=0)
    counts = csum[-1]
    rank = jnp.sum((csum - onehot) * onehot, axis=1)
    tiles_per = (counts + tm - 1) // tm
    tile_end = jnp.cumsum(tiles_per)
    offsets = (tile_end - tiles_per) * tm
    dest = jnp.sum(onehot * offsets[None, :], axis=1) + rank
    tidx = jnp.arange(n_tiles, dtype=jnp.int32)
    tile_expert = jnp.minimum(jnp.sum((tidx[:, None] >= tile_end[None, :]).astype(jnp.int32), axis=1), N_EXPERTS - 1)
    tile_valid = (tidx < tile_end[-1]).astype(jnp.int32)
    tok = jnp.concatenate([jnp.arange(nt, dtype=jnp.int32)] * 2)
    src_token = jnp.zeros((n_tiles * tm,), jnp.int32).at[dest].set(tok)
    return dict(dest1=dest[:nt], dest2=dest[nt:], src_token=src_token,
                tile_expert=tile_expert.astype(jnp.int32), tile_valid=tile_valid)


def _row_copy(src_ref, row, dst_ref, r, sem):
    return pltpu.make_async_copy(src_ref.at[pl.ds(row, 1)], dst_ref.at[pl.ds(r, 1)], sem)


def _gather_start(idx_ref, base, src_ref, dst_ref, sem):
    rows = dst_ref.shape[0]

    def group(gi, c):
        for u in range(GATHER_UNROLL):
            r = gi * GATHER_UNROLL + u
            _row_copy(src_ref, idx_ref[base + r], dst_ref, r, sem).start(priority=u % 2)
        return c

    lax.fori_loop(0, rows // GATHER_UNROLL, group, 0)


def _gather_wait(src_ref, dst_ref, sem):
    pltpu.make_async_copy(src_ref.at[pl.ds(0, dst_ref.shape[0])], dst_ref, sem).wait()


def _gather_rows_kernel(idx_ref, src_ref, o_ref, buf_ref, sem):
    i = pl.program_id(0)
    n = pl.num_programs(0)
    slot = i % 2

    @pl.when(i == 0)
    def _():
        _gather_start(idx_ref, 0, src_ref, buf_ref.at[0], sem.at[0])

    @pl.when(i + 1 < n)
    def _():
        _gather_start(idx_ref, (i + 1) * GATHER_ROWS, src_ref, buf_ref.at[1 - slot], sem.at[1 - slot])

    _gather_wait(src_ref, buf_ref.at[slot], sem.at[slot])
    o_ref[...] = buf_ref[slot].astype(o_ref.dtype)


def _gather_rows(src, idx, out_dtype):
    n_out = idx.shape[0]
    width = src.shape[1]
    return pl.pallas_call(
        _gather_rows_kernel,
        grid_spec=pltpu.PrefetchScalarGridSpec(
            num_scalar_prefetch=1,
            grid=(n_out // GATHER_ROWS,),
            in_specs=[pl.BlockSpec(memory_space=pl.ANY)],
            out_specs=pl.BlockSpec((GATHER_ROWS, width), lambda i, *_: (i, 0)),
            scratch_shapes=[pltpu.VMEM((2, GATHER_ROWS, width), src.dtype), pltpu.SemaphoreType.DMA((2,))],
        ),
        out_shape=jax.ShapeDtypeStruct((n_out, width), out_dtype),
        compiler_params=_cparams("arbitrary"),
        name="moe_gather",
    )(idx, src)


def _moe_ffn_kernel(te_ref, tv_ref, h_ref, wg_ref, wu_ref, wd_ref, o_ref):
    t = pl.program_id(0)

    @pl.when(pl.program_id(1) == 0)
    def _():
        o_ref[...] = jnp.zeros_like(o_ref)

    @pl.when(tv_ref[t] == 1)
    def _():
        o_ref[...] += _swiglu_step(h_ref[...], wg_ref, wu_ref, wd_ref)


def _moe_ffn(h_sorted, tile_expert, tile_valid, w_gate, w_up, w_down, j):
    d_ff = w_gate.shape[-1]
    tm, tf = FFN_TM, FFN_TF
    n_tiles = h_sorted.shape[0] // tm
    nf = d_ff // tf

    def f_idx(t, f, tv):
        return jnp.where(tv[t] == 1, f, nf - 1)

    return pl.pallas_call(
        _moe_ffn_kernel,
        grid_spec=pltpu.PrefetchScalarGridSpec(
            num_scalar_prefetch=2,
            grid=(n_tiles, nf),
            in_specs=[
                pl.BlockSpec((tm, D_MODEL), lambda t, f, te, tv: (t, 0)),
                pl.BlockSpec((None, None, D_MODEL, tf), lambda t, f, te, tv: (j, te[t], 0, f_idx(t, f, tv))),
                pl.BlockSpec((None, None, D_MODEL, tf), lambda t, f, te, tv: (j, te[t], 0, f_idx(t, f, tv))),
                pl.BlockSpec((None, None, tf, D_MODEL), lambda t, f, te, tv: (j, te[t], f_idx(t, f, tv), 0)),
            ],
            out_specs=pl.BlockSpec((tm, D_MODEL), lambda t, f, te, tv: (t, 0)),
        ),
        out_shape=jax.ShapeDtypeStruct((n_tiles * tm, D_MODEL), F32),
        compiler_params=_cparams("arbitrary", "arbitrary"),
        name="moe_ffn",
    )(tile_expert, tile_valid, h_sorted, w_gate, w_up, w_down)


def _moe_combine_kernel(d1_ref, d2_ref, y_ref, x_ref, route_ref, g_ref, gn_ref, *rest, final_norm, split_tiles):
    n_out = 1 if split_tiles is None else 2
    o_refs, (a_ref, b_ref, sem_a, sem_b) = rest[:n_out], rest[n_out:]
    i = pl.program_id(0)
    n = pl.num_programs(0)
    slot = i % 2

    def start(step, s):
        _gather_start(d1_ref, step * GATHER_ROWS, y_ref, a_ref.at[s], sem_a.at[s])
        _gather_start(d2_ref, step * GATHER_ROWS, y_ref, b_ref.at[s], sem_b.at[s])

    @pl.when(i == 0)
    def _():
        start(0, 0)

    @pl.when(i + 1 < n)
    def _():
        start(i + 1, 1 - slot)

    _gather_wait(y_ref, a_ref.at[slot], sem_a.at[slot])
    _gather_wait(y_ref, b_ref.at[slot], sem_b.at[slot])
    route = route_ref[...]
    r = route[:, 2:3] * a_ref[slot] + route[:, 3:4] * b_ref[slot]
    tm, n_col = r.shape
    xn = x_ref[...] + (r.reshape(tm // CHUNK, CHUNK, n_col) * g_ref[...]).reshape(tm, n_col)
    if final_norm:
        xn = xn * lax.rsqrt(jnp.mean(xn * xn, axis=-1, keepdims=True) + EPS) * gn_ref[...]
    if split_tiles is None:
        o_refs[0][...] = xn
    else:
        @pl.when(i < split_tiles)
        def _():
            o_refs[0][...] = xn

        @pl.when(i >= split_tiles)
        def _():
            o_refs[1][...] = xn


def _moe_combine(y_sorted, dest1, dest2, x, route, tab, l, which_gate, g_final=None, split_rows=None):
    final_norm = g_final is not None
    gn = (g_final if final_norm else jnp.ones((D_MODEL,), F32)).reshape(1, D_MODEL)
    nt = x.shape[0]
    tm = GATHER_ROWS
    blk = pl.BlockSpec((tm, D_MODEL), lambda i, *_: (i, 0))
    buf = pltpu.VMEM((2, tm, D_MODEL), F32)
    if split_rows is None:
        split_tiles = None
        out_specs = blk
        out_shape = jax.ShapeDtypeStruct((nt, D_MODEL), F32)
    else:
        split_tiles = split_rows // tm
        assert split_tiles * tm == split_rows and 0 < split_rows < nt
        out_specs = [pl.BlockSpec((tm, D_MODEL), lambda i, *_: (jnp.minimum(i, split_tiles - 1), 0)),
                     pl.BlockSpec((tm, D_MODEL), lambda i, *_: (jnp.maximum(i - split_tiles, 0), 0))]
        out_shape = [jax.ShapeDtypeStruct((split_rows, D_MODEL), F32),
                     jax.ShapeDtypeStruct((nt - split_rows, D_MODEL), F32)]
    return pl.pallas_call(
        functools.partial(_moe_combine_kernel, final_norm=final_norm, split_tiles=split_tiles),
        grid_spec=pltpu.PrefetchScalarGridSpec(
            num_scalar_prefetch=2,
            grid=(nt // tm,),
            in_specs=[pl.BlockSpec(memory_space=pl.ANY), blk, pl.BlockSpec((tm, LANES), lambda i, *_: (i, 0)),
                      _tab_spec(tm // CHUNK, l, which_gate), pl.BlockSpec((1, D_MODEL), lambda i, *_: (0, 0))],
            out_specs=out_specs,
            scratch_shapes=[buf, buf, pltpu.SemaphoreType.DMA((2,)), pltpu.SemaphoreType.DMA((2,))],
        ),
        out_shape=out_shape,
        compiler_params=_cparams("arbitrary"),
        name="moe_combine",
    )(dest1, dest2, y_sorted, x, route, tab, gn)


def _residual_kernel(x_ref, r_ref, g_ref, o_ref):
    r = r_ref[...]
    tm, n = r.shape
    o_ref[...] = x_ref[...] + (r.reshape(tm // CHUNK, CHUNK, n) * g_ref[...]).reshape(tm, n)


def _residual(x, r, tab, l, which_gate):
    nt = x.shape[0]
    tm = 512
    blk = pl.BlockSpec((tm, D_MODEL), lambda i: (i, 0))
    return pl.pallas_call(
        _residual_kernel, grid=(nt // tm,),
        in_specs=[blk, blk, _tab_spec(tm // CHUNK, l, which_gate)],
        out_specs=blk, out_shape=jax.ShapeDtypeStruct((nt, D_MODEL), F32),
        compiler_params=_cparams("arbitrary"), name="residual",
    )(x, r, tab)


def _residual_norm_kernel(x_ref, r_ref, gate_ref, g_ref, sh_ref, sc_ref, xo_ref, h_ref):
    r = r_ref[...]
    tm, n = r.shape
    xn = x_ref[...] + (r.reshape(tm // CHUNK, CHUNK, n) * gate_ref[...]).reshape(tm, n)
    xo_ref[...] = xn
    h_ref[...] = _modulate(xn, g_ref, sh_ref, sc_ref).astype(BF16)


def _residual_norm(x, r, tab, l, which_gate, g_all, l_next, which_shift, which_scale):
    nt = x.shape[0]
    tm = 512
    cpt = tm // CHUNK
    blk = pl.BlockSpec((tm, D_MODEL), lambda i: (i, 0))
    return pl.pallas_call(
        _residual_norm_kernel, grid=(nt // tm,),
        in_specs=[blk, blk, _tab_spec(cpt, l, which_gate),
                  pl.BlockSpec((None, 1, D_MODEL), lambda i: (l_next, 0, 0)),
                  _tab_spec(cpt, l_next, which_shift), _tab_spec(cpt, l_next, which_scale)],
        out_specs=[blk, blk],
        out_shape=[jax.ShapeDtypeStruct((nt, D_MODEL), F32), jax.ShapeDtypeStruct((nt, D_MODEL), BF16)],
        compiler_params=_cparams("arbitrary"), name="residual_norm",
    )(x, r, tab, g_all.reshape(DEPTH, 1, D_MODEL), tab, tab)


def _kv_rows_kernel(z_ref, o_ref):
    for h in range(N_HEADS):
        o_ref[:, h, :] = z_ref[:, h * HEAD_DIM:(h + 1) * HEAD_DIM]


def _kv_rows(z, n_pb, seq, n_sb):
    rows = PAST_WINDOW
    per_seq = seq // rows
    s_blocks = n_sb * CHUNK // rows
    assert per_seq * rows == seq and s_blocks * rows == n_sb * CHUNK

    def row_block(r):
        return jnp.where(r < n_pb, (r + 1) * per_seq - 1, n_pb * per_seq + (r - n_pb))

    n_blocks = n_pb + s_blocks
    return pl.pallas_call(
        _kv_rows_kernel, grid=(2, n_blocks),
        in_specs=[pl.BlockSpec((rows, ATT_W), lambda w, r: (row_block(r), 1 + w))],
        out_specs=pl.BlockSpec((None, rows, N_HEADS, HEAD_DIM), lambda w, r: (w, r, 0, 0)),
        out_shape=jax.ShapeDtypeStruct((2, n_blocks * rows, N_HEADS, HEAD_DIM), F32),
        compiler_params=_cparams("arbitrary", "arbitrary"), name="kv_rows",
    )(z)


def _final_norm_kernel(x_ref, g_ref, o_ref):
    x = x_ref[...]
    o_ref[...] = x * lax.rsqrt(jnp.mean(x * x, axis=-1, keepdims=True) + EPS) * g_ref[...]


def _final_norm(x, g):
    nt = x.shape[0]
    tm = 512
    blk = pl.BlockSpec((tm, D_MODEL), lambda i: (i, 0))
    return pl.pallas_call(
        _final_norm_kernel, grid=(nt // tm,),
        in_specs=[blk, pl.BlockSpec((1, D_MODEL), lambda i: (0, 0))],
        out_specs=blk, out_shape=jax.ShapeDtypeStruct((nt, D_MODEL), F32),
        compiler_params=_cparams("arbitrary"), name="final_norm",
    )(x, g.reshape(1, D_MODEL))


def kernel(x_prompt, x_sample, cache_k, cache_v, state_ssm_re, state_ssm_im, c_prompt, c_sample, w_ada, b_ada, g_norm_mix, g_norm_ffn, g_norm_final, w_in, rel_bias, ssm_a_re, ssm_a_im, ssm_log_step, ssm_b_re, ssm_b_im, ssm_c_re, ssm_c_im, ssm_d, w_ssm_glu, w_branch_attn, w_branch_ssm, w_out, w_ffn_gate, w_ffn_up, w_ffn_down, w_router, b_router, w_exp_gate, w_exp_up, w_exp_down):
    n_pb, seq = x_prompt.shape[0], x_prompt.shape[1]
    n_sb, dec = x_sample.shape[0], x_sample.shape[1]
    assert dec == CHUNK and cache_k.shape[2] == PAST_WINDOW and seq % ATT_ROWS == 0
    np_tok = n_pb * seq
    nt = np_tok + n_sb * dec
    n_batch = n_pb + n_sb

    x = jnp.concatenate([x_prompt.reshape(np_tok, D_MODEL), x_sample.reshape(n_sb * dec, D_MODEL)], axis=0)

    c_all = jnp.concatenate([c_prompt, c_sample], axis=0)
    c_pad = jnp.pad(c_all, ((0, -n_batch % 8), (0, 0)))
    ada = _ada(c_pad, w_ada, b_ada)
    ada_p = jnp.broadcast_to(ada[:, :n_pb, None, :], (DEPTH, n_pb, seq // CHUNK, 6 * D_MODEL))
    tab = jnp.concatenate([ada_p.reshape(DEPTH, np_tok // CHUNK, 6 * D_MODEL), ada[:, n_pb:n_batch]], axis=1)
    tab = tab.reshape(DEPTH, nt // CHUNK, 6, 1, D_MODEL)

    sub_p, sub_s = seq // SSM_T, dec // SSM_T
    seq_id = jnp.concatenate([jnp.repeat(jnp.arange(n_pb), sub_p), n_pb + jnp.repeat(jnp.arange(n_sb), sub_s)])
    seq_id = seq_id.astype(jnp.int32)
    first = jnp.concatenate([jnp.arange(n_pb * sub_p) % sub_p == 0, jnp.arange(n_sb * sub_s) % sub_s == 0])
    first = first.astype(jnp.int32)
    last_p = (jnp.arange(n_pb) + 1) * sub_p - 1
    last_s = n_pb * sub_p + (jnp.arange(n_sb) + 1) * sub_s - 1
    zeros_h0 = jnp.zeros((n_pb, SSM_GROUPS, SSM_STATE), F32)

    outs = {k: [] for k in ('kp', 'vp', 'srp', 'sip', 'ks', 'vs', 'srs', 'sis')}
    keep = min(PAST_WINDOW, seq)
    h = _norm_mod(x, g_norm_mix, tab, 0, SHIFT1, SCALE1)
    for l in range(DEPTH):
        last = l == DEPTH - 1
        z, u3 = _in_proj(h, w_in, l)

        b64 = _attention_bias(rel_bias[l])
        o_p = _attn_prompt(z, b64, n_pb, seq)
        o_s = _attn_sample(z, cache_k, cache_v, l, b64, np_tok, n_sb)

        mats = _ssm_matrices(ssm_a_re[l], ssm_a_im[l], ssm_log_step[l], ssm_b_re[l], ssm_b_im[l],
                             ssm_c_re[l], ssm_c_im[l])
        h0_re = jnp.concatenate([zeros_h0, state_ssm_re[l].astype(F32)], axis=0)
        h0_im = jnp.concatenate([zeros_h0, state_ssm_im[l].astype(F32)], axis=0)
        y3, xe_re, xe_im = _ssm(u3, mats, first, seq_id, h0_re, h0_im)

        ys2 = _glu(y3, z, ssm_d, w_ssm_glu, l)
        merged = _merge(o_p, o_s, ys2, z, w_branch_attn, w_branch_ssm, l)
        x = _proj_residual(merged, w_out, x, tab, l, GATE1)

        j = l // 2
        if l % 2 == 0:
            h2 = _norm_mod(x, g_norm_ffn, tab, l, SHIFT2, SCALE2)
            r = _ffn(h2, w_ffn_gate, w_ffn_up, w_ffn_down, j)
            if last:
                yn = _final_norm(_residual(x, r, tab, l, GATE2), g_norm_final)
                yn_p, yn_s = yn[:np_tok], yn[np_tok:]
            else:
                x, h = _residual_norm(x, r, tab, l, GATE2, g_norm_mix, l + 1, SHIFT1, SCALE1)
        else:
            h2, route = _norm_mod(x, g_norm_ffn, tab, l, SHIFT2, SCALE2, router=(w_router[j], b_router[j]))
            plan = _moe_plan(route, nt)
            h_sorted = _gather_rows(h2, plan['src_token'], BF16)
            y_sorted = _moe_ffn(h_sorted, plan['tile_expert'], plan['tile_valid'], w_exp_gate, w_exp_up, w_exp_down, j)
            if last:
                yn_p, yn_s = _moe_combine(y_sorted, plan['dest1'], plan['dest2'], x, route, tab, l, GATE2,
                                          g_final=g_norm_final, split_rows=np_tok)
            else:
                x = _moe_combine(y_sorted, plan['dest1'], plan['dest2'], x, route, tab, l, GATE2)
                h = _norm_mod(x, g_norm_mix, tab, l + 1, SHIFT1, SCALE1)

        kv = _kv_rows(z, n_pb, seq, n_sb)
        outs['kp'].append(kv[0, :n_pb * keep].reshape(n_pb, keep, N_HEADS, HEAD_DIM))
        outs['vp'].append(kv[1, :n_pb * keep].reshape(n_pb, keep, N_HEADS, HEAD_DIM))
        outs['ks'].append(kv[0, n_pb * keep:].reshape(n_sb, dec, N_HEADS, HEAD_DIM))
        outs['vs'].append(kv[1, n_pb * keep:].reshape(n_sb, dec, N_HEADS, HEAD_DIM))
        outs['srp'].append(_states_at(xe_re, last_p))
        outs['sip'].append(_states_at(xe_im, last_p))
        outs['srs'].append(_states_at(xe_re, last_s))
        outs['sis'].append(_states_at(xe_im, last_s))

    y_prompt = yn_p.reshape(n_pb, seq, D_MODEL)
    y_sample = yn_s.reshape(n_sb, dec, D_MODEL)
    st = lambda name: jnp.stack(outs[name])
    return (y_prompt, y_sample, st('kp'), st('vp'), st('srp'), st('sip'),
            st('ks'), st('vs'), st('srs'), st('sis'))
```

```python
import functools
import math

import jax
import jax.numpy as jnp
from jax import lax
from jax.experimental import pallas as pl
from jax.experimental.pallas import tpu as pltpu

F32 = jnp.float32
BF16 = jnp.bfloat16

D_MODEL = 2048
DEPTH = 2
CHUNK = 64
PAST_CHUNKS = 8
PAST_WINDOW = PAST_CHUNKS * CHUNK
BAND = PAST_WINDOW + CHUNK
N_HEADS = 8
HEAD_DIM = 128
ATT_W = N_HEADS * HEAD_DIM
ATT_SCALE = HEAD_DIM ** -0.5
REL_CLIP = 256
SSM_W = 1024
SSM_GROUP = 16
SSM_GROUPS = SSM_W // SSM_GROUP
SSM_STATE = 64
N_EXPERTS = 8
IN_COLS = 3 * ATT_W + SSM_W + 2 * D_MODEL
EPS = 1e-6
NEG_INF = -1e30

LANES = 128
VMEM_LIMIT_BYTES = 56 * 1024 * 1024

SSM_T = 16
SSM_OCT = LANES // SSM_GROUP
N_OCT = SSM_GROUPS // SSM_OCT
OCT_K = SSM_T * LANES
OCT_STATE = SSM_OCT * SSM_STATE
GROUP_SHIFT = SSM_GROUP.bit_length() - 1
STATE_SHIFT = SSM_STATE.bit_length() - 1
assert (1 << GROUP_SHIFT, 1 << STATE_SHIFT) == (SSM_GROUP, SSM_STATE)
ATT_GROUP = 4
ATT_ROWS = ATT_GROUP * CHUNK
ATT_KEYS = 3 * ATT_ROWS

IN_TM, IN_TN = 1536, 512
FFN_TM = 768
MOE_TM = 832
FFN_TF = 512
GATHER_ROWS = 256
DISPATCH_ROWS = MOE_TM // 4
GATHER_UNROLL = 8

SHIFT1, SCALE1, GATE1, SHIFT2, SCALE2, GATE2 = range(6)


def _cparams(*sem):
    return pltpu.CompilerParams(dimension_semantics=sem, vmem_limit_bytes=VMEM_LIMIT_BYTES)


def _tab_spec(cpt, l, which, width=D_MODEL, col=None):
    if col is None:
        return pl.BlockSpec((None, cpt, None, 1, width), lambda i, *_: (l, i, which, 0, 0))
    return pl.BlockSpec((None, cpt, None, 1, width), lambda i, j, *_: (l, i, which, 0, j))


def _ada_kernel(c_ref, w_ref, b_ref, o_ref):
    c = c_ref[...]
    a = (c * jax.nn.sigmoid(c)).astype(BF16)
    o_ref[...] = jnp.dot(a, w_ref[...].astype(BF16), preferred_element_type=F32) + b_ref[...]


def _ada(c_pad, w_ada, b_ada):
    rows = c_pad.shape[0]
    n = w_ada.shape[-1]
    tn = 1024
    return pl.pallas_call(
        _ada_kernel,
        grid=(DEPTH, n // tn),
        in_specs=[
            pl.BlockSpec((rows, D_MODEL), lambda l, j: (0, 0)),
            pl.BlockSpec((None, D_MODEL, tn), lambda l, j: (l, 0, j)),
            pl.BlockSpec((None, 1, tn), lambda l, j: (l, 0, j)),
        ],
        out_specs=pl.BlockSpec((None, rows, tn), lambda l, j: (l, 0, j)),
        out_shape=jax.ShapeDtypeStruct((DEPTH, rows, n), F32),
        compiler_params=_cparams("arbitrary", "arbitrary"),
        name="ada",
    )(c_pad, w_ada, b_ada.reshape(DEPTH, 1, n))


def _modulated(x_ref, g_ref, sh_ref, sc_ref):
    return _modulate(x_ref[...], g_ref, sh_ref, sc_ref)


def _modulate(x, g_ref, sh_ref, sc_ref):
    tm = x.shape[0]
    y = x * lax.rsqrt(jnp.mean(x * x, axis=-1, keepdims=True) + EPS) * g_ref[...]
    y3 = y.reshape(tm // CHUNK, CHUNK, D_MODEL)
    h = y3 * (1.0 + sc_ref[...]) + sh_ref[...]
    return h.reshape(tm, D_MODEL)


def _norm_mod_kernel(x_ref, g_ref, sh_ref, sc_ref, h_ref):
    h_ref[...] = _modulated(x_ref, g_ref, sh_ref, sc_ref).astype(BF16)


def _split_bf16(a):
    hi = a.astype(BF16)
    lo = (a - hi.astype(F32)).astype(BF16)
    return hi, lo


def _norm_mod_router_kernel(x_ref, g_ref, sh_ref, sc_ref, wr_ref, br_ref, h_ref, route_ref):
    h = _modulated(x_ref, g_ref, sh_ref, sc_ref)
    h_ref[...] = h
    h_hi, h_lo = _split_bf16(h)
    w_hi, w_lo = _split_bf16(wr_ref[...])
    dot = functools.partial(jnp.dot, preferred_element_type=F32)
    logits = (dot(h_hi, w_hi) + (dot(h_hi, w_lo) + dot(h_lo, w_hi)) + dot(h_lo, w_lo)) + br_ref[...]
    lane = lax.broadcasted_iota(jnp.int32, logits.shape, 1).astype(F32)
    lg = jnp.where(lane < N_EXPERTS, logits, -jnp.inf)
    m1 = jnp.max(lg, axis=-1, keepdims=True)
    i1 = jnp.min(jnp.where(lg == m1, lane, float(LANES)), axis=-1, keepdims=True)
    lg2 = jnp.where(lane == i1, -jnp.inf, lg)
    m2 = jnp.max(lg2, axis=-1, keepdims=True)
    i2 = jnp.min(jnp.where(lg2 == m2, lane, float(LANES)), axis=-1, keepdims=True)
    e2 = jnp.exp(m2 - m1)
    den = 1.0 + e2
    route = jnp.where(lane == 0.0, i1, jnp.where(lane == 1.0, i2, jnp.where(lane == 2.0, 1.0 / den, e2 / den)))
    route_ref[...] = jnp.where(lane < 4.0, route, 0.0)


def _split_rows(kernel_fn, pos, head_tiles):
    def kern(*refs):
        before, after = refs[:pos], refs[pos + 2:]
        i = pl.program_id(0)

        @pl.when(i < head_tiles)
        def _():
            kernel_fn(*before, refs[pos], *after)

        @pl.when(i >= head_tiles)
        def _():
            kernel_fn(*before, refs[pos + 1], *after)

    return kern


def _split_specs(block, head_tiles, with_col):
    if with_col:
        return [pl.BlockSpec(block, lambda i, j: (jnp.minimum(i, head_tiles - 1), j)),
                pl.BlockSpec(block, lambda i, j: (jnp.maximum(i - head_tiles, 0), j))]
    return [pl.BlockSpec(block, lambda i: (jnp.minimum(i, head_tiles - 1), 0)),
            pl.BlockSpec(block, lambda i: (jnp.maximum(i - head_tiles, 0), 0))]


def _norm_mod(x, g_all, tab, l, which_shift, which_scale, router=None, x_tail=None):
    tm = 512
    cpt = tm // CHUNK
    in_specs = [
        pl.BlockSpec((tm, D_MODEL), lambda i: (i, 0)),
        pl.BlockSpec((None, 1, D_MODEL), lambda i: (l, 0, 0)),
        _tab_spec(cpt, l, which_shift),
        _tab_spec(cpt, l, which_scale),
    ]
    h_spec = pl.BlockSpec((tm, D_MODEL), lambda i: (i, 0))
    args = [x, g_all.reshape(DEPTH, 1, D_MODEL), tab, tab]
    if x_tail is not None:
        assert router is None and x.shape[0] % tm == 0 and x_tail.shape[0] % tm == 0
        head_tiles = x.shape[0] // tm
        nt = x.shape[0] + x_tail.shape[0]
        return pl.pallas_call(
            _split_rows(_norm_mod_kernel, 0, head_tiles), grid=(nt // tm,),
            in_specs=_split_specs((tm, D_MODEL), head_tiles, False) + in_specs[1:], out_specs=h_spec,
            out_shape=jax.ShapeDtypeStruct((nt, D_MODEL), BF16),
            compiler_params=_cparams("arbitrary"), name="norm_mod",
        )(x, x_tail, *args[1:])
    nt = x.shape[0]
    if router is None:
        return pl.pallas_call(
            _norm_mod_kernel, grid=(nt // tm,), in_specs=in_specs, out_specs=h_spec,
            out_shape=jax.ShapeDtypeStruct((nt, D_MODEL), BF16),
            compiler_params=_cparams("arbitrary"), name="norm_mod",
        )(*args)
    w_r, b_r = router
    w_pad = jnp.zeros((D_MODEL, LANES), F32).at[:, :N_EXPERTS].set(w_r)
    b_pad = jnp.zeros((1, LANES), F32).at[0, :N_EXPERTS].set(b_r)
    in_specs += [
        pl.BlockSpec((D_MODEL, LANES), lambda i: (0, 0)),
        pl.BlockSpec((1, LANES), lambda i: (0, 0)),
    ]
    return pl.pallas_call(
        _norm_mod_router_kernel, grid=(nt // tm,), in_specs=in_specs,
        out_specs=[h_spec, pl.BlockSpec((tm, LANES), lambda i: (i, 0))],
        out_shape=[jax.ShapeDtypeStruct((nt, D_MODEL), F32), jax.ShapeDtypeStruct((nt, LANES), F32)],
        compiler_params=_cparams("arbitrary"), name="norm_mod_router",
    )(*args, w_pad, b_pad)


def _in_proj_kernel(x_ref, w_ref, z_ref, u3_ref, scr_ref):
    j = pl.program_id(1)
    r = jnp.dot(x_ref[...], w_ref[...].astype(BF16), preferred_element_type=F32)
    z_ref[...] = r
    tm, tn = r.shape
    u_first = 3 * ATT_W // tn
    for part in range(SSM_W // tn):
        @pl.when(j == u_first + part)
        def _():
            for c in range(tn // LANES):
                scr_ref[c] = r[:, c * LANES:(c + 1) * LANES]
            for s in range(SSM_T):
                for c in range(tn // LANES):
                    lo = part * tn + c * LANES
                    rows = scr_ref[c, pl.ds(s, tm // SSM_T, stride=SSM_T), :]
                    u3_ref[s, :, lo:lo + LANES] = rows.astype(BF16)


def _in_proj(h, w_all, l):
    nt, k = h.shape
    n = w_all.shape[-1]
    tm, tn = IN_TM, IN_TN
    return pl.pallas_call(
        _in_proj_kernel,
        grid=(nt // tm, n // tn),
        in_specs=[
            pl.BlockSpec((tm, k), lambda i, j: (i, 0)),
            pl.BlockSpec((None, k, tn), lambda i, j: (l, 0, j)),
        ],
        out_specs=[
            pl.BlockSpec((tm, tn), lambda i, j: (i, j)),
            pl.BlockSpec((SSM_T, tm // SSM_T, SSM_W), lambda i, j: (0, i, 0)),
        ],
        out_shape=[
            jax.ShapeDtypeStruct((nt, n), F32),
            jax.ShapeDtypeStruct((SSM_T, nt // SSM_T, SSM_W), BF16),
        ],
        scratch_shapes=[pltpu.VMEM((tn // LANES, tm, LANES), F32)],
        compiler_params=_cparams("arbitrary", "arbitrary"),
        name="in_proj",
    )(h, w_all)


def _softmax_rows(s):
    m = jnp.max(s, axis=-1, keepdims=True)
    e = jnp.exp(s - m)
    return e / jnp.sum(e, axis=-1, keepdims=True)


def _attn_prompt_kernel(q_ref, k0_ref, k1_ref, k2_ref, v0_ref, v1_ref, v2_ref, b64_ref, o_ref, bias_ref):
    g = pl.program_id(1)

    @pl.when((pl.program_id(0) == 0) & (g == 0))
    def _():
        bias_ref[...] = jnp.full(bias_ref.shape, NEG_INF, F32)
        for c in range(ATT_GROUP):
            bias_ref[:, c * CHUNK:(c + 1) * CHUNK, c * CHUNK:c * CHUNK + BAND] = b64_ref[...]

    col = lax.broadcasted_iota(jnp.int32, (ATT_ROWS, ATT_KEYS), 1)
    visible = (g * ATT_ROWS - 2 * ATT_ROWS + col) >= 0
    for h in range(N_HEADS):
        sl = slice(h * HEAD_DIM, (h + 1) * HEAD_DIM)
        qh = q_ref[:, sl].astype(BF16)
        kh = jnp.concatenate([k0_ref[:, sl], k1_ref[:, sl], k2_ref[:, sl]], axis=0).astype(BF16)
        vh = jnp.concatenate([v0_ref[:, sl], v1_ref[:, sl], v2_ref[:, sl]], axis=0).astype(BF16)
        s = lax.dot_general(qh, kh, (((1,), (1,)), ((), ())), preferred_element_type=F32)
        s = s * ATT_SCALE + bias_ref[h]
        s = jnp.where(visible, s, NEG_INF)
        p = _softmax_rows(s).astype(BF16)
        o_ref[:, sl] = jnp.dot(p, vh, preferred_element_type=F32).astype(o_ref.dtype)


def _attn_prompt(z, b64, n_batch, seq):
    groups = seq // ATT_ROWS
    blk = (ATT_ROWS, ATT_W)

    def kv_spec(col, back):
        return pl.BlockSpec(blk, lambda b, g: (b * groups + jnp.maximum(g - back, 0), col))

    return pl.pallas_call(
        _attn_prompt_kernel,
        grid=(n_batch, groups),
        in_specs=[
            pl.BlockSpec(blk, lambda b, g: (b * groups + g, 0)),
            kv_spec(1, 2), kv_spec(1, 1), kv_spec(1, 0),
            kv_spec(2, 2), kv_spec(2, 1), kv_spec(2, 0),
            pl.BlockSpec((N_HEADS, CHUNK, BAND), lambda b, g: (0, 0, 0)),
        ],
        out_specs=pl.BlockSpec(blk, lambda b, g: (b * groups + g, 0)),
        out_shape=jax.ShapeDtypeStruct((n_batch * seq, ATT_W), BF16),
        scratch_shapes=[pltpu.VMEM((N_HEADS, ATT_ROWS, ATT_KEYS), F32)],
        compiler_params=_cparams("arbitrary", "arbitrary"),
        name="attn_prompt",
    )(z, z, z, z, z, z, z, b64)


def _attn_sample_kernel(q_ref, kn_ref, vn_ref, kc_ref, vc_ref, bias_ref, o_ref):
    for h in range(N_HEADS):
        sl = slice(h * HEAD_DIM, (h + 1) * HEAD_DIM)
        qh = q_ref[:, sl].astype(BF16)
        kh = jnp.concatenate([kc_ref[:, h, :], kn_ref[:, sl]], axis=0).astype(BF16)
        vh = jnp.concatenate([vc_ref[:, h, :], vn_ref[:, sl]], axis=0).astype(BF16)
        s = lax.dot_general(qh, kh, (((1,), (1,)), ((), ())), preferred_element_type=F32)
        s = s * ATT_SCALE + bias_ref[h]
        p = _softmax_rows(s).astype(BF16)
        o_ref[:, sl] = jnp.dot(p, vh, preferred_element_type=F32).astype(o_ref.dtype)


def _attn_sample(z, cache_k, cache_v, l, b64, row0, n_batch):
    base = row0 // CHUNK
    blk = (CHUNK, ATT_W)
    cache_spec = pl.BlockSpec((None, None, PAST_WINDOW, N_HEADS, HEAD_DIM), lambda b: (l, b, 0, 0, 0))
    return pl.pallas_call(
        _attn_sample_kernel,
        grid=(n_batch,),
        in_specs=[
            pl.BlockSpec(blk, lambda b: (base + b, 0)),
            pl.BlockSpec(blk, lambda b: (base + b, 1)),
            pl.BlockSpec(blk, lambda b: (base + b, 2)),
            cache_spec, cache_spec,
            pl.BlockSpec((N_HEADS, CHUNK, BAND), lambda b: (0, 0, 0)),
        ],
        out_specs=pl.BlockSpec(blk, lambda b: (b, 0)),
        out_shape=jax.ShapeDtypeStruct((n_batch * CHUNK, ATT_W), BF16),
        compiler_params=_cparams("arbitrary"),
        name="attn_sample",
    )(z, z, z, cache_k, cache_v, b64)


def _attention_bias(rel_bias):
    rev = rel_bias[::-1].astype(F32)
    flat = PAST_WINDOW - REL_CLIP + CHUNK - 1
    vec = jnp.concatenate([jnp.broadcast_to(rev[:1], (flat, N_HEADS)), rev[:BAND + CHUNK - 1 - flat]], axis=0)
    rows = [vec[CHUNK - 1 - qi:CHUNK - 1 - qi + BAND] for qi in range(CHUNK)]
    return jnp.transpose(jnp.stack(rows), (2, 0, 1))


def _octet_lhs(u3_ref):
    return jnp.concatenate([u3_ref[s] for s in range(SSM_T)], axis=1)


def _group_mask(rows, cols, row_shift, col_shift):
    r = lax.shift_right_logical(lax.broadcasted_iota(jnp.int32, (rows, cols), 0), row_shift)
    c = lax.shift_right_logical(lax.broadcasted_iota(jnp.int32, (rows, cols), 1), col_shift)
    return r == c


def _blockdiag(x, rep_ref, mask):
    e = jnp.dot(x, rep_ref[...], preferred_element_type=F32)
    return jnp.where(mask, e, 0.0).astype(BF16)


def _ssm_state_kernel(u3_ref, mre_ref, mim_ref, rep_ref, sre_ref, sim_ref, w_ref):
    mask = _group_mask(LANES, OCT_STATE, GROUP_SHIFT, STATE_SHIFT)
    for s in range(SSM_T):
        rows = slice(s * LANES, (s + 1) * LANES)
        w_ref[rows, :OCT_STATE] = _blockdiag(mre_ref[s], rep_ref, mask)
        w_ref[rows, OCT_STATE:] = _blockdiag(mim_ref[s], rep_ref, mask)
    s = jnp.dot(_octet_lhs(u3_ref), w_ref[...], preferred_element_type=F32)
    sre_ref[...] = s[:, :OCT_STATE]
    sim_ref[...] = s[:, OCT_STATE:]


def _ssm_state(u3, m_re, m_im, rep_state):
    n_sub = u3.shape[1]
    out = jax.ShapeDtypeStruct((N_OCT, n_sub, OCT_STATE), F32)
    m_spec = pl.BlockSpec((None, SSM_T, LANES, SSM_STATE), lambda j: (j, 0, 0, 0))
    return pl.pallas_call(
        _ssm_state_kernel,
        grid=(N_OCT,),
        in_specs=[
            pl.BlockSpec((SSM_T, n_sub, LANES), lambda j: (0, 0, j)),
            m_spec, m_spec,
            pl.BlockSpec((SSM_STATE, OCT_STATE), lambda j: (0, 0)),
        ],
        out_specs=[pl.BlockSpec((None, n_sub, OCT_STATE), lambda j: (j, 0, 0))] * 2,
        out_shape=[out, out],
        scratch_shapes=[pltpu.VMEM((OCT_K, 2 * OCT_STATE), BF16)],
        compiler_params=_cparams("arbitrary"),
        name="ssm_state",
    )(u3, m_re, m_im, rep_state)


def _ssm_carry_kernel(first_ref, seq_ref, sre_ref, sim_ref, ar_ref, ai_ref, h0r_ref, h0i_ref,
                      xsr_ref, xsi_ref, xer_ref, xei_ref, st_ref):
    blk = pl.program_id(0)
    steps = sre_ref.shape[1]

    @pl.when(blk == 0)
    def _():
        st_ref[...] = jnp.zeros_like(st_ref)

    ar = ar_ref[...]
    ai = ai_ref[...]

    def body(kk, carry):
        k = blk * steps + kk
        is_first = first_ref[k] == 1
        sq = seq_ref[k]
        xr = jnp.where(is_first, h0r_ref[sq], st_ref[0])
        xi = jnp.where(is_first, h0i_ref[sq], st_ref[1])
        xsr_ref[:, kk, :] = xr
        xsi_ref[:, kk, :] = xi
        nr = ar * xr - ai * xi + sre_ref[:, kk, :]
        ni = ar * xi + ai * xr + sim_ref[:, kk, :]
        xer_ref[:, kk, :] = nr
        xei_ref[:, kk, :] = ni
        st_ref[0] = nr
        st_ref[1] = ni
        return carry

    lax.fori_loop(0, steps, body, 0)


def _ssm_carry(first, seq_id, sre, sim, at_re, at_im, h0_re, h0_im):
    n_sub = sre.shape[1]
    steps = 64
    tile = (N_OCT, OCT_STATE)
    blk = pl.BlockSpec((N_OCT, steps, OCT_STATE), lambda i, *_: (0, i, 0))
    const = pl.BlockSpec(tile, lambda i, *_: (0, 0))
    tab = pl.BlockSpec((h0_re.shape[0],) + tile, lambda i, *_: (0, 0, 0))
    out = jax.ShapeDtypeStruct((N_OCT, n_sub, OCT_STATE), F32)
    return pl.pallas_call(
        _ssm_carry_kernel,
        grid_spec=pltpu.PrefetchScalarGridSpec(
            num_scalar_prefetch=2,
            grid=(n_sub // steps,),
            in_specs=[blk, blk, const, const, tab, tab],
            out_specs=[blk, blk, blk, blk],
            scratch_shapes=[pltpu.VMEM((2,) + tile, F32)],
        ),
        out_shape=[out, out, out, out],
        compiler_params=_cparams("arbitrary"),
        name="ssm_carry",
    )(first, seq_id, sre, sim, at_re.reshape(tile), at_im.reshape(tile),
      h0_re.reshape((-1,) + tile), h0_im.reshape((-1,) + tile))


def _ssm_out_kernel(u3_ref, taps_ref, xr_ref, xi_ref, ccr_ref, cci_ref, rep_ref, y3_ref, kt_ref, cr_ref, ci_ref):
    @pl.when(pl.program_id(0) == 0)
    def _():
        kt_ref[...] = jnp.zeros_like(kt_ref)

    tap_mask = _group_mask(LANES, LANES, GROUP_SHIFT, GROUP_SHIFT)
    for tau in range(SSM_T):
        block = _blockdiag(taps_ref[tau], rep_ref, tap_mask)
        for s in range(SSM_T - tau):
            t = s + tau
            kt_ref[s * LANES:(s + 1) * LANES, t * LANES:(t + 1) * LANES] = block
    c_mask = _group_mask(OCT_STATE, LANES, STATE_SHIFT, GROUP_SHIFT)
    for t in range(SSM_T):
        cols = slice(t * LANES, (t + 1) * LANES)
        cr_ref[:, cols] = _blockdiag(ccr_ref[t], rep_ref, c_mask)
        ci_ref[:, cols] = _blockdiag(cci_ref[t], rep_ref, c_mask)

    y = jnp.dot(_octet_lhs(u3_ref), kt_ref[...], preferred_element_type=F32)
    y += jnp.dot(xr_ref[...].astype(BF16), cr_ref[...], preferred_element_type=F32)
    y += jnp.dot(xi_ref[...].astype(BF16), ci_ref[...], preferred_element_type=F32)
    for t in range(SSM_T):
        y3_ref[t] = y[:, t * LANES:(t + 1) * LANES]


def _ssm_out(u3, taps, xs_re, xs_im, c_re, c_im, rep_group):
    n_sub = u3.shape[1]
    u_spec = pl.BlockSpec((SSM_T, n_sub, LANES), lambda j: (0, 0, j))
    x_spec = pl.BlockSpec((None, n_sub, OCT_STATE), lambda j: (j, 0, 0))
    c_spec = pl.BlockSpec((None, SSM_T, OCT_STATE, SSM_GROUP), lambda j: (j, 0, 0, 0))
    return pl.pallas_call(
        _ssm_out_kernel,
        grid=(N_OCT,),
        in_specs=[u_spec, pl.BlockSpec((None, SSM_T, LANES, SSM_GROUP), lambda j: (j, 0, 0, 0)),
                  x_spec, x_spec, c_spec, c_spec, pl.BlockSpec((SSM_GROUP, LANES), lambda j: (0, 0))],
        out_specs=u_spec,
        out_shape=jax.ShapeDtypeStruct((SSM_T, n_sub, SSM_W), F32),
        scratch_shapes=[pltpu.VMEM((OCT_K, OCT_K), BF16), pltpu.VMEM((OCT_STATE, OCT_K), BF16),
                        pltpu.VMEM((OCT_STATE, OCT_K), BF16)],
        compiler_params=_cparams("arbitrary"),
        name="ssm_out",
    )(u3, taps, xs_re, xs_im, c_re, c_im, rep_group)


def _ssm_matrices(a_re, a_im, log_step, b_re, b_im, c_re, c_im):
    hp = lax.Precision.HIGHEST
    G, P, T, O = SSM_GROUPS, SSM_STATE, SSM_T, SSM_OCT
    step = jnp.exp(log_step.astype(F32))[:, None]
    mag = jnp.exp(a_re * step)
    ang = a_im * step
    ab_re = mag * jnp.cos(ang)
    ab_im = mag * jnp.sin(ang)
    den = a_re * a_re + a_im * a_im
    n_re = ab_re - 1.0
    f_re = (n_re * a_re + ab_im * a_im) / den
    f_im = (ab_im * a_re - n_re * a_im) / den
    bb_re = f_re[..., None] * b_re - f_im[..., None] * b_im
    bb_im = f_re[..., None] * b_im + f_im[..., None] * b_re
    pr, pi = [jnp.ones_like(ab_re)], [jnp.zeros_like(ab_re)]
    for _ in range(T):
        pr.append(pr[-1] * ab_re - pi[-1] * ab_im)
        pi.append(pr[-2] * ab_im + pi[-1] * ab_re)
    pw_re = jnp.stack(pr)
    pw_im = jnp.stack(pi)
    ab_r = pw_re[:T, :, :, None] * bb_re[None] - pw_im[:T, :, :, None] * bb_im[None]
    ab_i = pw_re[:T, :, :, None] * bb_im[None] + pw_im[:T, :, :, None] * bb_re[None]
    taps = (jnp.einsum('gcp,tgpd->tgcd', c_re, ab_r, precision=hp)
            - jnp.einsum('gcp,tgpd->tgcd', c_im, ab_i, precision=hp))

    def per_octet(x, perm, rows, cols):
        return jnp.transpose(x, perm).reshape(N_OCT, T, rows, cols).astype(BF16)

    tap_rows = per_octet(taps.reshape(T, N_OCT, O, SSM_GROUP, SSM_GROUP), (1, 0, 2, 4, 3), LANES, SSM_GROUP)

    def inject(m):
        return per_octet(m[::-1].reshape(T, N_OCT, O, P, SSM_GROUP), (1, 0, 2, 4, 3), LANES, P)

    def readout(ca):
        return per_octet(ca.reshape(T, N_OCT, O, SSM_GROUP, P), (1, 0, 2, 4, 3), OCT_STATE, SSM_GROUP)

    ca_re = c_re[None] * pw_re[1:][:, :, None, :] - c_im[None] * pw_im[1:][:, :, None, :]
    ca_im = c_re[None] * pw_im[1:][:, :, None, :] + c_im[None] * pw_re[1:][:, :, None, :]
    return dict(taps=tap_rows, m_re=inject(ab_r), m_im=inject(ab_i),
                c_re=readout(ca_re), c_im=readout(-ca_im),
                rep_group=jnp.tile(jnp.eye(SSM_GROUP, dtype=BF16), (1, O)),
                rep_state=jnp.tile(jnp.eye(P, dtype=BF16), (1, O)),
                at_re=pw_re[T], at_im=pw_im[T])


def _ssm(u3, mats, first, seq_id, h0_re, h0_im):
    sre, sim = _ssm_state(u3, mats['m_re'], mats['m_im'], mats['rep_state'])
    xsr, xsi, xer, xei = _ssm_carry(first, seq_id, sre, sim, mats['at_re'], mats['at_im'], h0_re, h0_im)
    y3 = _ssm_out(u3, mats['taps'], xsr, xsi, mats['c_re'], mats['c_im'], mats['rep_group'])
    return y3, xer, xei


def _states_at(xe, idx):
    return jnp.transpose(xe[:, idx, :], (1, 0, 2)).reshape(idx.shape[0], SSM_GROUPS, SSM_STATE)


def _gelu_tanh(x):
    return 0.5 * x * (1.0 + jnp.tanh(math.sqrt(2.0 / math.pi) * (x + 0.044715 * (x * x * x))))


def _glu_kernel(y3_ref, u_ref, d_ref, w_ref, o_ref, wbf_ref, scr_ref):
    @pl.when(pl.program_id(0) == 0)
    def _():
        wbf_ref[...] = w_ref[...].astype(BF16)

    rows = y3_ref.shape[1]
    n_col = SSM_W // LANES
    for t in range(SSM_T):
        for c in range(n_col):
            scr_ref[c, pl.ds(t, rows, stride=SSM_T), :] = y3_ref[t, :, c * LANES:(c + 1) * LANES]
    y = jnp.concatenate([scr_ref[c] for c in range(n_col)], axis=1)
    ys = _gelu_tanh(y + d_ref[...] * u_ref[...])
    t = jnp.dot(ys.astype(BF16), wbf_ref[...], preferred_element_type=F32)
    o_ref[...] = (ys * jax.nn.sigmoid(t)).astype(o_ref.dtype)


def _glu(y3, z, d_all, w_all, l):
    nt = z.shape[0]
    tm = 512
    return pl.pallas_call(
        _glu_kernel,
        grid=(nt // tm,),
        in_specs=[
            pl.BlockSpec((SSM_T, tm // SSM_T, SSM_W), lambda i: (0, i, 0)),
            pl.BlockSpec((tm, SSM_W), lambda i: (i, 3 * ATT_W // SSM_W)),
            pl.BlockSpec((None, 1, SSM_W), lambda i: (l, 0, 0)),
            pl.BlockSpec((None, SSM_W, SSM_W), lambda i: (l, 0, 0)),
        ],
        out_specs=pl.BlockSpec((tm, SSM_W), lambda i: (i, 0)),
        out_shape=jax.ShapeDtypeStruct((nt, SSM_W), BF16),
        scratch_shapes=[pltpu.VMEM((SSM_W, SSM_W), BF16), pltpu.VMEM((SSM_W // LANES, tm, LANES), F32)],
        compiler_params=_cparams("arbitrary"),
        name="glu",
    )(y3, z, d_all.reshape(DEPTH, 1, SSM_W), w_all)


def _merge_kernel(o_ref, y_ref, ga_ref, gs_ref, wa_ref, ws_ref, m_ref):
    a = jnp.dot(o_ref[...], wa_ref[...].astype(BF16), preferred_element_type=F32)
    s = jnp.dot(y_ref[...], ws_ref[...].astype(BF16), preferred_element_type=F32)
    m_ref[...] = (jax.nn.sigmoid(ga_ref[...]) * a + jax.nn.sigmoid(gs_ref[...]) * s).astype(m_ref.dtype)


def _merge(o_p, o_s, ys2, z, w_ba_all, w_bs_all, l):
    nt = ys2.shape[0]
    tm, tn = 1024, 512
    ga0 = (3 * ATT_W + SSM_W) // tn
    gs0 = ga0 + D_MODEL // tn
    p_tiles = o_p.shape[0] // tm
    s_tiles = o_s.shape[0] // tm
    assert p_tiles * tm == o_p.shape[0] and s_tiles * tm == o_s.shape[0]

    def kern(op_ref, os_ref, *rest):
        i = pl.program_id(0)

        @pl.when(i < p_tiles)
        def _():
            _merge_kernel(op_ref, *rest)

        @pl.when(i >= p_tiles)
        def _():
            _merge_kernel(os_ref, *rest)

    return pl.pallas_call(
        kern,
        grid=(nt // tm, D_MODEL // tn),
        in_specs=[
            pl.BlockSpec((tm, ATT_W), lambda i, j: (jnp.minimum(i, p_tiles - 1), 0)),
            pl.BlockSpec((tm, ATT_W), lambda i, j: (jnp.maximum(i - p_tiles, 0), 0)),
            pl.BlockSpec((tm, SSM_W), lambda i, j: (i, 0)),
            pl.BlockSpec((tm, tn), lambda i, j: (i, ga0 + j)),
            pl.BlockSpec((tm, tn), lambda i, j: (i, gs0 + j)),
            pl.BlockSpec((None, ATT_W, tn), lambda i, j: (l, 0, j)),
            pl.BlockSpec((None, SSM_W, tn), lambda i, j: (l, 0, j)),
        ],
        out_specs=pl.BlockSpec((tm, tn), lambda i, j: (i, j)),
        out_shape=jax.ShapeDtypeStruct((nt, D_MODEL), BF16),
        compiler_params=_cparams("arbitrary", "arbitrary"),
        name="merge",
    )(o_p, o_s, ys2, z, z, w_ba_all, w_bs_all)


def _proj_residual_kernel(a_ref, w_ref, x_ref, g_ref, o_ref):
    r = jnp.dot(a_ref[...], w_ref[...].astype(BF16), preferred_element_type=F32)
    tm, tn = r.shape
    r3 = r.reshape(tm // CHUNK, CHUNK, tn) * g_ref[...]
    o_ref[...] = x_ref[...] + r3.reshape(tm, tn)


def _proj_residual(a, w_all, x, tab, l, which_gate, x_tail=None):
    nt, k = a.shape
    tm, tn = 1024, 512
    cpt = tm // CHUNK
    kern = _proj_residual_kernel
    x_specs = [pl.BlockSpec((tm, tn), lambda i, j: (i, j))]
    xs = [x]
    if x_tail is not None:
        assert x.shape[0] % tm == 0 and x_tail.shape[0] % tm == 0 and x.shape[0] + x_tail.shape[0] == nt
        head_tiles = x.shape[0] // tm
        kern = _split_rows(_proj_residual_kernel, 2, head_tiles)
        x_specs = _split_specs((tm, tn), head_tiles, True)
        xs = [x, x_tail]
    return pl.pallas_call(
        kern,
        grid=(nt // tm, D_MODEL // tn),
        in_specs=[
            pl.BlockSpec((tm, k), lambda i, j: (i, 0)),
            pl.BlockSpec((None, k, tn), lambda i, j: (l, 0, j)),
            *x_specs,
            _tab_spec(cpt, l, which_gate, width=tn, col=True),
        ],
        out_specs=pl.BlockSpec((tm, tn), lambda i, j: (i, j)),
        out_shape=jax.ShapeDtypeStruct((nt, D_MODEL), F32),
        compiler_params=_cparams("arbitrary", "arbitrary"),
        name="proj_residual",
    )(a, w_all, *xs, tab)


def _swiglu_step(h, wg_ref, wu_ref, wd_ref):
    hg = jnp.dot(h, wg_ref[...].astype(BF16), preferred_element_type=F32)
    hu = jnp.dot(h, wu_ref[...].astype(BF16), preferred_element_type=F32)
    a = (hg * jax.nn.sigmoid(hg)) * hu
    return jnp.dot(a.astype(BF16), wd_ref[...].astype(BF16), preferred_element_type=F32)


def _ffn_kernel(h_ref, wg_ref, wu_ref, wd_ref, o_ref):
    @pl.when(pl.program_id(1) == 0)
    def _():
        o_ref[...] = jnp.zeros_like(o_ref)

    o_ref[...] += _swiglu_step(h_ref[...], wg_ref, wu_ref, wd_ref)


def _ffn(h, w_gate, w_up, w_down, j):
    nt = h.shape[0]
    d_ff = w_gate.shape[-1]
    tm, tf = FFN_TM, FFN_TF
    return pl.pallas_call(
        _ffn_kernel,
        grid=(nt // tm, d_ff // tf),
        in_specs=[
            pl.BlockSpec((tm, D_MODEL), lambda i, f: (i, 0)),
            pl.BlockSpec((None, D_MODEL, tf), lambda i, f: (j, 0, f)),
            pl.BlockSpec((None, D_MODEL, tf), lambda i, f: (j, 0, f)),
            pl.BlockSpec((None, tf, D_MODEL), lambda i, f: (j, f, 0)),
        ],
        out_specs=pl.BlockSpec((tm, D_MODEL), lambda i, f: (i, 0)),
        out_shape=jax.ShapeDtypeStruct((nt, D_MODEL), F32),
        compiler_params=_cparams("arbitrary", "arbitrary"),
        name="ffn",
    )(h, w_gate, w_up, w_down)


def _moe_plan(route, nt):
    tm = MOE_TM
    n_assign = 2 * nt
    n_tiles = (n_assign + N_EXPERTS * (tm - 1)) // tm
    e_flat = jnp.concatenate([route[:, 0], route[:, 1]]).astype(jnp.int32)
    onehot = (e_flat[:, None] == jnp.arange(N_EXPERTS, dtype=jnp.int32)[None, :]).astype(jnp.int32)
    csum = jnp.cumsum(onehot, axis=0)
    counts = csum[-1]
    rank = jnp.sum((csum - onehot) * onehot, axis=1)
    tiles_per = (counts + tm - 1) // tm
    tile_end = jnp.cumsum(tiles_per)
    offsets = (tile_end - tiles_per) * tm
    dest = jnp.sum(onehot * offsets[None, :], axis=1) + rank
    tidx = jnp.arange(n_tiles, dtype=jnp.int32)
    tile_expert = jnp.minimum(jnp.sum((tidx[:, None] >= tile_end[None, :]).astype(jnp.int32), axis=1), N_EXPERTS - 1)
    tile_valid = (tidx < tile_end[-1]).astype(jnp.int32)
    tok = jnp.concatenate([jnp.arange(nt, dtype=jnp.int32)] * 2)
    src_token = jnp.zeros((n_tiles * tm,), jnp.int32).at[dest].set(tok)
    return dict(dest1=dest[:nt], dest2=dest[nt:], src_token=src_token,
                tile_expert=tile_expert.astype(jnp.int32), tile_valid=tile_valid)


def _row_copy(src_ref, row, dst_ref, r, sem):
    return pltpu.make_async_copy(src_ref.at[pl.ds(row, 1)], dst_ref.at[pl.ds(r, 1)], sem)


def _gather_start(idx_ref, base, src_ref, dst_ref, sem):
    rows = dst_ref.shape[0]

    def group(gi, c):
        for u in range(GATHER_UNROLL):
            r = gi * GATHER_UNROLL + u
            _row_copy(src_ref, idx_ref[base + r], dst_ref, r, sem).start(priority=u % 2)
        return c

    lax.fori_loop(0, rows // GATHER_UNROLL, group, 0)


def _gather_wait(src_ref, dst_ref, sem):
    pltpu.make_async_copy(src_ref.at[pl.ds(0, dst_ref.shape[0])], dst_ref, sem).wait()


def _gather_rows_kernel(idx_ref, src_ref, o_ref, buf_ref, sem):
    i = pl.program_id(0)
    n = pl.num_programs(0)
    slot = i % 2

    @pl.when(i == 0)
    def _():
        _gather_start(idx_ref, 0, src_ref, buf_ref.at[0], sem.at[0])

    @pl.when(i + 1 < n)
    def _():
        _gather_start(idx_ref, (i + 1) * buf_ref.shape[1], src_ref, buf_ref.at[1 - slot], sem.at[1 - slot])

    _gather_wait(src_ref, buf_ref.at[slot], sem.at[slot])
    o_ref[...] = buf_ref[slot].astype(o_ref.dtype)


def _gather_rows(src, idx, out_dtype):
    n_out = idx.shape[0]
    width = src.shape[1]
    rows = DISPATCH_ROWS
    assert n_out % rows == 0 and rows % GATHER_UNROLL == 0
    return pl.pallas_call(
        _gather_rows_kernel,
        grid_spec=pltpu.PrefetchScalarGridSpec(
            num_scalar_prefetch=1,
            grid=(n_out // rows,),
            in_specs=[pl.BlockSpec(memory_space=pl.ANY)],
            out_specs=pl.BlockSpec((rows, width), lambda i, *_: (i, 0)),
            scratch_shapes=[pltpu.VMEM((2, rows, width), src.dtype), pltpu.SemaphoreType.DMA((2,))],
        ),
        out_shape=jax.ShapeDtypeStruct((n_out, width), out_dtype),
        compiler_params=_cparams("arbitrary"),
        name="moe_gather",
    )(idx, src)


def _moe_ffn_kernel(te_ref, tv_ref, h_ref, wg_ref, wu_ref, wd_ref, o_ref):
    t = pl.program_id(0)

    @pl.when(pl.program_id(1) == 0)
    def _():
        o_ref[...] = jnp.zeros_like(o_ref)

    @pl.when(tv_ref[t] == 1)
    def _():
        o_ref[...] += _swiglu_step(h_ref[...], wg_ref, wu_ref, wd_ref)


def _moe_ffn(h_sorted, tile_expert, tile_valid, w_gate, w_up, w_down, j):
    d_ff = w_gate.shape[-1]
    tm, tf = MOE_TM, FFN_TF
    n_tiles = h_sorted.shape[0] // tm
    nf = d_ff // tf

    def f_idx(t, f, tv):
        return jnp.where(tv[t] == 1, f, nf - 1)

    return pl.pallas_call(
        _moe_ffn_kernel,
        grid_spec=pltpu.PrefetchScalarGridSpec(
            num_scalar_prefetch=2,
            grid=(n_tiles, nf),
            in_specs=[
                pl.BlockSpec((tm, D_MODEL), lambda t, f, te, tv: (t, 0)),
                pl.BlockSpec((None, None, D_MODEL, tf), lambda t, f, te, tv: (j, te[t], 0, f_idx(t, f, tv))),
                pl.BlockSpec((None, None, D_MODEL, tf), lambda t, f, te, tv: (j, te[t], 0, f_idx(t, f, tv))),
                pl.BlockSpec((None, None, tf, D_MODEL), lambda t, f, te, tv: (j, te[t], f_idx(t, f, tv), 0)),
            ],
            out_specs=pl.BlockSpec((tm, D_MODEL), lambda t, f, te, tv: (t, 0)),
        ),
        out_shape=jax.ShapeDtypeStruct((n_tiles * tm, D_MODEL), F32),
        compiler_params=_cparams("arbitrary", "arbitrary"),
        name="moe_ffn",
    )(tile_expert, tile_valid, h_sorted, w_gate, w_up, w_down)


def _moe_combine_kernel(d1_ref, d2_ref, y_ref, x_ref, route_ref, g_ref, gn_ref, *rest, final_norm, split_tiles):
    n_out = 1 if split_tiles is None else 2
    o_refs, (a_ref, b_ref, sem_a, sem_b) = rest[:n_out], rest[n_out:]
    i = pl.program_id(0)
    n = pl.num_programs(0)
    slot = i % 2

    def start(step, s):
        _gather_start(d1_ref, step * GATHER_ROWS, y_ref, a_ref.at[s], sem_a.at[s])
        _gather_start(d2_ref, step * GATHER_ROWS, y_ref, b_ref.at[s], sem_b.at[s])

    @pl.when(i == 0)
    def _():
        start(0, 0)

    @pl.when(i + 1 < n)
    def _():
        start(i + 1, 1 - slot)

    _gather_wait(y_ref, a_ref.at[slot], sem_a.at[slot])
    _gather_wait(y_ref, b_ref.at[slot], sem_b.at[slot])
    route = route_ref[...]
    r = route[:, 2:3] * a_ref[slot] + route[:, 3:4] * b_ref[slot]
    tm, n_col = r.shape
    xn = x_ref[...] + (r.reshape(tm // CHUNK, CHUNK, n_col) * g_ref[...]).reshape(tm, n_col)
    if final_norm:
        xn = xn * lax.rsqrt(jnp.mean(xn * xn, axis=-1, keepdims=True) + EPS) * gn_ref[...]
    if split_tiles is None:
        o_refs[0][...] = xn
    else:
        @pl.when(i < split_tiles)
        def _():
            o_refs[0][...] = xn

        @pl.when(i >= split_tiles)
        def _():
            o_refs[1][...] = xn


def _moe_combine(y_sorted, dest1, dest2, x, route, tab, l, which_gate, g_final=None, split_rows=None):
    final_norm = g_final is not None
    gn = (g_final if final_norm else jnp.ones((D_MODEL,), F32)).reshape(1, D_MODEL)
    nt = x.shape[0]
    tm = GATHER_ROWS
    blk = pl.BlockSpec((tm, D_MODEL), lambda i, *_: (i, 0))
    buf = pltpu.VMEM((2, tm, D_MODEL), F32)
    if split_rows is None:
        split_tiles = None
        out_specs = blk
        out_shape = jax.ShapeDtypeStruct((nt, D_MODEL), F32)
    else:
        split_tiles = split_rows // tm
        assert split_tiles * tm == split_rows and 0 < split_rows < nt
        out_specs = [pl.BlockSpec((tm, D_MODEL), lambda i, *_: (jnp.minimum(i, split_tiles - 1), 0)),
                     pl.BlockSpec((tm, D_MODEL), lambda i, *_: (jnp.maximum(i - split_tiles, 0), 0))]
        out_shape = [jax.ShapeDtypeStruct((split_rows, D_MODEL), F32),
                     jax.ShapeDtypeStruct((nt - split_rows, D_MODEL), F32)]
    return pl.pallas_call(
        functools.partial(_moe_combine_kernel, final_norm=final_norm, split_tiles=split_tiles),
        grid_spec=pltpu.PrefetchScalarGridSpec(
            num_scalar_prefetch=2,
            grid=(nt // tm,),
            in_specs=[pl.BlockSpec(memory_space=pl.ANY), blk, pl.BlockSpec((tm, LANES), lambda i, *_: (i, 0)),
                      _tab_spec(tm // CHUNK, l, which_gate), pl.BlockSpec((1, D_MODEL), lambda i, *_: (0, 0))],
            out_specs=out_specs,
            scratch_shapes=[buf, buf, pltpu.SemaphoreType.DMA((2,)), pltpu.SemaphoreType.DMA((2,))],
        ),
        out_shape=out_shape,
        compiler_params=_cparams("arbitrary"),
        name="moe_combine",
    )(dest1, dest2, y_sorted, x, route, tab, gn)


def _residual_kernel(x_ref, r_ref, g_ref, o_ref):
    r = r_ref[...]
    tm, n = r.shape
    o_ref[...] = x_ref[...] + (r.reshape(tm // CHUNK, CHUNK, n) * g_ref[...]).reshape(tm, n)


def _residual(x, r, tab, l, which_gate):
    nt = x.shape[0]
    tm = 512
    blk = pl.BlockSpec((tm, D_MODEL), lambda i: (i, 0))
    return pl.pallas_call(
        _residual_kernel, grid=(nt // tm,),
        in_specs=[blk, blk, _tab_spec(tm // CHUNK, l, which_gate)],
        out_specs=blk, out_shape=jax.ShapeDtypeStruct((nt, D_MODEL), F32),
        compiler_params=_cparams("arbitrary"), name="residual",
    )(x, r, tab)


def _residual_norm_kernel(x_ref, r_ref, gate_ref, g_ref, sh_ref, sc_ref, xo_ref, h_ref):
    r = r_ref[...]
    tm, n = r.shape
    xn = x_ref[...] + (r.reshape(tm // CHUNK, CHUNK, n) * gate_ref[...]).reshape(tm, n)
    xo_ref[...] = xn
    h_ref[...] = _modulate(xn, g_ref, sh_ref, sc_ref).astype(BF16)


def _residual_norm(x, r, tab, l, which_gate, g_all, l_next, which_shift, which_scale):
    nt = x.shape[0]
    tm = 512
    cpt = tm // CHUNK
    blk = pl.BlockSpec((tm, D_MODEL), lambda i: (i, 0))
    return pl.pallas_call(
        _residual_norm_kernel, grid=(nt // tm,),
        in_specs=[blk, blk, _tab_spec(cpt, l, which_gate),
                  pl.BlockSpec((None, 1, D_MODEL), lambda i: (l_next, 0, 0)),
                  _tab_spec(cpt, l_next, which_shift), _tab_spec(cpt, l_next, which_scale)],
        out_specs=[blk, blk],
        out_shape=[jax.ShapeDtypeStruct((nt, D_MODEL), F32), jax.ShapeDtypeStruct((nt, D_MODEL), BF16)],
        compiler_params=_cparams("arbitrary"), name="residual_norm",
    )(x, r, tab, g_all.reshape(DEPTH, 1, D_MODEL), tab, tab)


def _kv_rows_kernel(z_ref, o_ref):
    for h in range(N_HEADS):
        o_ref[:, h, :] = z_ref[:, h * HEAD_DIM:(h + 1) * HEAD_DIM]


def _kv_rows(z, n_pb, seq, n_sb):
    rows = PAST_WINDOW
    per_seq = seq // rows
    s_blocks = n_sb * CHUNK // rows
    assert per_seq * rows == seq and s_blocks * rows == n_sb * CHUNK

    def row_block(r):
        return jnp.where(r < n_pb, (r + 1) * per_seq - 1, n_pb * per_seq + (r - n_pb))

    n_blocks = n_pb + s_blocks
    return pl.pallas_call(
        _kv_rows_kernel, grid=(2, n_blocks),
        in_specs=[pl.BlockSpec((rows, ATT_W), lambda w, r: (row_block(r), 1 + w))],
        out_specs=pl.BlockSpec((None, rows, N_HEADS, HEAD_DIM), lambda w, r: (w, r, 0, 0)),
        out_shape=jax.ShapeDtypeStruct((2, n_blocks * rows, N_HEADS, HEAD_DIM), F32),
        compiler_params=_cparams("arbitrary", "arbitrary"), name="kv_rows",
    )(z)


def _final_norm_kernel(x_ref, g_ref, o_ref):
    x = x_ref[...]
    o_ref[...] = x * lax.rsqrt(jnp.mean(x * x, axis=-1, keepdims=True) + EPS) * g_ref[...]


def _final_norm(x, g):
    nt = x.shape[0]
    tm = 512
    blk = pl.BlockSpec((tm, D_MODEL), lambda i: (i, 0))
    return pl.pallas_call(
        _final_norm_kernel, grid=(nt // tm,),
        in_specs=[blk, pl.BlockSpec((1, D_MODEL), lambda i: (0, 0))],
        out_specs=blk, out_shape=jax.ShapeDtypeStruct((nt, D_MODEL), F32),
        compiler_params=_cparams("arbitrary"), name="final_norm",
    )(x, g.reshape(1, D_MODEL))


def kernel(x_prompt, x_sample, cache_k, cache_v, state_ssm_re, state_ssm_im, c_prompt, c_sample, w_ada, b_ada, g_norm_mix, g_norm_ffn, g_norm_final, w_in, rel_bias, ssm_a_re, ssm_a_im, ssm_log_step, ssm_b_re, ssm_b_im, ssm_c_re, ssm_c_im, ssm_d, w_ssm_glu, w_branch_attn, w_branch_ssm, w_out, w_ffn_gate, w_ffn_up, w_ffn_down, w_router, b_router, w_exp_gate, w_exp_up, w_exp_down):
    n_pb, seq = x_prompt.shape[0], x_prompt.shape[1]
    n_sb, dec = x_sample.shape[0], x_sample.shape[1]
    assert dec == CHUNK and cache_k.shape[2] == PAST_WINDOW and seq % ATT_ROWS == 0
    np_tok = n_pb * seq
    nt = np_tok + n_sb * dec
    n_batch = n_pb + n_sb

    x_head = x_prompt.reshape(np_tok, D_MODEL)
    x_tail = x_sample.reshape(n_sb * dec, D_MODEL)

    c_all = jnp.concatenate([c_prompt, c_sample], axis=0)
    c_pad = jnp.pad(c_all, ((0, -n_batch % 8), (0, 0)))
    ada = _ada(c_pad, w_ada, b_ada)
    ada_p = jnp.broadcast_to(ada[:, :n_pb, None, :], (DEPTH, n_pb, seq // CHUNK, 6 * D_MODEL))
    tab = jnp.concatenate([ada_p.reshape(DEPTH, np_tok // CHUNK, 6 * D_MODEL), ada[:, n_pb:n_batch]], axis=1)
    tab = tab.reshape(DEPTH, nt // CHUNK, 6, 1, D_MODEL)

    sub_p, sub_s = seq // SSM_T, dec // SSM_T
    seq_id = jnp.concatenate([jnp.repeat(jnp.arange(n_pb), sub_p), n_pb + jnp.repeat(jnp.arange(n_sb), sub_s)])
    seq_id = seq_id.astype(jnp.int32)
    first = jnp.concatenate([jnp.arange(n_pb * sub_p) % sub_p == 0, jnp.arange(n_sb * sub_s) % sub_s == 0])
    first = first.astype(jnp.int32)
    last_p = (jnp.arange(n_pb) + 1) * sub_p - 1
    last_s = n_pb * sub_p + (jnp.arange(n_sb) + 1) * sub_s - 1
    zeros_h0 = jnp.zeros((n_pb, SSM_GROUPS, SSM_STATE), F32)

    outs = {k: [] for k in ('kp', 'vp', 'srp', 'sip', 'ks', 'vs', 'srs', 'sis')}
    keep = min(PAST_WINDOW, seq)
    h = _norm_mod(x_head, g_norm_mix, tab, 0, SHIFT1, SCALE1, x_tail=x_tail)
    for l in range(DEPTH):
        last = l == DEPTH - 1
        z, u3 = _in_proj(h, w_in, l)

        b64 = _attention_bias(rel_bias[l])
        o_p = _attn_prompt(z, b64, n_pb, seq)
        o_s = _attn_sample(z, cache_k, cache_v, l, b64, np_tok, n_sb)

        mats = _ssm_matrices(ssm_a_re[l], ssm_a_im[l], ssm_log_step[l], ssm_b_re[l], ssm_b_im[l],
                             ssm_c_re[l], ssm_c_im[l])
        h0_re = jnp.concatenate([zeros_h0, state_ssm_re[l].astype(F32)], axis=0)
        h0_im = jnp.concatenate([zeros_h0, state_ssm_im[l].astype(F32)], axis=0)
        y3, xe_re, xe_im = _ssm(u3, mats, first, seq_id, h0_re, h0_im)

        ys2 = _glu(y3, z, ssm_d, w_ssm_glu, l)
        merged = _merge(o_p, o_s, ys2, z, w_branch_attn, w_branch_ssm, l)
        if l == 0:
            x = _proj_residual(merged, w_out, x_head, tab, l, GATE1, x_tail=x_tail)
        else:
            x = _proj_residual(merged, w_out, x, tab, l, GATE1)

        j = l // 2
        if l % 2 == 0:
            h2 = _norm_mod(x, g_norm_ffn, tab, l, SHIFT2, SCALE2)
            r = _ffn(h2, w_ffn_gate, w_ffn_up, w_ffn_down, j)
            if last:
                yn = _final_norm(_residual(x, r, tab, l, GATE2), g_norm_final)
                yn_p, yn_s = yn[:np_tok], yn[np_tok:]
            else:
                x, h = _residual_norm(x, r, tab, l, GATE2, g_norm_mix, l + 1, SHIFT1, SCALE1)
        else:
            h2, route = _norm_mod(x, g_norm_ffn, tab, l, SHIFT2, SCALE2, router=(w_router[j], b_router[j]))
            plan = _moe_plan(route, nt)
            h_sorted = _gather_rows(h2, plan['src_token'], BF16)
            y_sorted = _moe_ffn(h_sorted, plan['tile_expert'], plan['tile_valid'], w_exp_gate, w_exp_up, w_exp_down, j)
            if last:
                yn_p, yn_s = _moe_combine(y_sorted, plan['dest1'], plan['dest2'], x, route, tab, l, GATE2,
                                          g_final=g_norm_final, split_rows=np_tok)
            else:
                x = _moe_combine(y_sorted, plan['dest1'], plan['dest2'], x, route, tab, l, GATE2)
                h = _norm_mod(x, g_norm_mix, tab, l + 1, SHIFT1, SCALE1)

        kv = _kv_rows(z, n_pb, seq, n_sb)
        outs['kp'].append(kv[0, :n_pb * keep].reshape(n_pb, keep, N_HEADS, HEAD_DIM))
        outs['vp'].append(kv[1, :n_pb * keep].reshape(n_pb, keep, N_HEADS, HEAD_DIM))
        outs['ks'].append(kv[0, n_pb * keep:].reshape(n_sb, dec, N_HEADS, HEAD_DIM))
        outs['vs'].append(kv[1, n_pb * keep:].reshape(n_sb, dec, N_HEADS, HEAD_DIM))
        outs['srp'].append(_states_at(xe_re, last_p))
        outs['sip'].append(_states_at(xe_im, last_p))
        outs['srs'].append(_states_at(xe_re, last_s))
        outs['sis'].append(_states_at(xe_im, last_s))

    y_prompt = yn_p.reshape(n_pb, seq, D_MODEL)
    y_sample = yn_s.reshape(n_sb, dec, D_MODEL)
    st = lambda name: jnp.stack(outs[name])
    return (y_prompt, y_sample, st('kp'), st('vp'), st('srp'), st('sip'),
            st('ks'), st('vs'), st('srs'), st('sis'))
```

```python
import functools
import math

import jax
import jax.numpy as jnp
from jax import lax
from jax.experimental import pallas as pl
from jax.experimental.pallas import tpu as pltpu

F32 = jnp.float32
BF16 = jnp.bfloat16

D_MODEL = 2048
DEPTH = 2
CHUNK = 64
PAST_CHUNKS = 8
PAST_WINDOW = PAST_CHUNKS * CHUNK
BAND = PAST_WINDOW + CHUNK
N_HEADS = 8
HEAD_DIM = 128
ATT_W = N_HEADS * HEAD_DIM
ATT_SCALE = HEAD_DIM ** -0.5
REL_CLIP = 256
SSM_W = 1024
SSM_GROUP = 16
SSM_GROUPS = SSM_W // SSM_GROUP
SSM_STATE = 64
N_EXPERTS = 8
IN_COLS = 3 * ATT_W + SSM_W + 2 * D_MODEL
EPS = 1e-6
NEG_INF = -1e30

LANES = 128
VMEM_LIMIT_BYTES = 56 * 1024 * 1024

SSM_T = 16
SSM_OCT = LANES // SSM_GROUP
N_OCT = SSM_GROUPS // SSM_OCT
OCT_K = SSM_T * LANES
OCT_STATE = SSM_OCT * SSM_STATE
GROUP_SHIFT = SSM_GROUP.bit_length() - 1
STATE_SHIFT = SSM_STATE.bit_length() - 1
assert (1 << GROUP_SHIFT, 1 << STATE_SHIFT) == (SSM_GROUP, SSM_STATE)
ATT_GROUP = 4
ATT_ROWS = ATT_GROUP * CHUNK
ATT_KEYS = 3 * ATT_ROWS

IN_TM, IN_TN = 1536, 512
FFN_TM = 768
MOE_TM = 768
FFN_TF = 512
GATHER_ROWS = 512
GATHER_UNROLL = 8

SHIFT1, SCALE1, GATE1, SHIFT2, SCALE2, GATE2 = range(6)


def _cparams(*sem):
    return pltpu.CompilerParams(dimension_semantics=sem, vmem_limit_bytes=VMEM_LIMIT_BYTES)


def _tab_spec(cpt, l, which, width=D_MODEL, col=None):
    if col is None:
        return pl.BlockSpec((None, cpt, None, 1, width), lambda i, *_: (l, i, which, 0, 0))
    return pl.BlockSpec((None, cpt, None, 1, width), lambda i, j, *_: (l, i, which, 0, j))


def _ada_kernel(c_ref, w_ref, b_ref, o_ref):
    c = c_ref[...]
    a = (c * jax.nn.sigmoid(c)).astype(BF16)
    o_ref[...] = jnp.dot(a, w_ref[...].astype(BF16), preferred_element_type=F32) + b_ref[...]


def _ada(c_pad, w_ada, b_ada):
    rows = c_pad.shape[0]
    n = w_ada.shape[-1]
    tn = 1024
    return pl.pallas_call(
        _ada_kernel,
        grid=(DEPTH, n // tn),
        in_specs=[
            pl.BlockSpec((rows, D_MODEL), lambda l, j: (0, 0)),
            pl.BlockSpec((None, D_MODEL, tn), lambda l, j: (l, 0, j)),
            pl.BlockSpec((None, 1, tn), lambda l, j: (l, 0, j)),
        ],
        out_specs=pl.BlockSpec((None, rows, tn), lambda l, j: (l, 0, j)),
        out_shape=jax.ShapeDtypeStruct((DEPTH, rows, n), F32),
        compiler_params=_cparams("arbitrary", "arbitrary"),
        name="ada",
    )(c_pad, w_ada, b_ada.reshape(DEPTH, 1, n))


def _modulated(x_ref, g_ref, sh_ref, sc_ref):
    return _modulate(x_ref[...], g_ref, sh_ref, sc_ref)


def _modulate(x, g_ref, sh_ref, sc_ref):
    tm = x.shape[0]
    y = x * lax.rsqrt(jnp.mean(x * x, axis=-1, keepdims=True) + EPS) * g_ref[...]
    y3 = y.reshape(tm // CHUNK, CHUNK, D_MODEL)
    h = y3 * (1.0 + sc_ref[...]) + sh_ref[...]
    return h.reshape(tm, D_MODEL)


def _norm_mod_kernel(x_ref, g_ref, sh_ref, sc_ref, h_ref):
    h_ref[...] = _modulated(x_ref, g_ref, sh_ref, sc_ref).astype(BF16)


def _split_bf16(a):
    hi = a.astype(BF16)
    lo = (a - hi.astype(F32)).astype(BF16)
    return hi, lo


def _norm_mod_router_kernel(x_ref, g_ref, sh_ref, sc_ref, wr_ref, br_ref, h_ref, route_ref):
    h = _modulated(x_ref, g_ref, sh_ref, sc_ref)
    h_ref[...] = h
    h_hi, h_lo = _split_bf16(h)
    w_hi, w_lo = _split_bf16(wr_ref[...])
    dot = functools.partial(jnp.dot, preferred_element_type=F32)
    logits = (dot(h_hi, w_hi) + (dot(h_hi, w_lo) + dot(h_lo, w_hi)) + dot(h_lo, w_lo)) + br_ref[...]
    lane = lax.broadcasted_iota(jnp.int32, logits.shape, 1).astype(F32)
    lg = jnp.where(lane < N_EXPERTS, logits, -jnp.inf)
    m1 = jnp.max(lg, axis=-1, keepdims=True)
    i1 = jnp.min(jnp.where(lg == m1, lane, float(LANES)), axis=-1, keepdims=True)
    lg2 = jnp.where(lane == i1, -jnp.inf, lg)
    m2 = jnp.max(lg2, axis=-1, keepdims=True)
    i2 = jnp.min(jnp.where(lg2 == m2, lane, float(LANES)), axis=-1, keepdims=True)
    e2 = jnp.exp(m2 - m1)
    den = 1.0 + e2
    route = jnp.where(lane == 0.0, i1, jnp.where(lane == 1.0, i2, jnp.where(lane == 2.0, 1.0 / den, e2 / den)))
    route_ref[...] = jnp.where(lane < 4.0, route, 0.0)


def _split_rows(kernel_fn, pos, head_tiles):
    def kern(*refs):
        before, after = refs[:pos], refs[pos + 2:]
        i = pl.program_id(0)

        @pl.when(i < head_tiles)
        def _():
            kernel_fn(*before, refs[pos], *after)

        @pl.when(i >= head_tiles)
        def _():
            kernel_fn(*before, refs[pos + 1], *after)

    return kern


def _split_specs(block, head_tiles, with_col):
    if with_col:
        return [pl.BlockSpec(block, lambda i, j: (jnp.minimum(i, head_tiles - 1), j)),
                pl.BlockSpec(block, lambda i, j: (jnp.maximum(i - head_tiles, 0), j))]
    return [pl.BlockSpec(block, lambda i: (jnp.minimum(i, head_tiles - 1), 0)),
            pl.BlockSpec(block, lambda i: (jnp.maximum(i - head_tiles, 0), 0))]


def _norm_mod(x, g_all, tab, l, which_shift, which_scale, router=None, x_tail=None):
    tm = 512
    cpt = tm // CHUNK
    in_specs = [
        pl.BlockSpec((tm, D_MODEL), lambda i: (i, 0)),
        pl.BlockSpec((None, 1, D_MODEL), lambda i: (l, 0, 0)),
        _tab_spec(cpt, l, which_shift),
        _tab_spec(cpt, l, which_scale),
    ]
    h_spec = pl.BlockSpec((tm, D_MODEL), lambda i: (i, 0))
    args = [x, g_all.reshape(DEPTH, 1, D_MODEL), tab, tab]
    if x_tail is not None:
        assert router is None and x.shape[0] % tm == 0 and x_tail.shape[0] % tm == 0
        head_tiles = x.shape[0] // tm
        nt = x.shape[0] + x_tail.shape[0]
        return pl.pallas_call(
            _split_rows(_norm_mod_kernel, 0, head_tiles), grid=(nt // tm,),
            in_specs=_split_specs((tm, D_MODEL), head_tiles, False) + in_specs[1:], out_specs=h_spec,
            out_shape=jax.ShapeDtypeStruct((nt, D_MODEL), BF16),
            compiler_params=_cparams("arbitrary"), name="norm_mod",
        )(x, x_tail, *args[1:])
    nt = x.shape[0]
    if router is None:
        return pl.pallas_call(
            _norm_mod_kernel, grid=(nt // tm,), in_specs=in_specs, out_specs=h_spec,
            out_shape=jax.ShapeDtypeStruct((nt, D_MODEL), BF16),
            compiler_params=_cparams("arbitrary"), name="norm_mod",
        )(*args)
    w_r, b_r = router
    w_pad = jnp.zeros((D_MODEL, LANES), F32).at[:, :N_EXPERTS].set(w_r)
    b_pad = jnp.zeros((1, LANES), F32).at[0, :N_EXPERTS].set(b_r)
    in_specs += [
        pl.BlockSpec((D_MODEL, LANES), lambda i: (0, 0)),
        pl.BlockSpec((1, LANES), lambda i: (0, 0)),
    ]
    return pl.pallas_call(
        _norm_mod_router_kernel, grid=(nt // tm,), in_specs=in_specs,
        out_specs=[h_spec, pl.BlockSpec((tm, LANES), lambda i: (i, 0))],
        out_shape=[jax.ShapeDtypeStruct((nt, D_MODEL), F32), jax.ShapeDtypeStruct((nt, LANES), F32)],
        compiler_params=_cparams("arbitrary"), name="norm_mod_router",
    )(*args, w_pad, b_pad)


def _in_proj_kernel(x_ref, w_ref, z_ref, u3_ref, scr_ref):
    j = pl.program_id(1)
    r = jnp.dot(x_ref[...], w_ref[...].astype(BF16), preferred_element_type=F32)
    z_ref[...] = r
    tm, tn = r.shape
    u_first = 3 * ATT_W // tn
    for part in range(SSM_W // tn):
        @pl.when(j == u_first + part)
        def _():
            for c in range(tn // LANES):
                scr_ref[c] = r[:, c * LANES:(c + 1) * LANES]
            for s in range(SSM_T):
                for c in range(tn // LANES):
                    lo = part * tn + c * LANES
                    rows = scr_ref[c, pl.ds(s, tm // SSM_T, stride=SSM_T), :]
                    u3_ref[s, :, lo:lo + LANES] = rows.astype(BF16)


def _in_proj(h, w_all, l):
    nt, k = h.shape
    n = w_all.shape[-1]
    tm, tn = IN_TM, IN_TN
    return pl.pallas_call(
        _in_proj_kernel,
        grid=(nt // tm, n // tn),
        in_specs=[
            pl.BlockSpec((tm, k), lambda i, j: (i, 0)),
            pl.BlockSpec((None, k, tn), lambda i, j: (l, 0, j)),
        ],
        out_specs=[
            pl.BlockSpec((tm, tn), lambda i, j: (i, j)),
            pl.BlockSpec((SSM_T, tm // SSM_T, SSM_W), lambda i, j: (0, i, 0)),
        ],
        out_shape=[
            jax.ShapeDtypeStruct((nt, n), F32),
            jax.ShapeDtypeStruct((SSM_T, nt // SSM_T, SSM_W), BF16),
        ],
        scratch_shapes=[pltpu.VMEM((tn // LANES, tm, LANES), F32)],
        compiler_params=_cparams("arbitrary", "arbitrary"),
        name="in_proj",
    )(h, w_all)


def _softmax_rows(s):
    m = jnp.max(s, axis=-1, keepdims=True)
    e = jnp.exp(s - m)
    return e / jnp.sum(e, axis=-1, keepdims=True)


def _attn_prompt_kernel(q_ref, k0_ref, k1_ref, k2_ref, v0_ref, v1_ref, v2_ref, b64_ref, o_ref, bias_ref):
    g = pl.program_id(1)

    @pl.when((pl.program_id(0) == 0) & (g == 0))
    def _():
        bias_ref[...] = jnp.full(bias_ref.shape, NEG_INF, F32)
        for c in range(ATT_GROUP):
            bias_ref[:, c * CHUNK:(c + 1) * CHUNK, c * CHUNK:c * CHUNK + BAND] = b64_ref[...]

    col = lax.broadcasted_iota(jnp.int32, (ATT_ROWS, ATT_KEYS), 1)
    visible = (g * ATT_ROWS - 2 * ATT_ROWS + col) >= 0
    for h in range(N_HEADS):
        sl = slice(h * HEAD_DIM, (h + 1) * HEAD_DIM)
        qh = q_ref[:, sl].astype(BF16)
        kh = jnp.concatenate([k0_ref[:, sl], k1_ref[:, sl], k2_ref[:, sl]], axis=0).astype(BF16)
        vh = jnp.concatenate([v0_ref[:, sl], v1_ref[:, sl], v2_ref[:, sl]], axis=0).astype(BF16)
        s = lax.dot_general(qh, kh, (((1,), (1,)), ((), ())), preferred_element_type=F32)
        s = s * ATT_SCALE + bias_ref[h]
        s = jnp.where(visible, s, NEG_INF)
        p = _softmax_rows(s).astype(BF16)
        o_ref[:, sl] = jnp.dot(p, vh, preferred_element_type=F32).astype(o_ref.dtype)


def _attn_prompt(z, b64, n_batch, seq):
    groups = seq // ATT_ROWS
    blk = (ATT_ROWS, ATT_W)

    def kv_spec(col, back):
        return pl.BlockSpec(blk, lambda b, g: (b * groups + jnp.maximum(g - back, 0), col))

    return pl.pallas_call(
        _attn_prompt_kernel,
        grid=(n_batch, groups),
        in_specs=[
            pl.BlockSpec(blk, lambda b, g: (b * groups + g, 0)),
            kv_spec(1, 2), kv_spec(1, 1), kv_spec(1, 0),
            kv_spec(2, 2), kv_spec(2, 1), kv_spec(2, 0),
            pl.BlockSpec((N_HEADS, CHUNK, BAND), lambda b, g: (0, 0, 0)),
        ],
        out_specs=pl.BlockSpec(blk, lambda b, g: (b * groups + g, 0)),
        out_shape=jax.ShapeDtypeStruct((n_batch * seq, ATT_W), BF16),
        scratch_shapes=[pltpu.VMEM((N_HEADS, ATT_ROWS, ATT_KEYS), F32)],
        compiler_params=_cparams("arbitrary", "arbitrary"),
        name="attn_prompt",
    )(z, z, z, z, z, z, z, b64)


def _attn_sample_kernel(q_ref, kn_ref, vn_ref, kc_ref, vc_ref, bias_ref, o_ref):
    for h in range(N_HEADS):
        sl = slice(h * HEAD_DIM, (h + 1) * HEAD_DIM)
        qh = q_ref[:, sl].astype(BF16)
        kh = jnp.concatenate([kc_ref[:, h, :], kn_ref[:, sl]], axis=0).astype(BF16)
        vh = jnp.concatenate([vc_ref[:, h, :], vn_ref[:, sl]], axis=0).astype(BF16)
        s = lax.dot_general(qh, kh, (((1,), (1,)), ((), ())), preferred_element_type=F32)
        s = s * ATT_SCALE + bias_ref[h]
        p = _softmax_rows(s).astype(BF16)
        o_ref[:, sl] = jnp.dot(p, vh, preferred_element_type=F32).astype(o_ref.dtype)


def _attn_sample(z, cache_k, cache_v, l, b64, row0, n_batch):
    base = row0 // CHUNK
    blk = (CHUNK, ATT_W)
    cache_spec = pl.BlockSpec((None, None, PAST_WINDOW, N_HEADS, HEAD_DIM), lambda b: (l, b, 0, 0, 0))
    return pl.pallas_call(
        _attn_sample_kernel,
        grid=(n_batch,),
        in_specs=[
            pl.BlockSpec(blk, lambda b: (base + b, 0)),
            pl.BlockSpec(blk, lambda b: (base + b, 1)),
            pl.BlockSpec(blk, lambda b: (base + b, 2)),
            cache_spec, cache_spec,
            pl.BlockSpec((N_HEADS, CHUNK, BAND), lambda b: (0, 0, 0)),
        ],
        out_specs=pl.BlockSpec(blk, lambda b: (b, 0)),
        out_shape=jax.ShapeDtypeStruct((n_batch * CHUNK, ATT_W), BF16),
        compiler_params=_cparams("arbitrary"),
        name="attn_sample",
    )(z, z, z, cache_k, cache_v, b64)


def _attention_bias(rel_bias):
    rev = rel_bias[::-1].astype(F32)
    flat = PAST_WINDOW - REL_CLIP + CHUNK - 1
    vec = jnp.concatenate([jnp.broadcast_to(rev[:1], (flat, N_HEADS)), rev[:BAND + CHUNK - 1 - flat]], axis=0)
    rows = [vec[CHUNK - 1 - qi:CHUNK - 1 - qi + BAND] for qi in range(CHUNK)]
    return jnp.transpose(jnp.stack(rows), (2, 0, 1))


def _octet_lhs(u3_ref):
    return jnp.concatenate([u3_ref[s] for s in range(SSM_T)], axis=1)


def _group_mask(rows, cols, row_shift, col_shift):
    r = lax.shift_right_logical(lax.broadcasted_iota(jnp.int32, (rows, cols), 0), row_shift)
    c = lax.shift_right_logical(lax.broadcasted_iota(jnp.int32, (rows, cols), 1), col_shift)
    return r == c


def _blockdiag(x, rep_ref, mask):
    e = jnp.dot(x, rep_ref[...], preferred_element_type=F32)
    return jnp.where(mask, e, 0.0).astype(BF16)


def _ssm_state_kernel(u3_ref, mre_ref, mim_ref, rep_ref, sre_ref, sim_ref, w_ref):
    mask = _group_mask(LANES, OCT_STATE, GROUP_SHIFT, STATE_SHIFT)
    for s in range(SSM_T):
        rows = slice(s * LANES, (s + 1) * LANES)
        w_ref[rows, :OCT_STATE] = _blockdiag(mre_ref[s], rep_ref, mask)
        w_ref[rows, OCT_STATE:] = _blockdiag(mim_ref[s], rep_ref, mask)
    s = jnp.dot(_octet_lhs(u3_ref), w_ref[...], preferred_element_type=F32)
    sre_ref[...] = s[:, :OCT_STATE]
    sim_ref[...] = s[:, OCT_STATE:]


def _ssm_state(u3, m_re, m_im, rep_state):
    n_sub = u3.shape[1]
    out = jax.ShapeDtypeStruct((N_OCT, n_sub, OCT_STATE), F32)
    m_spec = pl.BlockSpec((None, SSM_T, LANES, SSM_STATE), lambda j: (j, 0, 0, 0))
    return pl.pallas_call(
        _ssm_state_kernel,
        grid=(N_OCT,),
        in_specs=[
            pl.BlockSpec((SSM_T, n_sub, LANES), lambda j: (0, 0, j)),
            m_spec, m_spec,
            pl.BlockSpec((SSM_STATE, OCT_STATE), lambda j: (0, 0)),
        ],
        out_specs=[pl.BlockSpec((None, n_sub, OCT_STATE), lambda j: (j, 0, 0))] * 2,
        out_shape=[out, out],
        scratch_shapes=[pltpu.VMEM((OCT_K, 2 * OCT_STATE), BF16)],
        compiler_params=_cparams("arbitrary"),
        name="ssm_state",
    )(u3, m_re, m_im, rep_state)


def _ssm_carry_kernel(first_ref, seq_ref, sre_ref, sim_ref, ar_ref, ai_ref, h0r_ref, h0i_ref,
                      xsr_ref, xsi_ref, xer_ref, xei_ref, st_ref):
    blk = pl.program_id(0)
    steps = sre_ref.shape[1]

    @pl.when(blk == 0)
    def _():
        st_ref[...] = jnp.zeros_like(st_ref)

    ar = ar_ref[...]
    ai = ai_ref[...]

    def body(kk, carry):
        k = blk * steps + kk
        is_first = first_ref[k] == 1
        sq = seq_ref[k]
        xr = jnp.where(is_first, h0r_ref[sq], st_ref[0])
        xi = jnp.where(is_first, h0i_ref[sq], st_ref[1])
        xsr_ref[:, kk, :] = xr
        xsi_ref[:, kk, :] = xi
        nr = ar * xr - ai * xi + sre_ref[:, kk, :]
        ni = ar * xi + ai * xr + sim_ref[:, kk, :]
        xer_ref[:, kk, :] = nr
        xei_ref[:, kk, :] = ni
        st_ref[0] = nr
        st_ref[1] = ni
        return carry

    lax.fori_loop(0, steps, body, 0)


def _ssm_carry(first, seq_id, sre, sim, at_re, at_im, h0_re, h0_im):
    n_sub = sre.shape[1]
    steps = 64
    tile = (N_OCT, OCT_STATE)
    blk = pl.BlockSpec((N_OCT, steps, OCT_STATE), lambda i, *_: (0, i, 0))
    const = pl.BlockSpec(tile, lambda i, *_: (0, 0))
    tab = pl.BlockSpec((h0_re.shape[0],) + tile, lambda i, *_: (0, 0, 0))
    out = jax.ShapeDtypeStruct((N_OCT, n_sub, OCT_STATE), F32)
    return pl.pallas_call(
        _ssm_carry_kernel,
        grid_spec=pltpu.PrefetchScalarGridSpec(
            num_scalar_prefetch=2,
            grid=(n_sub // steps,),
            in_specs=[blk, blk, const, const, tab, tab],
            out_specs=[blk, blk, blk, blk],
            scratch_shapes=[pltpu.VMEM((2,) + tile, F32)],
        ),
        out_shape=[out, out, out, out],
        compiler_params=_cparams("arbitrary"),
        name="ssm_carry",
    )(first, seq_id, sre, sim, at_re.reshape(tile), at_im.reshape(tile),
      h0_re.reshape((-1,) + tile), h0_im.reshape((-1,) + tile))


def _ssm_out_kernel(u3_ref, taps_ref, xr_ref, xi_ref, ccr_ref, cci_ref, rep_ref, y3_ref, kt_ref, cr_ref, ci_ref):
    @pl.when(pl.program_id(0) == 0)
    def _():
        kt_ref[...] = jnp.zeros_like(kt_ref)

    tap_mask = _group_mask(LANES, LANES, GROUP_SHIFT, GROUP_SHIFT)
    for tau in range(SSM_T):
        block = _blockdiag(taps_ref[tau], rep_ref, tap_mask)
        for s in range(SSM_T - tau):
            t = s + tau
            kt_ref[s * LANES:(s + 1) * LANES, t * LANES:(t + 1) * LANES] = block
    c_mask = _group_mask(OCT_STATE, LANES, STATE_SHIFT, GROUP_SHIFT)
    for t in range(SSM_T):
        cols = slice(t * LANES, (t + 1) * LANES)
        cr_ref[:, cols] = _blockdiag(ccr_ref[t], rep_ref, c_mask)
        ci_ref[:, cols] = _blockdiag(cci_ref[t], rep_ref, c_mask)

    y = jnp.dot(_octet_lhs(u3_ref), kt_ref[...], preferred_element_type=F32)
    y += jnp.dot(xr_ref[...].astype(BF16), cr_ref[...], preferred_element_type=F32)
    y += jnp.dot(xi_ref[...].astype(BF16), ci_ref[...], preferred_element_type=F32)
    for t in range(SSM_T):
        y3_ref[t] = y[:, t * LANES:(t + 1) * LANES]


def _ssm_out(u3, taps, xs_re, xs_im, c_re, c_im, rep_group):
    n_sub = u3.shape[1]
    u_spec = pl.BlockSpec((SSM_T, n_sub, LANES), lambda j: (0, 0, j))
    x_spec = pl.BlockSpec((None, n_sub, OCT_STATE), lambda j: (j, 0, 0))
    c_spec = pl.BlockSpec((None, SSM_T, OCT_STATE, SSM_GROUP), lambda j: (j, 0, 0, 0))
    return pl.pallas_call(
        _ssm_out_kernel,
        grid=(N_OCT,),
        in_specs=[u_spec, pl.BlockSpec((None, SSM_T, LANES, SSM_GROUP), lambda j: (j, 0, 0, 0)),
                  x_spec, x_spec, c_spec, c_spec, pl.BlockSpec((SSM_GROUP, LANES), lambda j: (0, 0))],
        out_specs=u_spec,
        out_shape=jax.ShapeDtypeStruct((SSM_T, n_sub, SSM_W), F32),
        scratch_shapes=[pltpu.VMEM((OCT_K, OCT_K), BF16), pltpu.VMEM((OCT_STATE, OCT_K), BF16),
                        pltpu.VMEM((OCT_STATE, OCT_K), BF16)],
        compiler_params=_cparams("arbitrary"),
        name="ssm_out",
    )(u3, taps, xs_re, xs_im, c_re, c_im, rep_group)


def _ssm_matrices(a_re, a_im, log_step, b_re, b_im, c_re, c_im):
    hp = lax.Precision.HIGHEST
    G, P, T, O = SSM_GROUPS, SSM_STATE, SSM_T, SSM_OCT
    step = jnp.exp(log_step.astype(F32))[:, None]
    mag = jnp.exp(a_re * step)
    ang = a_im * step
    ab_re = mag * jnp.cos(ang)
    ab_im = mag * jnp.sin(ang)
    den = a_re * a_re + a_im * a_im
    n_re = ab_re - 1.0
    f_re = (n_re * a_re + ab_im * a_im) / den
    f_im = (ab_im * a_re - n_re * a_im) / den
    bb_re = f_re[..., None] * b_re - f_im[..., None] * b_im
    bb_im = f_re[..., None] * b_im + f_im[..., None] * b_re
    pr, pi = [jnp.ones_like(ab_re)], [jnp.zeros_like(ab_re)]
    for _ in range(T):
        pr.append(pr[-1] * ab_re - pi[-1] * ab_im)
        pi.append(pr[-2] * ab_im + pi[-1] * ab_re)
    pw_re = jnp.stack(pr)
    pw_im = jnp.stack(pi)
    ab_r = pw_re[:T, :, :, None] * bb_re[None] - pw_im[:T, :, :, None] * bb_im[None]
    ab_i = pw_re[:T, :, :, None] * bb_im[None] + pw_im[:T, :, :, None] * bb_re[None]
    taps = (jnp.einsum('gcp,tgpd->tgcd', c_re, ab_r, precision=hp)
            - jnp.einsum('gcp,tgpd->tgcd', c_im, ab_i, precision=hp))

    def per_octet(x, perm, rows, cols):
        return jnp.transpose(x, perm).reshape(N_OCT, T, rows, cols).astype(BF16)

    tap_rows = per_octet(taps.reshape(T, N_OCT, O, SSM_GROUP, SSM_GROUP), (1, 0, 2, 4, 3), LANES, SSM_GROUP)

    def inject(m):
        return per_octet(m[::-1].reshape(T, N_OCT, O, P, SSM_GROUP), (1, 0, 2, 4, 3), LANES, P)

    def readout(ca):
        return per_octet(ca.reshape(T, N_OCT, O, SSM_GROUP, P), (1, 0, 2, 4, 3), OCT_STATE, SSM_GROUP)

    ca_re = c_re[None] * pw_re[1:][:, :, None, :] - c_im[None] * pw_im[1:][:, :, None, :]
    ca_im = c_re[None] * pw_im[1:][:, :, None, :] + c_im[None] * pw_re[1:][:, :, None, :]
    return dict(taps=tap_rows, m_re=inject(ab_r), m_im=inject(ab_i),
                c_re=readout(ca_re), c_im=readout(-ca_im),
                rep_group=jnp.tile(jnp.eye(SSM_GROUP, dtype=BF16), (1, O)),
                rep_state=jnp.tile(jnp.eye(P, dtype=BF16), (1, O)),
                at_re=pw_re[T], at_im=pw_im[T])


def _ssm(u3, mats, first, seq_id, h0_re, h0_im):
    sre, sim = _ssm_state(u3, mats['m_re'], mats['m_im'], mats['rep_state'])
    xsr, xsi, xer, xei = _ssm_carry(first, seq_id, sre, sim, mats['at_re'], mats['at_im'], h0_re, h0_im)
    y3 = _ssm_out(u3, mats['taps'], xsr, xsi, mats['c_re'], mats['c_im'], mats['rep_group'])
    return y3, xer, xei


def _states_at(xe, idx):
    return jnp.transpose(xe[:, idx, :], (1, 0, 2)).reshape(idx.shape[0], SSM_GROUPS, SSM_STATE)


def _gelu_tanh(x):
    return 0.5 * x * (1.0 + jnp.tanh(math.sqrt(2.0 / math.pi) * (x + 0.044715 * (x * x * x))))


def _glu_kernel(y3_ref, u_ref, d_ref, w_ref, o_ref, wbf_ref, scr_ref):
    @pl.when(pl.program_id(0) == 0)
    def _():
        wbf_ref[...] = w_ref[...].astype(BF16)

    rows = y3_ref.shape[1]
    n_col = SSM_W // LANES
    for t in range(SSM_T):
        for c in range(n_col):
            scr_ref[c, pl.ds(t, rows, stride=SSM_T), :] = y3_ref[t, :, c * LANES:(c + 1) * LANES]
    y = jnp.concatenate([scr_ref[c] for c in range(n_col)], axis=1)
    ys = _gelu_tanh(y + d_ref[...] * u_ref[...])
    t = jnp.dot(ys.astype(BF16), wbf_ref[...], preferred_element_type=F32)
    o_ref[...] = (ys * jax.nn.sigmoid(t)).astype(o_ref.dtype)


def _glu(y3, z, d_all, w_all, l):
    nt = z.shape[0]
    tm = 512
    return pl.pallas_call(
        _glu_kernel,
        grid=(nt // tm,),
        in_specs=[
            pl.BlockSpec((SSM_T, tm // SSM_T, SSM_W), lambda i: (0, i, 0)),
            pl.BlockSpec((tm, SSM_W), lambda i: (i, 3 * ATT_W // SSM_W)),
            pl.BlockSpec((None, 1, SSM_W), lambda i: (l, 0, 0)),
            pl.BlockSpec((None, SSM_W, SSM_W), lambda i: (l, 0, 0)),
        ],
        out_specs=pl.BlockSpec((tm, SSM_W), lambda i: (i, 0)),
        out_shape=jax.ShapeDtypeStruct((nt, SSM_W), BF16),
        scratch_shapes=[pltpu.VMEM((SSM_W, SSM_W), BF16), pltpu.VMEM((SSM_W // LANES, tm, LANES), F32)],
        compiler_params=_cparams("arbitrary"),
        name="glu",
    )(y3, z, d_all.reshape(DEPTH, 1, SSM_W), w_all)


def _merge_kernel(o_ref, y_ref, ga_ref, gs_ref, wa_ref, ws_ref, m_ref):
    a = jnp.dot(o_ref[...], wa_ref[...].astype(BF16), preferred_element_type=F32)
    s = jnp.dot(y_ref[...], ws_ref[...].astype(BF16), preferred_element_type=F32)
    m_ref[...] = (jax.nn.sigmoid(ga_ref[...]) * a + jax.nn.sigmoid(gs_ref[...]) * s).astype(m_ref.dtype)


def _merge(o_p, o_s, ys2, z, w_ba_all, w_bs_all, l):
    nt = ys2.shape[0]
    tm, tn = 1024, 512
    ga0 = (3 * ATT_W + SSM_W) // tn
    gs0 = ga0 + D_MODEL // tn
    p_tiles = o_p.shape[0] // tm
    s_tiles = o_s.shape[0] // tm
    assert p_tiles * tm == o_p.shape[0] and s_tiles * tm == o_s.shape[0]

    def kern(op_ref, os_ref, *rest):
        i = pl.program_id(0)

        @pl.when(i < p_tiles)
        def _():
            _merge_kernel(op_ref, *rest)

        @pl.when(i >= p_tiles)
        def _():
            _merge_kernel(os_ref, *rest)

    return pl.pallas_call(
        kern,
        grid=(nt // tm, D_MODEL // tn),
        in_specs=[
            pl.BlockSpec((tm, ATT_W), lambda i, j: (jnp.minimum(i, p_tiles - 1), 0)),
            pl.BlockSpec((tm, ATT_W), lambda i, j: (jnp.maximum(i - p_tiles, 0), 0)),
            pl.BlockSpec((tm, SSM_W), lambda i, j: (i, 0)),
            pl.BlockSpec((tm, tn), lambda i, j: (i, ga0 + j)),
            pl.BlockSpec((tm, tn), lambda i, j: (i, gs0 + j)),
            pl.BlockSpec((None, ATT_W, tn), lambda i, j: (l, 0, j)),
            pl.BlockSpec((None, SSM_W, tn), lambda i, j: (l, 0, j)),
        ],
        out_specs=pl.BlockSpec((tm, tn), lambda i, j: (i, j)),
        out_shape=jax.ShapeDtypeStruct((nt, D_MODEL), BF16),
        compiler_params=_cparams("arbitrary", "arbitrary"),
        name="merge",
    )(o_p, o_s, ys2, z, z, w_ba_all, w_bs_all)


def _proj_residual_kernel(a_ref, w_ref, x_ref, g_ref, o_ref):
    r = jnp.dot(a_ref[...], w_ref[...].astype(BF16), preferred_element_type=F32)
    tm, tn = r.shape
    r3 = r.reshape(tm // CHUNK, CHUNK, tn) * g_ref[...]
    o_ref[...] = x_ref[...] + r3.reshape(tm, tn)


def _proj_residual(a, w_all, x, tab, l, which_gate, x_tail=None):
    nt, k = a.shape
    tm, tn = 1024, 512
    cpt = tm // CHUNK
    kern = _proj_residual_kernel
    x_specs = [pl.BlockSpec((tm, tn), lambda i, j: (i, j))]
    xs = [x]
    if x_tail is not None:
        assert x.shape[0] % tm == 0 and x_tail.shape[0] % tm == 0 and x.shape[0] + x_tail.shape[0] == nt
        head_tiles = x.shape[0] // tm
        kern = _split_rows(_proj_residual_kernel, 2, head_tiles)
        x_specs = _split_specs((tm, tn), head_tiles, True)
        xs = [x, x_tail]
    return pl.pallas_call(
        kern,
        grid=(nt // tm, D_MODEL // tn),
        in_specs=[
            pl.BlockSpec((tm, k), lambda i, j: (i, 0)),
            pl.BlockSpec((None, k, tn), lambda i, j: (l, 0, j)),
            *x_specs,
            _tab_spec(cpt, l, which_gate, width=tn, col=True),
        ],
        out_specs=pl.BlockSpec((tm, tn), lambda i, j: (i, j)),
        out_shape=jax.ShapeDtypeStruct((nt, D_MODEL), F32),
        compiler_params=_cparams("arbitrary", "arbitrary"),
        name="proj_residual",
    )(a, w_all, *xs, tab)


def _swiglu_step(h, wg_ref, wu_ref, wd_ref):
    hg = jnp.dot(h, wg_ref[...].astype(BF16), preferred_element_type=F32)
    hu = jnp.dot(h, wu_ref[...].astype(BF16), preferred_element_type=F32)
    a = (hg * jax.nn.sigmoid(hg)) * hu
    return jnp.dot(a.astype(BF16), wd_ref[...].astype(BF16), preferred_element_type=F32)


def _ffn_kernel(h_ref, wg_ref, wu_ref, wd_ref, o_ref):
    @pl.when(pl.program_id(1) == 0)
    def _():
        o_ref[...] = jnp.zeros_like(o_ref)

    o_ref[...] += _swiglu_step(h_ref[...], wg_ref, wu_ref, wd_ref)


def _ffn(h, w_gate, w_up, w_down, j):
    nt = h.shape[0]
    d_ff = w_gate.shape[-1]
    tm, tf = FFN_TM, FFN_TF
    return pl.pallas_call(
        _ffn_kernel,
        grid=(nt // tm, d_ff // tf),
        in_specs=[
            pl.BlockSpec((tm, D_MODEL), lambda i, f: (i, 0)),
            pl.BlockSpec((None, D_MODEL, tf), lambda i, f: (j, 0, f)),
            pl.BlockSpec((None, D_MODEL, tf), lambda i, f: (j, 0, f)),
            pl.BlockSpec((None, tf, D_MODEL), lambda i, f: (j, f, 0)),
        ],
        out_specs=pl.BlockSpec((tm, D_MODEL), lambda i, f: (i, 0)),
        out_shape=jax.ShapeDtypeStruct((nt, D_MODEL), F32),
        compiler_params=_cparams("arbitrary", "arbitrary"),
        name="ffn",
    )(h, w_gate, w_up, w_down)


def _moe_plan(route, nt):
    tm = MOE_TM
    n_assign = 2 * nt
    n_tiles = (n_assign + N_EXPERTS * (tm - 1)) // tm
    e_flat = jnp.concatenate([route[:, 0], route[:, 1]]).astype(jnp.int32)
    onehot = (e_flat[:, None] == jnp.arange(N_EXPERTS, dtype=jnp.int32)[None, :]).astype(jnp.int32)
    csum = jnp.cumsum(onehot, axis=0)
    counts = csum[-1]
    rank = jnp.sum((csum - onehot) * onehot, axis=1)
    tiles_per = (counts + tm - 1) // tm
    tile_end = jnp.cumsum(tiles_per)
    offsets = (tile_end - tiles_per) * tm
    dest = jnp.sum(onehot * offsets[None, :], axis=1) + rank
    tidx = jnp.arange(n_tiles, dtype=jnp.int32)
    tile_expert = jnp.minimum(jnp.sum((tidx[:, None] >= tile_end[None, :]).astype(jnp.int32), axis=1), N_EXPERTS - 1)
    tile_valid = (tidx < tile_end[-1]).astype(jnp.int32)
    tok = jnp.concatenate([jnp.arange(nt, dtype=jnp.int32)] * 2)
    src_token = jnp.zeros((n_tiles * tm,), jnp.int32).at[dest].set(tok)
    return dict(dest1=dest[:nt], dest2=dest[nt:], src_token=src_token,
                tile_expert=tile_expert.astype(jnp.int32), tile_valid=tile_valid)


def _row_copy(src_ref, row, dst_ref, r, sem):
    return pltpu.make_async_copy(src_ref.at[pl.ds(row, 1)], dst_ref.at[pl.ds(r, 1)], sem)


def _gather_start(idx_ref, base, src_ref, dst_ref, sem):
    rows = dst_ref.shape[0]

    def group(gi, c):
        for u in range(GATHER_UNROLL):
            r = gi * GATHER_UNROLL + u
            _row_copy(src_ref, idx_ref[base + r], dst_ref, r, sem).start(priority=u % 2)
        return c

    lax.fori_loop(0, rows // GATHER_UNROLL, group, 0)


def _gather_wait(src_ref, dst_ref, sem):
    pltpu.make_async_copy(src_ref.at[pl.ds(0, dst_ref.shape[0])], dst_ref, sem).wait()


def _gather_rows_kernel(idx_ref, live_ref, src_ref, o_ref, buf_ref, sem):
    i = pl.program_id(0)
    n = pl.num_programs(0)
    slot = i % 2
    rows = buf_ref.shape[1]

    @pl.when((i == 0) & (live_ref[0] == 1))
    def _():
        _gather_start(idx_ref, 0, src_ref, buf_ref.at[0], sem.at[0])

    @pl.when((i + 1 < n) & (live_ref[jnp.minimum(i + 1, n - 1)] == 1))
    def _():
        _gather_start(idx_ref, (i + 1) * rows, src_ref, buf_ref.at[1 - slot], sem.at[1 - slot])

    @pl.when(live_ref[i] == 1)
    def _():
        _gather_wait(src_ref, buf_ref.at[slot], sem.at[slot])
        o_ref[...] = buf_ref[slot].astype(o_ref.dtype)

    @pl.when(live_ref[i] == 0)
    def _():
        o_ref[...] = jnp.zeros_like(o_ref)


def _gather_rows(src, idx, live, out_dtype):
    n_out = idx.shape[0]
    width = src.shape[1]
    rows = MOE_TM
    assert n_out == live.shape[0] * rows and rows % GATHER_UNROLL == 0
    return pl.pallas_call(
        _gather_rows_kernel,
        grid_spec=pltpu.PrefetchScalarGridSpec(
            num_scalar_prefetch=2,
            grid=(n_out // rows,),
            in_specs=[pl.BlockSpec(memory_space=pl.ANY)],
            out_specs=pl.BlockSpec((rows, width), lambda i, *_: (i, 0)),
            scratch_shapes=[pltpu.VMEM((2, rows, width), src.dtype), pltpu.SemaphoreType.DMA((2,))],
        ),
        out_shape=jax.ShapeDtypeStruct((n_out, width), out_dtype),
        compiler_params=_cparams("arbitrary"),
        name="moe_gather",
    )(idx, live, src)


def _moe_ffn_kernel(te_ref, tv_ref, h_ref, wg_ref, wu_ref, wd_ref, o_ref):
    t = pl.program_id(0)

    @pl.when(pl.program_id(1) == 0)
    def _():
        o_ref[...] = jnp.zeros_like(o_ref)

    @pl.when(tv_ref[t] == 1)
    def _():
        o_ref[...] += _swiglu_step(h_ref[...], wg_ref, wu_ref, wd_ref)


def _moe_ffn(h_sorted, tile_expert, tile_valid, w_gate, w_up, w_down, j):
    d_ff = w_gate.shape[-1]
    tm, tf = MOE_TM, FFN_TF
    n_tiles = h_sorted.shape[0] // tm
    nf = d_ff // tf

    def f_idx(t, f, tv):
        return jnp.where(tv[t] == 1, f, nf - 1)

    return pl.pallas_call(
        _moe_ffn_kernel,
        grid_spec=pltpu.PrefetchScalarGridSpec(
            num_scalar_prefetch=2,
            grid=(n_tiles, nf),
            in_specs=[
                pl.BlockSpec((tm, D_MODEL), lambda t, f, te, tv: (t, 0)),
                pl.BlockSpec((None, None, D_MODEL, tf), lambda t, f, te, tv: (j, te[t], 0, f_idx(t, f, tv))),
                pl.BlockSpec((None, None, D_MODEL, tf), lambda t, f, te, tv: (j, te[t], 0, f_idx(t, f, tv))),
                pl.BlockSpec((None, None, tf, D_MODEL), lambda t, f, te, tv: (j, te[t], f_idx(t, f, tv), 0)),
            ],
            out_specs=pl.BlockSpec((tm, D_MODEL), lambda t, f, te, tv: (t, 0)),
        ),
        out_shape=jax.ShapeDtypeStruct((n_tiles * tm, D_MODEL), F32),
        compiler_params=_cparams("arbitrary", "arbitrary"),
        name="moe_ffn",
    )(tile_expert, tile_valid, h_sorted, w_gate, w_up, w_down)


def _moe_combine_kernel(d1_ref, d2_ref, y_ref, x_ref, route_ref, g_ref, gn_ref, *rest, final_norm, split_tiles):
    n_out = 1 if split_tiles is None else 2
    o_refs, (a_ref, b_ref, sem_a, sem_b) = rest[:n_out], rest[n_out:]
    i = pl.program_id(0)
    n = pl.num_programs(0)
    slot = i % 2

    def start(step, s):
        _gather_start(d1_ref, step * GATHER_ROWS, y_ref, a_ref.at[s], sem_a.at[s])
        _gather_start(d2_ref, step * GATHER_ROWS, y_ref, b_ref.at[s], sem_b.at[s])

    @pl.when(i == 0)
    def _():
        start(0, 0)

    @pl.when(i + 1 < n)
    def _():
        start(i + 1, 1 - slot)

    _gather_wait(y_ref, a_ref.at[slot], sem_a.at[slot])
    _gather_wait(y_ref, b_ref.at[slot], sem_b.at[slot])
    route = route_ref[...]
    r = route[:, 2:3] * a_ref[slot] + route[:, 3:4] * b_ref[slot]
    tm, n_col = r.shape
    xn = x_ref[...] + (r.reshape(tm // CHUNK, CHUNK, n_col) * g_ref[...]).reshape(tm, n_col)
    if final_norm:
        xn = xn * lax.rsqrt(jnp.mean(xn * xn, axis=-1, keepdims=True) + EPS) * gn_ref[...]
    if split_tiles is None:
        o_refs[0][...] = xn
    else:
        @pl.when(i < split_tiles)
        def _():
            o_refs[0][...] = xn

        @pl.when(i >= split_tiles)
        def _():
            o_refs[1][...] = xn


def _moe_combine(y_sorted, dest1, dest2, x, route, tab, l, which_gate, g_final=None, split_rows=None):
    final_norm = g_final is not None
    gn = (g_final if final_norm else jnp.ones((D_MODEL,), F32)).reshape(1, D_MODEL)
    nt = x.shape[0]
    tm = GATHER_ROWS
    blk = pl.BlockSpec((tm, D_MODEL), lambda i, *_: (i, 0))
    buf = pltpu.VMEM((2, tm, D_MODEL), F32)
    if split_rows is None:
        split_tiles = None
        out_specs = blk
        out_shape = jax.ShapeDtypeStruct((nt, D_MODEL), F32)
    else:
        split_tiles = split_rows // tm
        assert split_tiles * tm == split_rows and 0 < split_rows < nt
        out_specs = [pl.BlockSpec((tm, D_MODEL), lambda i, *_: (jnp.minimum(i, split_tiles - 1), 0)),
                     pl.BlockSpec((tm, D_MODEL), lambda i, *_: (jnp.maximum(i - split_tiles, 0), 0))]
        out_shape = [jax.ShapeDtypeStruct((split_rows, D_MODEL), F32),
                     jax.ShapeDtypeStruct((nt - split_rows, D_MODEL), F32)]
    return pl.pallas_call(
        functools.partial(_moe_combine_kernel, final_norm=final_norm, split_tiles=split_tiles),
        grid_spec=pltpu.PrefetchScalarGridSpec(
            num_scalar_prefetch=2,
            grid=(nt // tm,),
            in_specs=[pl.BlockSpec(memory_space=pl.ANY), blk, pl.BlockSpec((tm, LANES), lambda i, *_: (i, 0)),
                      _tab_spec(tm // CHUNK, l, which_gate), pl.BlockSpec((1, D_MODEL), lambda i, *_: (0, 0))],
            out_specs=out_specs,
            scratch_shapes=[buf, buf, pltpu.SemaphoreType.DMA((2,)), pltpu.SemaphoreType.DMA((2,))],
        ),
        out_shape=out_shape,
        compiler_params=_cparams("arbitrary"),
        name="moe_combine",
    )(dest1, dest2, y_sorted, x, route, tab, gn)


def _residual_kernel(x_ref, r_ref, g_ref, o_ref):
    r = r_ref[...]
    tm, n = r.shape
    o_ref[...] = x_ref[...] + (r.reshape(tm // CHUNK, CHUNK, n) * g_ref[...]).reshape(tm, n)


def _residual(x, r, tab, l, which_gate):
    nt = x.shape[0]
    tm = 512
    blk = pl.BlockSpec((tm, D_MODEL), lambda i: (i, 0))
    return pl.pallas_call(
        _residual_kernel, grid=(nt // tm,),
        in_specs=[blk, blk, _tab_spec(tm // CHUNK, l, which_gate)],
        out_specs=blk, out_shape=jax.ShapeDtypeStruct((nt, D_MODEL), F32),
        compiler_params=_cparams("arbitrary"), name="residual",
    )(x, r, tab)


def _residual_norm_kernel(x_ref, r_ref, gate_ref, g_ref, sh_ref, sc_ref, xo_ref, h_ref):
    r = r_ref[...]
    tm, n = r.shape
    xn = x_ref[...] + (r.reshape(tm // CHUNK, CHUNK, n) * gate_ref[...]).reshape(tm, n)
    xo_ref[...] = xn
    h_ref[...] = _modulate(xn, g_ref, sh_ref, sc_ref).astype(BF16)


def _residual_norm(x, r, tab, l, which_gate, g_all, l_next, which_shift, which_scale):
    nt = x.shape[0]
    tm = 512
    cpt = tm // CHUNK
    blk = pl.BlockSpec((tm, D_MODEL), lambda i: (i, 0))
    return pl.pallas_call(
        _residual_norm_kernel, grid=(nt // tm,),
        in_specs=[blk, blk, _tab_spec(cpt, l, which_gate),
                  pl.BlockSpec((None, 1, D_MODEL), lambda i: (l_next, 0, 0)),
                  _tab_spec(cpt, l_next, which_shift), _tab_spec(cpt, l_next, which_scale)],
        out_specs=[blk, blk],
        out_shape=[jax.ShapeDtypeStruct((nt, D_MODEL), F32), jax.ShapeDtypeStruct((nt, D_MODEL), BF16)],
        compiler_params=_cparams("arbitrary"), name="residual_norm",
    )(x, r, tab, g_all.reshape(DEPTH, 1, D_MODEL), tab, tab)


def _kv_rows_kernel(z_ref, o_ref):
    for h in range(N_HEADS):
        o_ref[:, h, :] = z_ref[:, h * HEAD_DIM:(h + 1) * HEAD_DIM]


def _kv_rows(z, n_pb, seq, n_sb):
    rows = PAST_WINDOW
    per_seq = seq // rows
    s_blocks = n_sb * CHUNK // rows
    assert per_seq * rows == seq and s_blocks * rows == n_sb * CHUNK

    def row_block(r):
        return jnp.where(r < n_pb, (r + 1) * per_seq - 1, n_pb * per_seq + (r - n_pb))

    n_blocks = n_pb + s_blocks
    return pl.pallas_call(
        _kv_rows_kernel, grid=(2, n_blocks),
        in_specs=[pl.BlockSpec((rows, ATT_W), lambda w, r: (row_block(r), 1 + w))],
        out_specs=pl.BlockSpec((None, rows, N_HEADS, HEAD_DIM), lambda w, r: (w, r, 0, 0)),
        out_shape=jax.ShapeDtypeStruct((2, n_blocks * rows, N_HEADS, HEAD_DIM), F32),
        compiler_params=_cparams("arbitrary", "arbitrary"), name="kv_rows",
    )(z)


def _final_norm_kernel(x_ref, g_ref, o_ref):
    x = x_ref[...]
    o_ref[...] = x * lax.rsqrt(jnp.mean(x * x, axis=-1, keepdims=True) + EPS) * g_ref[...]


def _final_norm(x, g):
    nt = x.shape[0]
    tm = 512
    blk = pl.BlockSpec((tm, D_MODEL), lambda i: (i, 0))
    return pl.pallas_call(
        _final_norm_kernel, grid=(nt // tm,),
        in_specs=[blk, pl.BlockSpec((1, D_MODEL), lambda i: (0, 0))],
        out_specs=blk, out_shape=jax.ShapeDtypeStruct((nt, D_MODEL), F32),
        compiler_params=_cparams("arbitrary"), name="final_norm",
    )(x, g.reshape(1, D_MODEL))


def kernel(x_prompt, x_sample, cache_k, cache_v, state_ssm_re, state_ssm_im, c_prompt, c_sample, w_ada, b_ada, g_norm_mix, g_norm_ffn, g_norm_final, w_in, rel_bias, ssm_a_re, ssm_a_im, ssm_log_step, ssm_b_re, ssm_b_im, ssm_c_re, ssm_c_im, ssm_d, w_ssm_glu, w_branch_attn, w_branch_ssm, w_out, w_ffn_gate, w_ffn_up, w_ffn_down, w_router, b_router, w_exp_gate, w_exp_up, w_exp_down):
    n_pb, seq = x_prompt.shape[0], x_prompt.shape[1]
    n_sb, dec = x_sample.shape[0], x_sample.shape[1]
    assert dec == CHUNK and cache_k.shape[2] == PAST_WINDOW and seq % ATT_ROWS == 0
    np_tok = n_pb * seq
    nt = np_tok + n_sb * dec
    n_batch = n_pb + n_sb

    x_head = x_prompt.reshape(np_tok, D_MODEL)
    x_tail = x_sample.reshape(n_sb * dec, D_MODEL)

    c_all = jnp.concatenate([c_prompt, c_sample], axis=0)
    c_pad = jnp.pad(c_all, ((0, -n_batch % 8), (0, 0)))
    ada = _ada(c_pad, w_ada, b_ada)
    ada_p = jnp.broadcast_to(ada[:, :n_pb, None, :], (DEPTH, n_pb, seq // CHUNK, 6 * D_MODEL))
    tab = jnp.concatenate([ada_p.reshape(DEPTH, np_tok // CHUNK, 6 * D_MODEL), ada[:, n_pb:n_batch]], axis=1)
    tab = tab.reshape(DEPTH, nt // CHUNK, 6, 1, D_MODEL)

    sub_p, sub_s = seq // SSM_T, dec // SSM_T
    seq_id = jnp.concatenate([jnp.repeat(jnp.arange(n_pb), sub_p), n_pb + jnp.repeat(jnp.arange(n_sb), sub_s)])
    seq_id = seq_id.astype(jnp.int32)
    first = jnp.concatenate([jnp.arange(n_pb * sub_p) % sub_p == 0, jnp.arange(n_sb * sub_s) % sub_s == 0])
    first = first.astype(jnp.int32)
    last_p = (jnp.arange(n_pb) + 1) * sub_p - 1
    last_s = n_pb * sub_p + (jnp.arange(n_sb) + 1) * sub_s - 1
    zeros_h0 = jnp.zeros((n_pb, SSM_GROUPS, SSM_STATE), F32)

    outs = {k: [] for k in ('kp', 'vp', 'srp', 'sip', 'ks', 'vs', 'srs', 'sis')}
    keep = min(PAST_WINDOW, seq)
    h = _norm_mod(x_head, g_norm_mix, tab, 0, SHIFT1, SCALE1, x_tail=x_tail)
    for l in range(DEPTH):
        last = l == DEPTH - 1
        z, u3 = _in_proj(h, w_in, l)

        b64 = _attention_bias(rel_bias[l])
        o_p = _attn_prompt(z, b64, n_pb, seq)
        o_s = _attn_sample(z, cache_k, cache_v, l, b64, np_tok, n_sb)

        mats = _ssm_matrices(ssm_a_re[l], ssm_a_im[l], ssm_log_step[l], ssm_b_re[l], ssm_b_im[l],
                             ssm_c_re[l], ssm_c_im[l])
        h0_re = jnp.concatenate([zeros_h0, state_ssm_re[l].astype(F32)], axis=0)
        h0_im = jnp.concatenate([zeros_h0, state_ssm_im[l].astype(F32)], axis=0)
        y3, xe_re, xe_im = _ssm(u3, mats, first, seq_id, h0_re, h0_im)

        ys2 = _glu(y3, z, ssm_d, w_ssm_glu, l)
        merged = _merge(o_p, o_s, ys2, z, w_branch_attn, w_branch_ssm, l)
        if l == 0:
            x = _proj_residual(merged, w_out, x_head, tab, l, GATE1, x_tail=x_tail)
        else:
            x = _proj_residual(merged, w_out, x, tab, l, GATE1)

        j = l // 2
        if l % 2 == 0:
            h2 = _norm_mod(x, g_norm_ffn, tab, l, SHIFT2, SCALE2)
            r = _ffn(h2, w_ffn_gate, w_ffn_up, w_ffn_down, j)
            if last:
                yn = _final_norm(_residual(x, r, tab, l, GATE2), g_norm_final)
                yn_p, yn_s = yn[:np_tok], yn[np_tok:]
            else:
                x, h = _residual_norm(x, r, tab, l, GATE2, g_norm_mix, l + 1, SHIFT1, SCALE1)
        else:
            h2, route = _norm_mod(x, g_norm_ffn, tab, l, SHIFT2, SCALE2, router=(w_router[j], b_router[j]))
            plan = _moe_plan(route, nt)
            h_sorted = _gather_rows(h2, plan['src_token'], plan['tile_valid'], BF16)
            y_sorted = _moe_ffn(h_sorted, plan['tile_expert'], plan['tile_valid'], w_exp_gate, w_exp_up, w_exp_down, j)
            if last:
                yn_p, yn_s = _moe_combine(y_sorted, plan['dest1'], plan['dest2'], x, route, tab, l, GATE2,
                                          g_final=g_norm_final, split_rows=np_tok)
            else:
                x = _moe_combine(y_sorted, plan['dest1'], plan['dest2'], x, route, tab, l, GATE2)
                h = _norm_mod(x, g_norm_mix, tab, l + 1, SHIFT1, SCALE1)

        kv = _kv_rows(z, n_pb, seq, n_sb)
        outs['kp'].append(kv[0, :n_pb * keep].reshape(n_pb, keep, N_HEADS, HEAD_DIM))
        outs['vp'].append(kv[1, :n_pb * keep].reshape(n_pb, keep, N_HEADS, HEAD_DIM))
        outs['ks'].append(kv[0, n_pb * keep:].reshape(n_sb, dec, N_HEADS, HEAD_DIM))
        outs['vs'].append(kv[1, n_pb * keep:].reshape(n_sb, dec, N_HEADS, HEAD_DIM))
        outs['srp'].append(_states_at(xe_re, last_p))
        outs['sip'].append(_states_at(xe_im, last_p))
        outs['srs'].append(_states_at(xe_re, last_s))
        outs['sis'].append(_states_at(xe_im, last_s))

    y_prompt = yn_p.reshape(n_pb, seq, D_MODEL)
    y_sample = yn_s.reshape(n_sb, dec, D_MODEL)
    st = lambda name: jnp.stack(outs[name])
    return (y_prompt, y_sample, st('kp'), st('vp'), st('srp'), st('sip'),
            st('ks'), st('vs'), st('srs'), st('sis'))
```

```python
import functools
import math

import jax
import jax.numpy as jnp
from jax import lax
from jax.experimental import pallas as pl
from jax.experimental.pallas import tpu as pltpu

F32 = jnp.float32
BF16 = jnp.bfloat16

D_MODEL = 2048
DEPTH = 2
CHUNK = 64
PAST_CHUNKS = 8
PAST_WINDOW = PAST_CHUNKS * CHUNK
BAND = PAST_WINDOW + CHUNK
N_HEADS = 8
HEAD_DIM = 128
ATT_W = N_HEADS * HEAD_DIM
ATT_SCALE = HEAD_DIM ** -0.5
REL_CLIP = 256
SSM_W = 1024
SSM_GROUP = 16
SSM_GROUPS = SSM_W // SSM_GROUP
SSM_STATE = 64
N_EXPERTS = 8
IN_COLS = 3 * ATT_W + SSM_W + 2 * D_MODEL
EPS = 1e-6
NEG_INF = -1e30

LANES = 128
VMEM_LIMIT_BYTES = 56 * 1024 * 1024

SSM_T = 16
SSM_OCT = LANES // SSM_GROUP
N_OCT = SSM_GROUPS // SSM_OCT
OCT_K = SSM_T * LANES
OCT_STATE = SSM_OCT * SSM_STATE
GROUP_SHIFT = SSM_GROUP.bit_length() - 1
STATE_SHIFT = SSM_STATE.bit_length() - 1
assert (1 << GROUP_SHIFT, 1 << STATE_SHIFT) == (SSM_GROUP, SSM_STATE)
ATT_GROUP = 4
ATT_ROWS = ATT_GROUP * CHUNK
ATT_KEYS = 3 * ATT_ROWS

IN_TM, IN_TN = 1536, 512
FFN_TM = 768
MOE_TM = 768
FFN_TF = 512
GATHER_ROWS = 512
GATHER_UNROLL = 8
GATHER_STREAMS = 4
assert GATHER_UNROLL % GATHER_STREAMS == 0

SHIFT1, SCALE1, GATE1, SHIFT2, SCALE2, GATE2 = range(6)


def _cparams(*sem):
    return pltpu.CompilerParams(dimension_semantics=sem, vmem_limit_bytes=VMEM_LIMIT_BYTES)


def _tab_spec(cpt, l, which, width=D_MODEL, col=None):
    if col is None:
        return pl.BlockSpec((None, cpt, None, 1, width), lambda i, *_: (l, i, which, 0, 0))
    return pl.BlockSpec((None, cpt, None, 1, width), lambda i, j, *_: (l, i, which, 0, j))


def _ada_kernel(c_ref, w_ref, b_ref, o_ref):
    c = c_ref[...]
    a = (c * jax.nn.sigmoid(c)).astype(BF16)
    o_ref[...] = jnp.dot(a, w_ref[...].astype(BF16), preferred_element_type=F32) + b_ref[...]


def _ada(c_pad, w_ada, b_ada):
    rows = c_pad.shape[0]
    n = w_ada.shape[-1]
    tn = 1024
    return pl.pallas_call(
        _ada_kernel,
        grid=(DEPTH, n // tn),
        in_specs=[
            pl.BlockSpec((rows, D_MODEL), lambda l, j: (0, 0)),
            pl.BlockSpec((None, D_MODEL, tn), lambda l, j: (l, 0, j)),
            pl.BlockSpec((None, 1, tn), lambda l, j: (l, 0, j)),
        ],
        out_specs=pl.BlockSpec((None, rows, tn), lambda l, j: (l, 0, j)),
        out_shape=jax.ShapeDtypeStruct((DEPTH, rows, n), F32),
        compiler_params=_cparams("arbitrary", "arbitrary"),
        name="ada",
    )(c_pad, w_ada, b_ada.reshape(DEPTH, 1, n))


def _modulated(x_ref, g_ref, sh_ref, sc_ref):
    return _modulate(x_ref[...], g_ref, sh_ref, sc_ref)


def _modulate(x, g_ref, sh_ref, sc_ref):
    tm = x.shape[0]
    y = x * lax.rsqrt(jnp.mean(x * x, axis=-1, keepdims=True) + EPS) * g_ref[...]
    y3 = y.reshape(tm // CHUNK, CHUNK, D_MODEL)
    h = y3 * (1.0 + sc_ref[...]) + sh_ref[...]
    return h.reshape(tm, D_MODEL)


def _norm_mod_kernel(x_ref, g_ref, sh_ref, sc_ref, h_ref):
    h_ref[...] = _modulated(x_ref, g_ref, sh_ref, sc_ref).astype(BF16)


def _split_bf16(a):
    hi = a.astype(BF16)
    lo = (a - hi.astype(F32)).astype(BF16)
    return hi, lo


def _norm_mod_router_kernel(x_ref, g_ref, sh_ref, sc_ref, wr_ref, br_ref, h_ref, route_ref):
    h = _modulated(x_ref, g_ref, sh_ref, sc_ref)
    h_ref[...] = h
    h_hi, h_lo = _split_bf16(h)
    w_hi, w_lo = _split_bf16(wr_ref[...])
    dot = functools.partial(jnp.dot, preferred_element_type=F32)
    logits = (dot(h_hi, w_hi) + (dot(h_hi, w_lo) + dot(h_lo, w_hi)) + dot(h_lo, w_lo)) + br_ref[...]
    lane = lax.broadcasted_iota(jnp.int32, logits.shape, 1).astype(F32)
    lg = jnp.where(lane < N_EXPERTS, logits, -jnp.inf)
    m1 = jnp.max(lg, axis=-1, keepdims=True)
    i1 = jnp.min(jnp.where(lg == m1, lane, float(LANES)), axis=-1, keepdims=True)
    lg2 = jnp.where(lane == i1, -jnp.inf, lg)
    m2 = jnp.max(lg2, axis=-1, keepdims=True)
    i2 = jnp.min(jnp.where(lg2 == m2, lane, float(LANES)), axis=-1, keepdims=True)
    e2 = jnp.exp(m2 - m1)
    den = 1.0 + e2
    route = jnp.where(lane == 0.0, i1, jnp.where(lane == 1.0, i2, jnp.where(lane == 2.0, 1.0 / den, e2 / den)))
    route_ref[...] = jnp.where(lane < 4.0, route, 0.0)


def _split_rows(kernel_fn, pos, head_tiles):
    def kern(*refs):
        before, after = refs[:pos], refs[pos + 2:]
        i = pl.program_id(0)

        @pl.when(i < head_tiles)
        def _():
            kernel_fn(*before, refs[pos], *after)

        @pl.when(i >= head_tiles)
        def _():
            kernel_fn(*before, refs[pos + 1], *after)

    return kern


def _split_specs(block, head_tiles, with_col):
    if with_col:
        return [pl.BlockSpec(block, lambda i, j: (jnp.minimum(i, head_tiles - 1), j)),
                pl.BlockSpec(block, lambda i, j: (jnp.maximum(i - head_tiles, 0), j))]
    return [pl.BlockSpec(block, lambda i: (jnp.minimum(i, head_tiles - 1), 0)),
            pl.BlockSpec(block, lambda i: (jnp.maximum(i - head_tiles, 0), 0))]


def _norm_mod(x, g_all, tab, l, which_shift, which_scale, router=None, x_tail=None):
    tm = 512
    cpt = tm // CHUNK
    in_specs = [
        pl.BlockSpec((tm, D_MODEL), lambda i: (i, 0)),
        pl.BlockSpec((None, 1, D_MODEL), lambda i: (l, 0, 0)),
        _tab_spec(cpt, l, which_shift),
        _tab_spec(cpt, l, which_scale),
    ]
    h_spec = pl.BlockSpec((tm, D_MODEL), lambda i: (i, 0))
    args = [x, g_all.reshape(DEPTH, 1, D_MODEL), tab, tab]
    if x_tail is not None:
        assert router is None and x.shape[0] % tm == 0 and x_tail.shape[0] % tm == 0
        head_tiles = x.shape[0] // tm
        nt = x.shape[0] + x_tail.shape[0]
        return pl.pallas_call(
            _split_rows(_norm_mod_kernel, 0, head_tiles), grid=(nt // tm,),
            in_specs=_split_specs((tm, D_MODEL), head_tiles, False) + in_specs[1:], out_specs=h_spec,
            out_shape=jax.ShapeDtypeStruct((nt, D_MODEL), BF16),
            compiler_params=_cparams("arbitrary"), name="norm_mod",
        )(x, x_tail, *args[1:])
    nt = x.shape[0]
    if router is None:
        return pl.pallas_call(
            _norm_mod_kernel, grid=(nt // tm,), in_specs=in_specs, out_specs=h_spec,
            out_shape=jax.ShapeDtypeStruct((nt, D_MODEL), BF16),
            compiler_params=_cparams("arbitrary"), name="norm_mod",
        )(*args)
    w_r, b_r = router
    w_pad = jnp.zeros((D_MODEL, LANES), F32).at[:, :N_EXPERTS].set(w_r)
    b_pad = jnp.zeros((1, LANES), F32).at[0, :N_EXPERTS].set(b_r)
    in_specs += [
        pl.BlockSpec((D_MODEL, LANES), lambda i: (0, 0)),
        pl.BlockSpec((1, LANES), lambda i: (0, 0)),
    ]
    return pl.pallas_call(
        _norm_mod_router_kernel, grid=(nt // tm,), in_specs=in_specs,
        out_specs=[h_spec, pl.BlockSpec((tm, LANES), lambda i: (i, 0))],
        out_shape=[jax.ShapeDtypeStruct((nt, D_MODEL), F32), jax.ShapeDtypeStruct((nt, LANES), F32)],
        compiler_params=_cparams("arbitrary"), name="norm_mod_router",
    )(*args, w_pad, b_pad)


def _in_proj_kernel(x_ref, w_ref, z_ref, u3_ref, scr_ref):
    j = pl.program_id(1)
    r = jnp.dot(x_ref[...], w_ref[...].astype(BF16), preferred_element_type=F32)
    z_ref[...] = r
    tm, tn = r.shape
    u_first = 3 * ATT_W // tn
    for part in range(SSM_W // tn):
        @pl.when(j == u_first + part)
        def _():
            for c in range(tn // LANES):
                scr_ref[c] = r[:, c * LANES:(c + 1) * LANES]
            for s in range(SSM_T):
                for c in range(tn // LANES):
                    lo = part * tn + c * LANES
                    rows = scr_ref[c, pl.ds(s, tm // SSM_T, stride=SSM_T), :]
                    u3_ref[s, :, lo:lo + LANES] = rows.astype(BF16)


def _in_proj(h, w_all, l):
    nt, k = h.shape
    n = w_all.shape[-1]
    tm, tn = IN_TM, IN_TN
    return pl.pallas_call(
        _in_proj_kernel,
        grid=(nt // tm, n // tn),
        in_specs=[
            pl.BlockSpec((tm, k), lambda i, j: (i, 0)),
            pl.BlockSpec((None, k, tn), lambda i, j: (l, 0, j)),
        ],
        out_specs=[
            pl.BlockSpec((tm, tn), lambda i, j: (i, j)),
            pl.BlockSpec((SSM_T, tm // SSM_T, SSM_W), lambda i, j: (0, i, 0)),
        ],
        out_shape=[
            jax.ShapeDtypeStruct((nt, n), F32),
            jax.ShapeDtypeStruct((SSM_T, nt // SSM_T, SSM_W), BF16),
        ],
        scratch_shapes=[pltpu.VMEM((tn // LANES, tm, LANES), F32)],
        compiler_params=_cparams("arbitrary", "arbitrary"),
        name="in_proj",
    )(h, w_all)


def _softmax_rows(s):
    m = jnp.max(s, axis=-1, keepdims=True)
    e = jnp.exp(s - m)
    return e / jnp.sum(e, axis=-1, keepdims=True)


def _attn_prompt_kernel(q_ref, k0_ref, k1_ref, k2_ref, v0_ref, v1_ref, v2_ref, b64_ref, o_ref, bias_ref):
    g = pl.program_id(1)

    @pl.when((pl.program_id(0) == 0) & (g == 0))
    def _():
        bias_ref[...] = jnp.full(bias_ref.shape, NEG_INF, F32)
        for c in range(ATT_GROUP):
            bias_ref[:, c * CHUNK:(c + 1) * CHUNK, c * CHUNK:c * CHUNK + BAND] = b64_ref[...]

    col = lax.broadcasted_iota(jnp.int32, (ATT_ROWS, ATT_KEYS), 1)
    visible = (g * ATT_ROWS - 2 * ATT_ROWS + col) >= 0
    for h in range(N_HEADS):
        sl = slice(h * HEAD_DIM, (h + 1) * HEAD_DIM)
        qh = q_ref[:, sl].astype(BF16)
        kh = jnp.concatenate([k0_ref[:, sl], k1_ref[:, sl], k2_ref[:, sl]], axis=0).astype(BF16)
        vh = jnp.concatenate([v0_ref[:, sl], v1_ref[:, sl], v2_ref[:, sl]], axis=0).astype(BF16)
        s = lax.dot_general(qh, kh, (((1,), (1,)), ((), ())), preferred_element_type=F32)
        s = s * ATT_SCALE + bias_ref[h]
        s = jnp.where(visible, s, NEG_INF)
        p = _softmax_rows(s).astype(BF16)
        o_ref[:, sl] = jnp.dot(p, vh, preferred_element_type=F32).astype(o_ref.dtype)


def _attn_prompt(z, b64, n_batch, seq):
    groups = seq // ATT_ROWS
    blk = (ATT_ROWS, ATT_W)

    def kv_spec(col, back):
        return pl.BlockSpec(blk, lambda b, g: (b * groups + jnp.maximum(g - back, 0), col))

    return pl.pallas_call(
        _attn_prompt_kernel,
        grid=(n_batch, groups),
        in_specs=[
            pl.BlockSpec(blk, lambda b, g: (b * groups + g, 0)),
            kv_spec(1, 2), kv_spec(1, 1), kv_spec(1, 0),
            kv_spec(2, 2), kv_spec(2, 1), kv_spec(2, 0),
            pl.BlockSpec((N_HEADS, CHUNK, BAND), lambda b, g: (0, 0, 0)),
        ],
        out_specs=pl.BlockSpec(blk, lambda b, g: (b * groups + g, 0)),
        out_shape=jax.ShapeDtypeStruct((n_batch * seq, ATT_W), BF16),
        scratch_shapes=[pltpu.VMEM((N_HEADS, ATT_ROWS, ATT_KEYS), F32)],
        compiler_params=_cparams("arbitrary", "arbitrary"),
        name="attn_prompt",
    )(z, z, z, z, z, z, z, b64)


def _attn_sample_kernel(q_ref, kn_ref, vn_ref, kc_ref, vc_ref, bias_ref, o_ref):
    for h in range(N_HEADS):
        sl = slice(h * HEAD_DIM, (h + 1) * HEAD_DIM)
        qh = q_ref[:, sl].astype(BF16)
        kh = jnp.concatenate([kc_ref[:, h, :], kn_ref[:, sl]], axis=0).astype(BF16)
        vh = jnp.concatenate([vc_ref[:, h, :], vn_ref[:, sl]], axis=0).astype(BF16)
        s = lax.dot_general(qh, kh, (((1,), (1,)), ((), ())), preferred_element_type=F32)
        s = s * ATT_SCALE + bias_ref[h]
        p = _softmax_rows(s).astype(BF16)
        o_ref[:, sl] = jnp.dot(p, vh, preferred_element_type=F32).astype(o_ref.dtype)


def _attn_sample(z, cache_k, cache_v, l, b64, row0, n_batch):
    base = row0 // CHUNK
    blk = (CHUNK, ATT_W)
    cache_spec = pl.BlockSpec((None, None, PAST_WINDOW, N_HEADS, HEAD_DIM), lambda b: (l, b, 0, 0, 0))
    return pl.pallas_call(
        _attn_sample_kernel,
        grid=(n_batch,),
        in_specs=[
            pl.BlockSpec(blk, lambda b: (base + b, 0)),
            pl.BlockSpec(blk, lambda b: (base + b, 1)),
            pl.BlockSpec(blk, lambda b: (base + b, 2)),
            cache_spec, cache_spec,
            pl.BlockSpec((N_HEADS, CHUNK, BAND), lambda b: (0, 0, 0)),
        ],
        out_specs=pl.BlockSpec(blk, lambda b: (b, 0)),
        out_shape=jax.ShapeDtypeStruct((n_batch * CHUNK, ATT_W), BF16),
        compiler_params=_cparams("arbitrary"),
        name="attn_sample",
    )(z, z, z, cache_k, cache_v, b64)


def _attention_bias(rel_bias):
    rev = rel_bias[::-1].astype(F32)
    flat = PAST_WINDOW - REL_CLIP + CHUNK - 1
    vec = jnp.concatenate([jnp.broadcast_to(rev[:1], (flat, N_HEADS)), rev[:BAND + CHUNK - 1 - flat]], axis=0)
    rows = [vec[CHUNK - 1 - qi:CHUNK - 1 - qi + BAND] for qi in range(CHUNK)]
    return jnp.transpose(jnp.stack(rows), (2, 0, 1))


def _octet_lhs(u3_ref):
    return jnp.concatenate([u3_ref[s] for s in range(SSM_T)], axis=1)


def _group_mask(rows, cols, row_shift, col_shift):
    r = lax.shift_right_logical(lax.broadcasted_iota(jnp.int32, (rows, cols), 0), row_shift)
    c = lax.shift_right_logical(lax.broadcasted_iota(jnp.int32, (rows, cols), 1), col_shift)
    return r == c


def _blockdiag(x, rep_ref, mask):
    e = jnp.dot(x, rep_ref[...], preferred_element_type=F32)
    return jnp.where(mask, e, 0.0).astype(BF16)


def _ssm_state_kernel(u3_ref, mre_ref, mim_ref, rep_ref, sre_ref, sim_ref, w_ref):
    mask = _group_mask(LANES, OCT_STATE, GROUP_SHIFT, STATE_SHIFT)
    for s in range(SSM_T):
        rows = slice(s * LANES, (s + 1) * LANES)
        w_ref[rows, :OCT_STATE] = _blockdiag(mre_ref[s], rep_ref, mask)
        w_ref[rows, OCT_STATE:] = _blockdiag(mim_ref[s], rep_ref, mask)
    s = jnp.dot(_octet_lhs(u3_ref), w_ref[...], preferred_element_type=F32)
    sre_ref[...] = s[:, :OCT_STATE]
    sim_ref[...] = s[:, OCT_STATE:]


def _ssm_state(u3, m_re, m_im, rep_state):
    n_sub = u3.shape[1]
    out = jax.ShapeDtypeStruct((N_OCT, n_sub, OCT_STATE), F32)
    m_spec = pl.BlockSpec((None, SSM_T, LANES, SSM_STATE), lambda j: (j, 0, 0, 0))
    return pl.pallas_call(
        _ssm_state_kernel,
        grid=(N_OCT,),
        in_specs=[
            pl.BlockSpec((SSM_T, n_sub, LANES), lambda j: (0, 0, j)),
            m_spec, m_spec,
            pl.BlockSpec((SSM_STATE, OCT_STATE), lambda j: (0, 0)),
        ],
        out_specs=[pl.BlockSpec((None, n_sub, OCT_STATE), lambda j: (j, 0, 0))] * 2,
        out_shape=[out, out],
        scratch_shapes=[pltpu.VMEM((OCT_K, 2 * OCT_STATE), BF16)],
        compiler_params=_cparams("arbitrary"),
        name="ssm_state",
    )(u3, m_re, m_im, rep_state)


def _ssm_carry_kernel(first_ref, seq_ref, sre_ref, sim_ref, ar_ref, ai_ref, h0r_ref, h0i_ref,
                      xsr_ref, xsi_ref, xer_ref, xei_ref, st_ref):
    blk = pl.program_id(0)
    steps = sre_ref.shape[1]

    @pl.when(blk == 0)
    def _():
        st_ref[...] = jnp.zeros_like(st_ref)

    ar = ar_ref[...]
    ai = ai_ref[...]

    def body(kk, carry):
        k = blk * steps + kk
        is_first = first_ref[k] == 1
        sq = seq_ref[k]
        xr = jnp.where(is_first, h0r_ref[sq], st_ref[0])
        xi = jnp.where(is_first, h0i_ref[sq], st_ref[1])
        xsr_ref[:, kk, :] = xr
        xsi_ref[:, kk, :] = xi
        nr = ar * xr - ai * xi + sre_ref[:, kk, :]
        ni = ar * xi + ai * xr + sim_ref[:, kk, :]
        xer_ref[:, kk, :] = nr
        xei_ref[:, kk, :] = ni
        st_ref[0] = nr
        st_ref[1] = ni
        return carry

    lax.fori_loop(0, steps, body, 0)


def _ssm_carry(first, seq_id, sre, sim, at_re, at_im, h0_re, h0_im):
    n_sub = sre.shape[1]
    steps = 64
    tile = (N_OCT, OCT_STATE)
    blk = pl.BlockSpec((N_OCT, steps, OCT_STATE), lambda i, *_: (0, i, 0))
    const = pl.BlockSpec(tile, lambda i, *_: (0, 0))
    tab = pl.BlockSpec((h0_re.shape[0],) + tile, lambda i, *_: (0, 0, 0))
    out = jax.ShapeDtypeStruct((N_OCT, n_sub, OCT_STATE), F32)
    return pl.pallas_call(
        _ssm_carry_kernel,
        grid_spec=pltpu.PrefetchScalarGridSpec(
            num_scalar_prefetch=2,
            grid=(n_sub // steps,),
            in_specs=[blk, blk, const, const, tab, tab],
            out_specs=[blk, blk, blk, blk],
            scratch_shapes=[pltpu.VMEM((2,) + tile, F32)],
        ),
        out_shape=[out, out, out, out],
        compiler_params=_cparams("arbitrary"),
        name="ssm_carry",
    )(first, seq_id, sre, sim, at_re.reshape(tile), at_im.reshape(tile),
      h0_re.reshape((-1,) + tile), h0_im.reshape((-1,) + tile))


def _ssm_out_kernel(u3_ref, taps_ref, xr_ref, xi_ref, ccr_ref, cci_ref, rep_ref, y3_ref, kt_ref, cr_ref, ci_ref):
    @pl.when(pl.program_id(0) == 0)
    def _():
        kt_ref[...] = jnp.zeros_like(kt_ref)

    tap_mask = _group_mask(LANES, LANES, GROUP_SHIFT, GROUP_SHIFT)
    for tau in range(SSM_T):
        block = _blockdiag(taps_ref[tau], rep_ref, tap_mask)
        for s in range(SSM_T - tau):
            t = s + tau
            kt_ref[s * LANES:(s + 1) * LANES, t * LANES:(t + 1) * LANES] = block
    c_mask = _group_mask(OCT_STATE, LANES, STATE_SHIFT, GROUP_SHIFT)
    for t in range(SSM_T):
        cols = slice(t * LANES, (t + 1) * LANES)
        cr_ref[:, cols] = _blockdiag(ccr_ref[t], rep_ref, c_mask)
        ci_ref[:, cols] = _blockdiag(cci_ref[t], rep_ref, c_mask)

    y = jnp.dot(_octet_lhs(u3_ref), kt_ref[...], preferred_element_type=F32)
    y += jnp.dot(xr_ref[...].astype(BF16), cr_ref[...], preferred_element_type=F32)
    y += jnp.dot(xi_ref[...].astype(BF16), ci_ref[...], preferred_element_type=F32)
    for t in range(SSM_T):
        y3_ref[t] = y[:, t * LANES:(t + 1) * LANES]


def _ssm_out(u3, taps, xs_re, xs_im, c_re, c_im, rep_group):
    n_sub = u3.shape[1]
    u_spec = pl.BlockSpec((SSM_T, n_sub, LANES), lambda j: (0, 0, j))
    x_spec = pl.BlockSpec((None, n_sub, OCT_STATE), lambda j: (j, 0, 0))
    c_spec = pl.BlockSpec((None, SSM_T, OCT_STATE, SSM_GROUP), lambda j: (j, 0, 0, 0))
    return pl.pallas_call(
        _ssm_out_kernel,
        grid=(N_OCT,),
        in_specs=[u_spec, pl.BlockSpec((None, SSM_T, LANES, SSM_GROUP), lambda j: (j, 0, 0, 0)),
                  x_spec, x_spec, c_spec, c_spec, pl.BlockSpec((SSM_GROUP, LANES), lambda j: (0, 0))],
        out_specs=u_spec,
        out_shape=jax.ShapeDtypeStruct((SSM_T, n_sub, SSM_W), F32),
        scratch_shapes=[pltpu.VMEM((OCT_K, OCT_K), BF16), pltpu.VMEM((OCT_STATE, OCT_K), BF16),
                        pltpu.VMEM((OCT_STATE, OCT_K), BF16)],
        compiler_params=_cparams("arbitrary"),
        name="ssm_out",
    )(u3, taps, xs_re, xs_im, c_re, c_im, rep_group)


def _ssm_matrices(a_re, a_im, log_step, b_re, b_im, c_re, c_im):
    hp = lax.Precision.HIGHEST
    G, P, T, O = SSM_GROUPS, SSM_STATE, SSM_T, SSM_OCT
    step = jnp.exp(log_step.astype(F32))[:, None]
    mag = jnp.exp(a_re * step)
    ang = a_im * step
    ab_re = mag * jnp.cos(ang)
    ab_im = mag * jnp.sin(ang)
    den = a_re * a_re + a_im * a_im
    n_re = ab_re - 1.0
    f_re = (n_re * a_re + ab_im * a_im) / den
    f_im = (ab_im * a_re - n_re * a_im) / den
    bb_re = f_re[..., None] * b_re - f_im[..., None] * b_im
    bb_im = f_re[..., None] * b_im + f_im[..., None] * b_re
    pr, pi = [jnp.ones_like(ab_re)], [jnp.zeros_like(ab_re)]
    for _ in range(T):
        pr.append(pr[-1] * ab_re - pi[-1] * ab_im)
        pi.append(pr[-2] * ab_im + pi[-1] * ab_re)
    pw_re = jnp.stack(pr)
    pw_im = jnp.stack(pi)
    ab_r = pw_re[:T, :, :, None] * bb_re[None] - pw_im[:T, :, :, None] * bb_im[None]
    ab_i = pw_re[:T, :, :, None] * bb_im[None] + pw_im[:T, :, :, None] * bb_re[None]
    taps = (jnp.einsum('gcp,tgpd->tgcd', c_re, ab_r, precision=hp)
            - jnp.einsum('gcp,tgpd->tgcd', c_im, ab_i, precision=hp))

    def per_octet(x, perm, rows, cols):
        return jnp.transpose(x, perm).reshape(N_OCT, T, rows, cols).astype(BF16)

    tap_rows = per_octet(taps.reshape(T, N_OCT, O, SSM_GROUP, SSM_GROUP), (1, 0, 2, 4, 3), LANES, SSM_GROUP)

    def inject(m):
        return per_octet(m[::-1].reshape(T, N_OCT, O, P, SSM_GROUP), (1, 0, 2, 4, 3), LANES, P)

    def readout(ca):
        return per_octet(ca.reshape(T, N_OCT, O, SSM_GROUP, P), (1, 0, 2, 4, 3), OCT_STATE, SSM_GROUP)

    ca_re = c_re[None] * pw_re[1:][:, :, None, :] - c_im[None] * pw_im[1:][:, :, None, :]
    ca_im = c_re[None] * pw_im[1:][:, :, None, :] + c_im[None] * pw_re[1:][:, :, None, :]
    return dict(taps=tap_rows, m_re=inject(ab_r), m_im=inject(ab_i),
                c_re=readout(ca_re), c_im=readout(-ca_im),
                rep_group=jnp.tile(jnp.eye(SSM_GROUP, dtype=BF16), (1, O)),
                rep_state=jnp.tile(jnp.eye(P, dtype=BF16), (1, O)),
                at_re=pw_re[T], at_im=pw_im[T])


def _ssm(u3, mats, first, seq_id, h0_re, h0_im):
    sre, sim = _ssm_state(u3, mats['m_re'], mats['m_im'], mats['rep_state'])
    xsr, xsi, xer, xei = _ssm_carry(first, seq_id, sre, sim, mats['at_re'], mats['at_im'], h0_re, h0_im)
    y3 = _ssm_out(u3, mats['taps'], xsr, xsi, mats['c_re'], mats['c_im'], mats['rep_group'])
    return y3, xer, xei


def _states_at(xe, idx):
    return jnp.transpose(xe[:, idx, :], (1, 0, 2)).reshape(idx.shape[0], SSM_GROUPS, SSM_STATE)


def _gelu_tanh(x):
    return 0.5 * x * (1.0 + jnp.tanh(math.sqrt(2.0 / math.pi) * (x + 0.044715 * (x * x * x))))


def _glu_kernel(y3_ref, u_ref, d_ref, w_ref, o_ref, wbf_ref, scr_ref):
    @pl.when(pl.program_id(0) == 0)
    def _():
        wbf_ref[...] = w_ref[...].astype(BF16)

    rows = y3_ref.shape[1]
    n_col = SSM_W // LANES
    for t in range(SSM_T):
        for c in range(n_col):
            scr_ref[c, pl.ds(t, rows, stride=SSM_T), :] = y3_ref[t, :, c * LANES:(c + 1) * LANES]
    y = jnp.concatenate([scr_ref[c] for c in range(n_col)], axis=1)
    ys = _gelu_tanh(y + d_ref[...] * u_ref[...])
    t = jnp.dot(ys.astype(BF16), wbf_ref[...], preferred_element_type=F32)
    o_ref[...] = (ys * jax.nn.sigmoid(t)).astype(o_ref.dtype)


def _glu(y3, z, d_all, w_all, l):
    nt = z.shape[0]
    tm = 512
    return pl.pallas_call(
        _glu_kernel,
        grid=(nt // tm,),
        in_specs=[
            pl.BlockSpec((SSM_T, tm // SSM_T, SSM_W), lambda i: (0, i, 0)),
            pl.BlockSpec((tm, SSM_W), lambda i: (i, 3 * ATT_W // SSM_W)),
            pl.BlockSpec((None, 1, SSM_W), lambda i: (l, 0, 0)),
            pl.BlockSpec((None, SSM_W, SSM_W), lambda i: (l, 0, 0)),
        ],
        out_specs=pl.BlockSpec((tm, SSM_W), lambda i: (i, 0)),
        out_shape=jax.ShapeDtypeStruct((nt, SSM_W), BF16),
        scratch_shapes=[pltpu.VMEM((SSM_W, SSM_W), BF16), pltpu.VMEM((SSM_W // LANES, tm, LANES), F32)],
        compiler_params=_cparams("arbitrary"),
        name="glu",
    )(y3, z, d_all.reshape(DEPTH, 1, SSM_W), w_all)


def _merge_kernel(o_ref, y_ref, ga_ref, gs_ref, wa_ref, ws_ref, m_ref):
    a = jnp.dot(o_ref[...], wa_ref[...].astype(BF16), preferred_element_type=F32)
    s = jnp.dot(y_ref[...], ws_ref[...].astype(BF16), preferred_element_type=F32)
    m_ref[...] = (jax.nn.sigmoid(ga_ref[...]) * a + jax.nn.sigmoid(gs_ref[...]) * s).astype(m_ref.dtype)


def _merge(o_p, o_s, ys2, z, w_ba_all, w_bs_all, l):
    nt = ys2.shape[0]
    tm, tn = 1024, 512
    ga0 = (3 * ATT_W + SSM_W) // tn
    gs0 = ga0 + D_MODEL // tn
    p_tiles = o_p.shape[0] // tm
    s_tiles = o_s.shape[0] // tm
    assert p_tiles * tm == o_p.shape[0] and s_tiles * tm == o_s.shape[0]

    def kern(op_ref, os_ref, *rest):
        i = pl.program_id(0)

        @pl.when(i < p_tiles)
        def _():
            _merge_kernel(op_ref, *rest)

        @pl.when(i >= p_tiles)
        def _():
            _merge_kernel(os_ref, *rest)

    return pl.pallas_call(
        kern,
        grid=(nt // tm, D_MODEL // tn),
        in_specs=[
            pl.BlockSpec((tm, ATT_W), lambda i, j: (jnp.minimum(i, p_tiles - 1), 0)),
            pl.BlockSpec((tm, ATT_W), lambda i, j: (jnp.maximum(i - p_tiles, 0), 0)),
            pl.BlockSpec((tm, SSM_W), lambda i, j: (i, 0)),
            pl.BlockSpec((tm, tn), lambda i, j: (i, ga0 + j)),
            pl.BlockSpec((tm, tn), lambda i, j: (i, gs0 + j)),
            pl.BlockSpec((None, ATT_W, tn), lambda i, j: (l, 0, j)),
            pl.BlockSpec((None, SSM_W, tn), lambda i, j: (l, 0, j)),
        ],
        out_specs=pl.BlockSpec((tm, tn), lambda i, j: (i, j)),
        out_shape=jax.ShapeDtypeStruct((nt, D_MODEL), BF16),
        compiler_params=_cparams("arbitrary", "arbitrary"),
        name="merge",
    )(o_p, o_s, ys2, z, z, w_ba_all, w_bs_all)


def _proj_residual_kernel(a_ref, w_ref, x_ref, g_ref, o_ref):
    r = jnp.dot(a_ref[...], w_ref[...].astype(BF16), preferred_element_type=F32)
    tm, tn = r.shape
    r3 = r.reshape(tm // CHUNK, CHUNK, tn) * g_ref[...]
    o_ref[...] = x_ref[...] + r3.reshape(tm, tn)


def _proj_residual(a, w_all, x, tab, l, which_gate, x_tail=None):
    nt, k = a.shape
    tm, tn = 1024, 512
    cpt = tm // CHUNK
    kern = _proj_residual_kernel
    x_specs = [pl.BlockSpec((tm, tn), lambda i, j: (i, j))]
    xs = [x]
    if x_tail is not None:
        assert x.shape[0] % tm == 0 and x_tail.shape[0] % tm == 0 and x.shape[0] + x_tail.shape[0] == nt
        head_tiles = x.shape[0] // tm
        kern = _split_rows(_proj_residual_kernel, 2, head_tiles)
        x_specs = _split_specs((tm, tn), head_tiles, True)
        xs = [x, x_tail]
    return pl.pallas_call(
        kern,
        grid=(nt // tm, D_MODEL // tn),
        in_specs=[
            pl.BlockSpec((tm, k), lambda i, j: (i, 0)),
            pl.BlockSpec((None, k, tn), lambda i, j: (l, 0, j)),
            *x_specs,
            _tab_spec(cpt, l, which_gate, width=tn, col=True),
        ],
        out_specs=pl.BlockSpec((tm, tn), lambda i, j: (i, j)),
        out_shape=jax.ShapeDtypeStruct((nt, D_MODEL), F32),
        compiler_params=_cparams("arbitrary", "arbitrary"),
        name="proj_residual",
    )(a, w_all, *xs, tab)


def _swiglu_step(h, wg_ref, wu_ref, wd_ref):
    hg = jnp.dot(h, wg_ref[...].astype(BF16), preferred_element_type=F32)
    hu = jnp.dot(h, wu_ref[...].astype(BF16), preferred_element_type=F32)
    a = (hg * jax.nn.sigmoid(hg)) * hu
    return jnp.dot(a.astype(BF16), wd_ref[...].astype(BF16), preferred_element_type=F32)


def _ffn_kernel(h_ref, wg_ref, wu_ref, wd_ref, o_ref):
    @pl.when(pl.program_id(1) == 0)
    def _():
        o_ref[...] = jnp.zeros_like(o_ref)

    o_ref[...] += _swiglu_step(h_ref[...], wg_ref, wu_ref, wd_ref)


def _ffn(h, w_gate, w_up, w_down, j):
    nt = h.shape[0]
    d_ff = w_gate.shape[-1]
    tm, tf = FFN_TM, FFN_TF
    return pl.pallas_call(
        _ffn_kernel,
        grid=(nt // tm, d_ff // tf),
        in_specs=[
            pl.BlockSpec((tm, D_MODEL), lambda i, f: (i, 0)),
            pl.BlockSpec((None, D_MODEL, tf), lambda i, f: (j, 0, f)),
            pl.BlockSpec((None, D_MODEL, tf), lambda i, f: (j, 0, f)),
            pl.BlockSpec((None, tf, D_MODEL), lambda i, f: (j, f, 0)),
        ],
        out_specs=pl.BlockSpec((tm, D_MODEL), lambda i, f: (i, 0)),
        out_shape=jax.ShapeDtypeStruct((nt, D_MODEL), F32),
        compiler_params=_cparams("arbitrary", "arbitrary"),
        name="ffn",
    )(h, w_gate, w_up, w_down)


def _moe_plan(route, nt):
    tm = MOE_TM
    n_assign = 2 * nt
    n_tiles = (n_assign + N_EXPERTS * (tm - 1)) // tm
    e_flat = jnp.concatenate([route[:, 0], route[:, 1]]).astype(jnp.int32)
    onehot = (e_flat[:, None] == jnp.arange(N_EXPERTS, dtype=jnp.int32)[None, :]).astype(jnp.int32)
    csum = jnp.cumsum(onehot, axis=0)
    counts = csum[-1]
    rank = jnp.sum((csum - onehot) * onehot, axis=1)
    tiles_per = (counts + tm - 1) // tm
    tile_end = jnp.cumsum(tiles_per)
    offsets = (tile_end - tiles_per) * tm
    dest = jnp.sum(onehot * offsets[None, :], axis=1) + rank
    tidx = jnp.arange(n_tiles, dtype=jnp.int32)
    tile_expert = jnp.minimum(jnp.sum((tidx[:, None] >= tile_end[None, :]).astype(jnp.int32), axis=1), N_EXPERTS - 1)
    tile_valid = (tidx < tile_end[-1]).astype(jnp.int32)
    tok = jnp.concatenate([jnp.arange(nt, dtype=jnp.int32)] * 2)
    src_token = jnp.zeros((n_tiles * tm,), jnp.int32).at[dest].set(tok)
    return dict(dest1=dest[:nt], dest2=dest[nt:], src_token=src_token,
                tile_expert=tile_expert.astype(jnp.int32), tile_valid=tile_valid)


def _row_copy(src_ref, row, dst_ref, r, sem):
    return pltpu.make_async_copy(src_ref.at[pl.ds(row, 1)], dst_ref.at[pl.ds(r, 1)], sem)


def _gather_start(idx_ref, base, src_ref, dst_ref, sems):
    rows = dst_ref.shape[0]

    def group(gi, c):
        for u in range(GATHER_UNROLL):
            r = gi * GATHER_UNROLL + u
            _row_copy(src_ref, idx_ref[base + r], dst_ref, r, sems.at[u % GATHER_STREAMS]).start(priority=u % 2)
        return c

    lax.fori_loop(0, rows // GATHER_UNROLL, group, 0)


def _gather_wait(src_ref, dst_ref, sems):
    share = dst_ref.shape[0] // GATHER_STREAMS
    for k in range(GATHER_STREAMS):
        pltpu.make_async_copy(src_ref.at[pl.ds(0, share)], dst_ref.at[pl.ds(0, share)], sems.at[k]).wait()


def _gather_rows_kernel(idx_ref, live_ref, src_ref, o_ref, buf_ref, sem):
    i = pl.program_id(0)
    n = pl.num_programs(0)
    slot = i % 2
    rows = buf_ref.shape[1]

    @pl.when((i == 0) & (live_ref[0] == 1))
    def _():
        _gather_start(idx_ref, 0, src_ref, buf_ref.at[0], sem.at[0])

    @pl.when((i + 1 < n) & (live_ref[jnp.minimum(i + 1, n - 1)] == 1))
    def _():
        _gather_start(idx_ref, (i + 1) * rows, src_ref, buf_ref.at[1 - slot], sem.at[1 - slot])

    @pl.when(live_ref[i] == 1)
    def _():
        _gather_wait(src_ref, buf_ref.at[slot], sem.at[slot])
        o_ref[...] = buf_ref[slot].astype(o_ref.dtype)

    @pl.when(live_ref[i] == 0)
    def _():
        o_ref[...] = jnp.zeros_like(o_ref)


def _gather_rows(src, idx, live, out_dtype):
    n_out = idx.shape[0]
    width = src.shape[1]
    rows = MOE_TM
    assert n_out == live.shape[0] * rows and rows % GATHER_UNROLL == 0
    return pl.pallas_call(
        _gather_rows_kernel,
        grid_spec=pltpu.PrefetchScalarGridSpec(
            num_scalar_prefetch=2,
            grid=(n_out // rows,),
            in_specs=[pl.BlockSpec(memory_space=pl.ANY)],
            out_specs=pl.BlockSpec((rows, width), lambda i, *_: (i, 0)),
            scratch_shapes=[pltpu.VMEM((2, rows, width), src.dtype), pltpu.SemaphoreType.DMA((2, GATHER_STREAMS))],
        ),
        out_shape=jax.ShapeDtypeStruct((n_out, width), out_dtype),
        compiler_params=_cparams("arbitrary"),
        name="moe_gather",
    )(idx, live, src)


def _moe_ffn_kernel(te_ref, tv_ref, h_ref, wg_ref, wu_ref, wd_ref, o_ref):
    t = pl.program_id(0)

    @pl.when(pl.program_id(1) == 0)
    def _():
        o_ref[...] = jnp.zeros_like(o_ref)

    @pl.when(tv_ref[t] == 1)
    def _():
        o_ref[...] += _swiglu_step(h_ref[...], wg_ref, wu_ref, wd_ref)


def _moe_ffn(h_sorted, tile_expert, tile_valid, w_gate, w_up, w_down, j):
    d_ff = w_gate.shape[-1]
    tm, tf = MOE_TM, FFN_TF
    n_tiles = h_sorted.shape[0] // tm
    nf = d_ff // tf

    def f_idx(t, f, tv):
        return jnp.where(tv[t] == 1, f, nf - 1)

    return pl.pallas_call(
        _moe_ffn_kernel,
        grid_spec=pltpu.PrefetchScalarGridSpec(
            num_scalar_prefetch=2,
            grid=(n_tiles, nf),
            in_specs=[
                pl.BlockSpec((tm, D_MODEL), lambda t, f, te, tv: (t, 0)),
                pl.BlockSpec((None, None, D_MODEL, tf), lambda t, f, te, tv: (j, te[t], 0, f_idx(t, f, tv))),
                pl.BlockSpec((None, None, D_MODEL, tf), lambda t, f, te, tv: (j, te[t], 0, f_idx(t, f, tv))),
                pl.BlockSpec((None, None, tf, D_MODEL), lambda t, f, te, tv: (j, te[t], f_idx(t, f, tv), 0)),
            ],
            out_specs=pl.BlockSpec((tm, D_MODEL), lambda t, f, te, tv: (t, 0)),
        ),
        out_shape=jax.ShapeDtypeStruct((n_tiles * tm, D_MODEL), F32),
        compiler_params=_cparams("arbitrary", "arbitrary"),
        name="moe_ffn",
    )(tile_expert, tile_valid, h_sorted, w_gate, w_up, w_down)


def _moe_combine_kernel(d1_ref, d2_ref, y_ref, x_ref, route_ref, g_ref, gn_ref, *rest, final_norm, split_tiles):
    n_out = 1 if split_tiles is None else 2
    o_refs, (a_ref, b_ref, sem_a, sem_b) = rest[:n_out], rest[n_out:]
    i = pl.program_id(0)
    n = pl.num_programs(0)
    slot = i % 2

    def start(step, s):
        _gather_start(d1_ref, step * GATHER_ROWS, y_ref, a_ref.at[s], sem_a.at[s])
        _gather_start(d2_ref, step * GATHER_ROWS, y_ref, b_ref.at[s], sem_b.at[s])

    @pl.when(i == 0)
    def _():
        start(0, 0)

    @pl.when(i + 1 < n)
    def _():
        start(i + 1, 1 - slot)

    _gather_wait(y_ref, a_ref.at[slot], sem_a.at[slot])
    _gather_wait(y_ref, b_ref.at[slot], sem_b.at[slot])
    route = route_ref[...]
    r = route[:, 2:3] * a_ref[slot] + route[:, 3:4] * b_ref[slot]
    tm, n_col = r.shape
    xn = x_ref[...] + (r.reshape(tm // CHUNK, CHUNK, n_col) * g_ref[...]).reshape(tm, n_col)
    if final_norm:
        xn = xn * lax.rsqrt(jnp.mean(xn * xn, axis=-1, keepdims=True) + EPS) * gn_ref[...]
    if split_tiles is None:
        o_refs[0][...] = xn
    else:
        @pl.when(i < split_tiles)
        def _():
            o_refs[0][...] = xn

        @pl.when(i >= split_tiles)
        def _():
            o_refs[1][...] = xn


def _moe_combine(y_sorted, dest1, dest2, x, route, tab, l, which_gate, g_final=None, split_rows=None):
    final_norm = g_final is not None
    gn = (g_final if final_norm else jnp.ones((D_MODEL,), F32)).reshape(1, D_MODEL)
    nt = x.shape[0]
    tm = GATHER_ROWS
    blk = pl.BlockSpec((tm, D_MODEL), lambda i, *_: (i, 0))
    buf = pltpu.VMEM((2, tm, D_MODEL), F32)
    if split_rows is None:
        split_tiles = None
        out_specs = blk
        out_shape = jax.ShapeDtypeStruct((nt, D_MODEL), F32)
    else:
        split_tiles = split_rows // tm
        assert split_tiles * tm == split_rows and 0 < split_rows < nt
        out_specs = [pl.BlockSpec((tm, D_MODEL), lambda i, *_: (jnp.minimum(i, split_tiles - 1), 0)),
                     pl.BlockSpec((tm, D_MODEL), lambda i, *_: (jnp.maximum(i - split_tiles, 0), 0))]
        out_shape = [jax.ShapeDtypeStruct((split_rows, D_MODEL), F32),
                     jax.ShapeDtypeStruct((nt - split_rows, D_MODEL), F32)]
    return pl.pallas_call(
        functools.partial(_moe_combine_kernel, final_norm=final_norm, split_tiles=split_tiles),
        grid_spec=pltpu.PrefetchScalarGridSpec(
            num_scalar_prefetch=2,
            grid=(nt // tm,),
            in_specs=[pl.BlockSpec(memory_space=pl.ANY), blk, pl.BlockSpec((tm, LANES), lambda i, *_: (i, 0)),
                      _tab_spec(tm // CHUNK, l, which_gate), pl.BlockSpec((1, D_MODEL), lambda i, *_: (0, 0))],
            out_specs=out_specs,
            scratch_shapes=[buf, buf, pltpu.SemaphoreType.DMA((2, GATHER_STREAMS)),
                            pltpu.SemaphoreType.DMA((2, GATHER_STREAMS))],
        ),
        out_shape=out_shape,
        compiler_params=_cparams("arbitrary"),
        name="moe_combine",
    )(dest1, dest2, y_sorted, x, route, tab, gn)


def _residual_kernel(x_ref, r_ref, g_ref, o_ref):
    r = r_ref[...]
    tm, n = r.shape
    o_ref[...] = x_ref[...] + (r.reshape(tm // CHUNK, CHUNK, n) * g_ref[...]).reshape(tm, n)


def _residual(x, r, tab, l, which_gate):
    nt = x.shape[0]
    tm = 512
    blk = pl.BlockSpec((tm, D_MODEL), lambda i: (i, 0))
    return pl.pallas_call(
        _residual_kernel, grid=(nt // tm,),
        in_specs=[blk, blk, _tab_spec(tm // CHUNK, l, which_gate)],
        out_specs=blk, out_shape=jax.ShapeDtypeStruct((nt, D_MODEL), F32),
        compiler_params=_cparams("arbitrary"), name="residual",
    )(x, r, tab)


def _residual_norm_kernel(x_ref, r_ref, gate_ref, g_ref, sh_ref, sc_ref, xo_ref, h_ref):
    r = r_ref[...]
    tm, n = r.shape
    xn = x_ref[...] + (r.reshape(tm // CHUNK, CHUNK, n) * gate_ref[...]).reshape(tm, n)
    xo_ref[...] = xn
    h_ref[...] = _modulate(xn, g_ref, sh_ref, sc_ref).astype(BF16)


def _residual_norm(x, r, tab, l, which_gate, g_all, l_next, which_shift, which_scale):
    nt = x.shape[0]
    tm = 512
    cpt = tm // CHUNK
    blk = pl.BlockSpec((tm, D_MODEL), lambda i: (i, 0))
    return pl.pallas_call(
        _residual_norm_kernel, grid=(nt // tm,),
        in_specs=[blk, blk, _tab_spec(cpt, l, which_gate),
                  pl.BlockSpec((None, 1, D_MODEL), lambda i: (l_next, 0, 0)),
                  _tab_spec(cpt, l_next, which_shift), _tab_spec(cpt, l_next, which_scale)],
        out_specs=[blk, blk],
        out_shape=[jax.ShapeDtypeStruct((nt, D_MODEL), F32), jax.ShapeDtypeStruct((nt, D_MODEL), BF16)],
        compiler_params=_cparams("arbitrary"), name="residual_norm",
    )(x, r, tab, g_all.reshape(DEPTH, 1, D_MODEL), tab, tab)


def _kv_rows_kernel(z_ref, o_ref):
    for h in range(N_HEADS):
        o_ref[:, h, :] = z_ref[:, h * HEAD_DIM:(h + 1) * HEAD_DIM]


def _kv_rows(z, n_pb, seq, n_sb):
    rows = PAST_WINDOW
    per_seq = seq // rows
    s_blocks = n_sb * CHUNK // rows
    assert per_seq * rows == seq and s_blocks * rows == n_sb * CHUNK

    def row_block(r):
        return jnp.where(r < n_pb, (r + 1) * per_seq - 1, n_pb * per_seq + (r - n_pb))

    n_blocks = n_pb + s_blocks
    return pl.pallas_call(
        _kv_rows_kernel, grid=(2, n_blocks),
        in_specs=[pl.BlockSpec((rows, ATT_W), lambda w, r: (row_block(r), 1 + w))],
        out_specs=pl.BlockSpec((None, rows, N_HEADS, HEAD_DIM), lambda w, r: (w, r, 0, 0)),
        out_shape=jax.ShapeDtypeStruct((2, n_blocks * rows, N_HEADS, HEAD_DIM), F32),
        compiler_params=_cparams("arbitrary", "arbitrary"), name="kv_rows",
    )(z)


def _final_norm_kernel(x_ref, g_ref, o_ref):
    x = x_ref[...]
    o_ref[...] = x * lax.rsqrt(jnp.mean(x * x, axis=-1, keepdims=True) + EPS) * g_ref[...]


def _final_norm(x, g):
    nt = x.shape[0]
    tm = 512
    blk = pl.BlockSpec((tm, D_MODEL), lambda i: (i, 0))
    return pl.pallas_call(
        _final_norm_kernel, grid=(nt // tm,),
        in_specs=[blk, pl.BlockSpec((1, D_MODEL), lambda i: (0, 0))],
        out_specs=blk, out_shape=jax.ShapeDtypeStruct((nt, D_MODEL), F32),
        compiler_params=_cparams("arbitrary"), name="final_norm",
    )(x, g.reshape(1, D_MODEL))


def kernel(x_prompt, x_sample, cache_k, cache_v, state_ssm_re, state_ssm_im, c_prompt, c_sample, w_ada, b_ada, g_norm_mix, g_norm_ffn, g_norm_final, w_in, rel_bias, ssm_a_re, ssm_a_im, ssm_log_step, ssm_b_re, ssm_b_im, ssm_c_re, ssm_c_im, ssm_d, w_ssm_glu, w_branch_attn, w_branch_ssm, w_out, w_ffn_gate, w_ffn_up, w_ffn_down, w_router, b_router, w_exp_gate, w_exp_up, w_exp_down):
    n_pb, seq = x_prompt.shape[0], x_prompt.shape[1]
    n_sb, dec = x_sample.shape[0], x_sample.shape[1]
    assert dec == CHUNK and cache_k.shape[2] == PAST_WINDOW and seq % ATT_ROWS == 0
    np_tok = n_pb * seq
    nt = np_tok + n_sb * dec
    n_batch = n_pb + n_sb

    x_head = x_prompt.reshape(np_tok, D_MODEL)
    x_tail = x_sample.reshape(n_sb * dec, D_MODEL)

    c_all = jnp.concatenate([c_prompt, c_sample], axis=0)
    c_pad = jnp.pad(c_all, ((0, -n_batch % 8), (0, 0)))
    ada = _ada(c_pad, w_ada, b_ada)
    ada_p = jnp.broadcast_to(ada[:, :n_pb, None, :], (DEPTH, n_pb, seq // CHUNK, 6 * D_MODEL))
    tab = jnp.concatenate([ada_p.reshape(DEPTH, np_tok // CHUNK, 6 * D_MODEL), ada[:, n_pb:n_batch]], axis=1)
    tab = tab.reshape(DEPTH, nt // CHUNK, 6, 1, D_MODEL)

    sub_p, sub_s = seq // SSM_T, dec // SSM_T
    seq_id = jnp.concatenate([jnp.repeat(jnp.arange(n_pb), sub_p), n_pb + jnp.repeat(jnp.arange(n_sb), sub_s)])
    seq_id = seq_id.astype(jnp.int32)
    first = jnp.concatenate([jnp.arange(n_pb * sub_p) % sub_p == 0, jnp.arange(n_sb * sub_s) % sub_s == 0])
    first = first.astype(jnp.int32)
    last_p = (jnp.arange(n_pb) + 1) * sub_p - 1
    last_s = n_pb * sub_p + (jnp.arange(n_sb) + 1) * sub_s - 1
    zeros_h0 = jnp.zeros((n_pb, SSM_GROUPS, SSM_STATE), F32)

    outs = {k: [] for k in ('kp', 'vp', 'srp', 'sip', 'ks', 'vs', 'srs', 'sis')}
    keep = min(PAST_WINDOW, seq)
    h = _norm_mod(x_head, g_norm_mix, tab, 0, SHIFT1, SCALE1, x_tail=x_tail)
    for l in range(DEPTH):
        last = l == DEPTH - 1
        z, u3 = _in_proj(h, w_in, l)

        b64 = _attention_bias(rel_bias[l])
        o_p = _attn_prompt(z, b64, n_pb, seq)
        o_s = _attn_sample(z, cache_k, cache_v, l, b64, np_tok, n_sb)

        mats = _ssm_matrices(ssm_a_re[l], ssm_a_im[l], ssm_log_step[l], ssm_b_re[l], ssm_b_im[l],
                             ssm_c_re[l], ssm_c_im[l])
        h0_re = jnp.concatenate([zeros_h0, state_ssm_re[l].astype(F32)], axis=0)
        h0_im = jnp.concatenate([zeros_h0, state_ssm_im[l].astype(F32)], axis=0)
        y3, xe_re, xe_im = _ssm(u3, mats, first, seq_id, h0_re, h0_im)

        ys2 = _glu(y3, z, ssm_d, w_ssm_glu, l)
        merged = _merge(o_p, o_s, ys2, z, w_branch_attn, w_branch_ssm, l)
        if l == 0:
            x = _proj_residual(merged, w_out, x_head, tab, l, GATE1, x_tail=x_tail)
        else:
            x = _proj_residual(merged, w_out, x, tab, l, GATE1)

        j = l // 2
        if l % 2 == 0:
            h2 = _norm_mod(x, g_norm_ffn, tab, l, SHIFT2, SCALE2)
            r = _ffn(h2, w_ffn_gate, w_ffn_up, w_ffn_down, j)
            if last:
                yn = _final_norm(_residual(x, r, tab, l, GATE2), g_norm_final)
                yn_p, yn_s = yn[:np_tok], yn[np_tok:]
            else:
                x, h = _residual_norm(x, r, tab, l, GATE2, g_norm_mix, l + 1, SHIFT1, SCALE1)
        else:
            h2, route = _norm_mod(x, g_norm_ffn, tab, l, SHIFT2, SCALE2, router=(w_router[j], b_router[j]))
            plan = _moe_plan(route, nt)
            h_sorted = _gather_rows(h2, plan['src_token'], plan['tile_valid'], BF16)
            y_sorted = _moe_ffn(h_sorted, plan['tile_expert'], plan['tile_valid'], w_exp_gate, w_exp_up, w_exp_down, j)
            if last:
                yn_p, yn_s = _moe_combine(y_sorted, plan['dest1'], plan['dest2'], x, route, tab, l, GATE2,
                                          g_final=g_norm_final, split_rows=np_tok)
            else:
                x = _moe_combine(y_sorted, plan['dest1'], plan['dest2'], x, route, tab, l, GATE2)
                h = _norm_mod(x, g_norm_mix, tab, l + 1, SHIFT1, SCALE1)

        kv = _kv_rows(z, n_pb, seq, n_sb)
        outs['kp'].append(kv[0, :n_pb * keep].reshape(n_pb, keep, N_HEADS, HEAD_DIM))
        outs['vp'].append(kv[1, :n_pb * keep].reshape(n_pb, keep, N_HEADS, HEAD_DIM))
        outs['ks'].append(kv[0, n_pb * keep:].reshape(n_sb, dec, N_HEADS, HEAD_DIM))
        outs['vs'].append(kv[1, n_pb * keep:].reshape(n_sb, dec, N_HEADS, HEAD_DIM))
        outs['srp'].append(_states_at(xe_re, last_p))
        outs['sip'].append(_states_at(xe_im, last_p))
        outs['srs'].append(_states_at(xe_re, last_s))
        outs['sis'].append(_states_at(xe_im, last_s))

    y_prompt = yn_p.reshape(n_pb, seq, D_MODEL)
    y_sample = yn_s.reshape(n_sb, dec, D_MODEL)
    st = lambda name: jnp.stack(outs[name])
    return (y_prompt, y_sample, st('kp'), st('vp'), st('srp'), st('sip'),
            st('ks'), st('vs'), st('srs'), st('sis'))
```

```python
import functools
import math

import jax
import jax.numpy as jnp
from jax import lax
from jax.experimental import pallas as pl
from jax.experimental.pallas import tpu as pltpu

F32 = jnp.float32
BF16 = jnp.bfloat16

D_MODEL = 2048
DEPTH = 2
CHUNK = 64
PAST_CHUNKS = 8
PAST_WINDOW = PAST_CHUNKS * CHUNK
BAND = PAST_WINDOW + CHUNK
N_HEADS = 8
HEAD_DIM = 128
ATT_W = N_HEADS * HEAD_DIM
ATT_SCALE = HEAD_DIM ** -0.5
REL_CLIP = 256
SSM_W = 1024
SSM_GROUP = 16
SSM_GROUPS = SSM_W // SSM_GROUP
SSM_STATE = 64
N_EXPERTS = 8
IN_COLS = 3 * ATT_W + SSM_W + 2 * D_MODEL
EPS = 1e-6
NEG_INF = -1e30

LANES = 128
VMEM_LIMIT_BYTES = 56 * 1024 * 1024

SSM_T = 16
SSM_OCT = LANES // SSM_GROUP
N_OCT = SSM_GROUPS // SSM_OCT
OCT_K = SSM_T * LANES
OCT_STATE = SSM_OCT * SSM_STATE
GROUP_SHIFT = SSM_GROUP.bit_length() - 1
STATE_SHIFT = SSM_STATE.bit_length() - 1
assert (1 << GROUP_SHIFT, 1 << STATE_SHIFT) == (SSM_GROUP, SSM_STATE)
ATT_GROUP = 4
ATT_ROWS = ATT_GROUP * CHUNK
ATT_KEYS = 3 * ATT_ROWS

IN_TM, IN_TN = 1536, 512
FFN_TM = 768
MOE_TM = 768
FFN_TF = 512
GATHER_ROWS = 512
GATHER_UNROLL = 8

SHIFT1, SCALE1, GATE1, SHIFT2, SCALE2, GATE2 = range(6)


def _cparams(*sem):
    return pltpu.CompilerParams(dimension_semantics=sem, vmem_limit_bytes=VMEM_LIMIT_BYTES)


def _tab_spec(cpt, l, which, width=D_MODEL, col=None):
    per = D_MODEL // width
    if col is None:
        return pl.BlockSpec((None, cpt, width), lambda i, *_: (l, i, which * per))
    return pl.BlockSpec((None, cpt, width), lambda i, j, *_: (l, i, which * per + j))


def _chunk_rows(v, fn):
    return jnp.concatenate([fn(v[c * CHUNK:(c + 1) * CHUNK], c) for c in range(v.shape[0] // CHUNK)], axis=0)


def _gated(r, g_ref):
    return _chunk_rows(r, lambda rows, c: rows * g_ref[c:c + 1, :])


def _ada_kernel(c_ref, w_ref, b_ref, o_ref):
    c = c_ref[...]
    a = (c * jax.nn.sigmoid(c)).astype(BF16)
    o_ref[...] = jnp.dot(a, w_ref[...].astype(BF16), preferred_element_type=F32) + b_ref[...]


def _ada(c_pad, w_ada, b_ada):
    rows = c_pad.shape[0]
    n = w_ada.shape[-1]
    tn = 1024
    return pl.pallas_call(
        _ada_kernel,
        grid=(DEPTH, n // tn),
        in_specs=[
            pl.BlockSpec((rows, D_MODEL), lambda l, j: (0, 0)),
            pl.BlockSpec((None, D_MODEL, tn), lambda l, j: (l, 0, j)),
            pl.BlockSpec((None, 1, tn), lambda l, j: (l, 0, j)),
        ],
        out_specs=pl.BlockSpec((None, rows, tn), lambda l, j: (l, 0, j)),
        out_shape=jax.ShapeDtypeStruct((DEPTH, rows, n), F32),
        compiler_params=_cparams("arbitrary", "arbitrary"),
        name="ada",
    )(c_pad, w_ada, b_ada.reshape(DEPTH, 1, n))


def _modulated(x_ref, g_ref, sh_ref, sc_ref):
    return _modulate(x_ref[...], g_ref, sh_ref, sc_ref)


def _modulate(x, g_ref, sh_ref, sc_ref):
    y = x * lax.rsqrt(jnp.mean(x * x, axis=-1, keepdims=True) + EPS) * g_ref[...]
    return _chunk_rows(y, lambda rows, c: rows * (1.0 + sc_ref[c:c + 1, :]) + sh_ref[c:c + 1, :])


def _norm_mod_kernel(x_ref, g_ref, sh_ref, sc_ref, h_ref):
    h_ref[...] = _modulated(x_ref, g_ref, sh_ref, sc_ref).astype(BF16)


def _split_bf16(a):
    hi = a.astype(BF16)
    lo = (a - hi.astype(F32)).astype(BF16)
    return hi, lo


def _norm_mod_router_kernel(x_ref, g_ref, sh_ref, sc_ref, wr_ref, br_ref, h_ref, route_ref):
    h = _modulated(x_ref, g_ref, sh_ref, sc_ref)
    h_ref[...] = h
    h_hi, h_lo = _split_bf16(h)
    w_hi, w_lo = _split_bf16(wr_ref[...])
    dot = functools.partial(jnp.dot, preferred_element_type=F32)
    logits = (dot(h_hi, w_hi) + (dot(h_hi, w_lo) + dot(h_lo, w_hi)) + dot(h_lo, w_lo)) + br_ref[...]
    lane = lax.broadcasted_iota(jnp.int32, logits.shape, 1).astype(F32)
    lg = jnp.where(lane < N_EXPERTS, logits, -jnp.inf)
    m1 = jnp.max(lg, axis=-1, keepdims=True)
    i1 = jnp.min(jnp.where(lg == m1, lane, float(LANES)), axis=-1, keepdims=True)
    lg2 = jnp.where(lane == i1, -jnp.inf, lg)
    m2 = jnp.max(lg2, axis=-1, keepdims=True)
    i2 = jnp.min(jnp.where(lg2 == m2, lane, float(LANES)), axis=-1, keepdims=True)
    e2 = jnp.exp(m2 - m1)
    den = 1.0 + e2
    route = jnp.where(lane == 0.0, i1, jnp.where(lane == 1.0, i2, jnp.where(lane == 2.0, 1.0 / den, e2 / den)))
    route_ref[...] = jnp.where(lane < 4.0, route, 0.0)


def _split_rows(kernel_fn, pos, head_tiles):
    def kern(*refs):
        before, after = refs[:pos], refs[pos + 2:]
        i = pl.program_id(0)

        @pl.when(i < head_tiles)
        def _():
            kernel_fn(*before, refs[pos], *after)

        @pl.when(i >= head_tiles)
        def _():
            kernel_fn(*before, refs[pos + 1], *after)

    return kern


def _split_specs(block, head_tiles, with_col):
    if with_col:
        return [pl.BlockSpec(block, lambda i, j: (jnp.minimum(i, head_tiles - 1), j)),
                pl.BlockSpec(block, lambda i, j: (jnp.maximum(i - head_tiles, 0), j))]
    return [pl.BlockSpec(block, lambda i: (jnp.minimum(i, head_tiles - 1), 0)),
            pl.BlockSpec(block, lambda i: (jnp.maximum(i - head_tiles, 0), 0))]


def _norm_mod(x, g_all, tab, l, which_shift, which_scale, router=None, x_tail=None):
    tm = 512
    cpt = tm // CHUNK
    in_specs = [
        pl.BlockSpec((tm, D_MODEL), lambda i: (i, 0)),
        pl.BlockSpec((None, 1, D_MODEL), lambda i: (l, 0, 0)),
        _tab_spec(cpt, l, which_shift),
        _tab_spec(cpt, l, which_scale),
    ]
    h_spec = pl.BlockSpec((tm, D_MODEL), lambda i: (i, 0))
    args = [x, g_all.reshape(DEPTH, 1, D_MODEL), tab, tab]
    if x_tail is not None:
        assert router is None and x.shape[0] % tm == 0 and x_tail.shape[0] % tm == 0
        head_tiles = x.shape[0] // tm
        nt = x.shape[0] + x_tail.shape[0]
        return pl.pallas_call(
            _split_rows(_norm_mod_kernel, 0, head_tiles), grid=(nt // tm,),
            in_specs=_split_specs((tm, D_MODEL), head_tiles, False) + in_specs[1:], out_specs=h_spec,
            out_shape=jax.ShapeDtypeStruct((nt, D_MODEL), BF16),
            compiler_params=_cparams("arbitrary"), name="norm_mod",
        )(x, x_tail, *args[1:])
    nt = x.shape[0]
    if router is None:
        return pl.pallas_call(
            _norm_mod_kernel, grid=(nt // tm,), in_specs=in_specs, out_specs=h_spec,
            out_shape=jax.ShapeDtypeStruct((nt, D_MODEL), BF16),
            compiler_params=_cparams("arbitrary"), name="norm_mod",
        )(*args)
    w_r, b_r = router
    w_pad = jnp.zeros((D_MODEL, LANES), F32).at[:, :N_EXPERTS].set(w_r)
    b_pad = jnp.zeros((1, LANES), F32).at[0, :N_EXPERTS].set(b_r)
    in_specs += [
        pl.BlockSpec((D_MODEL, LANES), lambda i: (0, 0)),
        pl.BlockSpec((1, LANES), lambda i: (0, 0)),
    ]
    return pl.pallas_call(
        _norm_mod_router_kernel, grid=(nt // tm,), in_specs=in_specs,
        out_specs=[h_spec, pl.BlockSpec((tm, LANES), lambda i: (i, 0))],
        out_shape=[jax.ShapeDtypeStruct((nt, D_MODEL), F32), jax.ShapeDtypeStruct((nt, LANES), F32)],
        compiler_params=_cparams("arbitrary"), name="norm_mod_router",
    )(*args, w_pad, b_pad)


def _in_proj_kernel(x_ref, w_ref, z_ref, u3_ref, scr_ref):
    j = pl.program_id(1)
    r = jnp.dot(x_ref[...], w_ref[...].astype(BF16), preferred_element_type=F32)
    z_ref[...] = r
    tm, tn = r.shape
    u_first = 3 * ATT_W // tn
    for part in range(SSM_W // tn):
        @pl.when(j == u_first + part)
        def _():
            for c in range(tn // LANES):
                scr_ref[c] = r[:, c * LANES:(c + 1) * LANES]
            for s in range(SSM_T):
                for c in range(tn // LANES):
                    lo = part * tn + c * LANES
                    rows = scr_ref[c, pl.ds(s, tm // SSM_T, stride=SSM_T), :]
                    u3_ref[s, :, lo:lo + LANES] = rows.astype(BF16)


def _in_proj(h, w_all, l):
    nt, k = h.shape
    n = w_all.shape[-1]
    tm, tn = IN_TM, IN_TN
    return pl.pallas_call(
        _in_proj_kernel,
        grid=(nt // tm, n // tn),
        in_specs=[
            pl.BlockSpec((tm, k), lambda i, j: (i, 0)),
            pl.BlockSpec((None, k, tn), lambda i, j: (l, 0, j)),
        ],
        out_specs=[
            pl.BlockSpec((tm, tn), lambda i, j: (i, j)),
            pl.BlockSpec((SSM_T, tm // SSM_T, SSM_W), lambda i, j: (0, i, 0)),
        ],
        out_shape=[
            jax.ShapeDtypeStruct((nt, n), F32),
            jax.ShapeDtypeStruct((SSM_T, nt // SSM_T, SSM_W), BF16),
        ],
        scratch_shapes=[pltpu.VMEM((tn // LANES, tm, LANES), F32)],
        compiler_params=_cparams("arbitrary", "arbitrary"),
        name="in_proj",
    )(h, w_all)


def _softmax_rows(s):
    m = jnp.max(s, axis=-1, keepdims=True)
    e = jnp.exp(s - m)
    return e / jnp.sum(e, axis=-1, keepdims=True)


def _attn_prompt_kernel(q_ref, k0_ref, k1_ref, k2_ref, v0_ref, v1_ref, v2_ref, b64_ref, o_ref, bias_ref):
    g = pl.program_id(1)

    @pl.when((pl.program_id(0) == 0) & (g == 0))
    def _():
        bias_ref[...] = jnp.full(bias_ref.shape, NEG_INF, F32)
        for c in range(ATT_GROUP):
            bias_ref[:, c * CHUNK:(c + 1) * CHUNK, c * CHUNK:c * CHUNK + BAND] = b64_ref[...]

    col = lax.broadcasted_iota(jnp.int32, (ATT_ROWS, ATT_KEYS), 1)
    visible = (g * ATT_ROWS - 2 * ATT_ROWS + col) >= 0
    for h in range(N_HEADS):
        sl = slice(h * HEAD_DIM, (h + 1) * HEAD_DIM)
        qh = q_ref[:, sl].astype(BF16)
        kh = jnp.concatenate([k0_ref[:, sl], k1_ref[:, sl], k2_ref[:, sl]], axis=0).astype(BF16)
        vh = jnp.concatenate([v0_ref[:, sl], v1_ref[:, sl], v2_ref[:, sl]], axis=0).astype(BF16)
        s = lax.dot_general(qh, kh, (((1,), (1,)), ((), ())), preferred_element_type=F32)
        s = s * ATT_SCALE + bias_ref[h]
        s = jnp.where(visible, s, NEG_INF)
        p = _softmax_rows(s).astype(BF16)
        o_ref[:, sl] = jnp.dot(p, vh, preferred_element_type=F32).astype(o_ref.dtype)


def _attn_prompt(z, b64, n_batch, seq):
    groups = seq // ATT_ROWS
    blk = (ATT_ROWS, ATT_W)

    def kv_spec(col, back):
        return pl.BlockSpec(blk, lambda b, g: (b * groups + jnp.maximum(g - back, 0), col))

    return pl.pallas_call(
        _attn_prompt_kernel,
        grid=(n_batch, groups),
        in_specs=[
            pl.BlockSpec(blk, lambda b, g: (b * groups + g, 0)),
            kv_spec(1, 2), kv_spec(1, 1), kv_spec(1, 0),
            kv_spec(2, 2), kv_spec(2, 1), kv_spec(2, 0),
            pl.BlockSpec((N_HEADS, CHUNK, BAND), lambda b, g: (0, 0, 0)),
        ],
        out_specs=pl.BlockSpec(blk, lambda b, g: (b * groups + g, 0)),
        out_shape=jax.ShapeDtypeStruct((n_batch * seq, ATT_W), BF16),
        scratch_shapes=[pltpu.VMEM((N_HEADS, ATT_ROWS, ATT_KEYS), F32)],
        compiler_params=_cparams("arbitrary", "arbitrary"),
        name="attn_prompt",
    )(z, z, z, z, z, z, z, b64)


def _attn_sample_kernel(q_ref, kn_ref, vn_ref, kc_ref, vc_ref, bias_ref, o_ref):
    for h in range(N_HEADS):
        sl = slice(h * HEAD_DIM, (h + 1) * HEAD_DIM)
        qh = q_ref[:, sl].astype(BF16)
        kh = jnp.concatenate([kc_ref[:, h, :], kn_ref[:, sl]], axis=0).astype(BF16)
        vh = jnp.concatenate([vc_ref[:, h, :], vn_ref[:, sl]], axis=0).astype(BF16)
        s = lax.dot_general(qh, kh, (((1,), (1,)), ((), ())), preferred_element_type=F32)
        s = s * ATT_SCALE + bias_ref[h]
        p = _softmax_rows(s).astype(BF16)
        o_ref[:, sl] = jnp.dot(p, vh, preferred_element_type=F32).astype(o_ref.dtype)


def _attn_sample(z, cache_k, cache_v, l, b64, row0, n_batch):
    base = row0 // CHUNK
    blk = (CHUNK, ATT_W)
    cache_spec = pl.BlockSpec((None, None, PAST_WINDOW, N_HEADS, HEAD_DIM), lambda b: (l, b, 0, 0, 0))
    return pl.pallas_call(
        _attn_sample_kernel,
        grid=(n_batch,),
        in_specs=[
            pl.BlockSpec(blk, lambda b: (base + b, 0)),
            pl.BlockSpec(blk, lambda b: (base + b, 1)),
            pl.BlockSpec(blk, lambda b: (base + b, 2)),
            cache_spec, cache_spec,
            pl.BlockSpec((N_HEADS, CHUNK, BAND), lambda b: (0, 0, 0)),
        ],
        out_specs=pl.BlockSpec(blk, lambda b: (b, 0)),
        out_shape=jax.ShapeDtypeStruct((n_batch * CHUNK, ATT_W), BF16),
        compiler_params=_cparams("arbitrary"),
        name="attn_sample",
    )(z, z, z, cache_k, cache_v, b64)


def _attention_bias(rel_bias):
    rev = rel_bias[::-1].astype(F32)
    flat = PAST_WINDOW - REL_CLIP + CHUNK - 1
    vec = jnp.concatenate([jnp.broadcast_to(rev[:1], (flat, N_HEADS)), rev[:BAND + CHUNK - 1 - flat]], axis=0)
    rows = [vec[CHUNK - 1 - qi:CHUNK - 1 - qi + BAND] for qi in range(CHUNK)]
    return jnp.transpose(jnp.stack(rows), (2, 0, 1))


def _octet_lhs(u3_ref):
    return jnp.concatenate([u3_ref[s] for s in range(SSM_T)], axis=1)


def _group_mask(rows, cols, row_shift, col_shift):
    r = lax.shift_right_logical(lax.broadcasted_iota(jnp.int32, (rows, cols), 0), row_shift)
    c = lax.shift_right_logical(lax.broadcasted_iota(jnp.int32, (rows, cols), 1), col_shift)
    return r == c


def _blockdiag(x, rep_ref, mask):
    e = jnp.dot(x, rep_ref[...], preferred_element_type=F32)
    return jnp.where(mask, e, 0.0).astype(BF16)


def _ssm_state_kernel(u3_ref, mre_ref, mim_ref, rep_ref, sre_ref, sim_ref, w_ref):
    mask = _group_mask(LANES, OCT_STATE, GROUP_SHIFT, STATE_SHIFT)
    for s in range(SSM_T):
        rows = slice(s * LANES, (s + 1) * LANES)
        w_ref[rows, :OCT_STATE] = _blockdiag(mre_ref[s], rep_ref, mask)
        w_ref[rows, OCT_STATE:] = _blockdiag(mim_ref[s], rep_ref, mask)
    s = jnp.dot(_octet_lhs(u3_ref), w_ref[...], preferred_element_type=F32)
    sre_ref[...] = s[:, :OCT_STATE]
    sim_ref[...] = s[:, OCT_STATE:]


def _ssm_state(u3, m_re, m_im, rep_state):
    n_sub = u3.shape[1]
    out = jax.ShapeDtypeStruct((N_OCT, n_sub, OCT_STATE), F32)
    m_spec = pl.BlockSpec((None, SSM_T, LANES, SSM_STATE), lambda j: (j, 0, 0, 0))
    return pl.pallas_call(
        _ssm_state_kernel,
        grid=(N_OCT,),
        in_specs=[
            pl.BlockSpec((SSM_T, n_sub, LANES), lambda j: (0, 0, j)),
            m_spec, m_spec,
            pl.BlockSpec((SSM_STATE, OCT_STATE), lambda j: (0, 0)),
        ],
        out_specs=[pl.BlockSpec((None, n_sub, OCT_STATE), lambda j: (j, 0, 0))] * 2,
        out_shape=[out, out],
        scratch_shapes=[pltpu.VMEM((OCT_K, 2 * OCT_STATE), BF16)],
        compiler_params=_cparams("arbitrary"),
        name="ssm_state",
    )(u3, m_re, m_im, rep_state)


def _ssm_carry_kernel(first_ref, seq_ref, sre_ref, sim_ref, ar_ref, ai_ref, h0r_ref, h0i_ref,
                      xsr_ref, xsi_ref, xer_ref, xei_ref, st_ref):
    blk = pl.program_id(0)
    steps = sre_ref.shape[1]

    @pl.when(blk == 0)
    def _():
        st_ref[...] = jnp.zeros_like(st_ref)

    ar = ar_ref[...]
    ai = ai_ref[...]

    def body(kk, carry):
        k = blk * steps + kk
        is_first = first_ref[k] == 1
        sq = seq_ref[k]
        xr = jnp.where(is_first, h0r_ref[sq], st_ref[0])
        xi = jnp.where(is_first, h0i_ref[sq], st_ref[1])
        xsr_ref[:, kk, :] = xr
        xsi_ref[:, kk, :] = xi
        nr = ar * xr - ai * xi + sre_ref[:, kk, :]
        ni = ar * xi + ai * xr + sim_ref[:, kk, :]
        xer_ref[:, kk, :] = nr
        xei_ref[:, kk, :] = ni
        st_ref[0] = nr
        st_ref[1] = ni
        return carry

    lax.fori_loop(0, steps, body, 0)


def _ssm_carry(first, seq_id, sre, sim, at_re, at_im, h0_re, h0_im):
    n_sub = sre.shape[1]
    steps = 64
    tile = (N_OCT, OCT_STATE)
    blk = pl.BlockSpec((N_OCT, steps, OCT_STATE), lambda i, *_: (0, i, 0))
    const = pl.BlockSpec(tile, lambda i, *_: (0, 0))
    tab = pl.BlockSpec((h0_re.shape[0],) + tile, lambda i, *_: (0, 0, 0))
    out = jax.ShapeDtypeStruct((N_OCT, n_sub, OCT_STATE), F32)
    return pl.pallas_call(
        _ssm_carry_kernel,
        grid_spec=pltpu.PrefetchScalarGridSpec(
            num_scalar_prefetch=2,
            grid=(n_sub // steps,),
            in_specs=[blk, blk, const, const, tab, tab],
            out_specs=[blk, blk, blk, blk],
            scratch_shapes=[pltpu.VMEM((2,) + tile, F32)],
        ),
        out_shape=[out, out, out, out],
        compiler_params=_cparams("arbitrary"),
        name="ssm_carry",
    )(first, seq_id, sre, sim, at_re.reshape(tile), at_im.reshape(tile),
      h0_re.reshape((-1,) + tile), h0_im.reshape((-1,) + tile))


def _ssm_out_kernel(u3_ref, taps_ref, xr_ref, xi_ref, ccr_ref, cci_ref, rep_ref, y3_ref, kt_ref, cr_ref, ci_ref):
    @pl.when(pl.program_id(0) == 0)
    def _():
        kt_ref[...] = jnp.zeros_like(kt_ref)

    tap_mask = _group_mask(LANES, LANES, GROUP_SHIFT, GROUP_SHIFT)
    for tau in range(SSM_T):
        block = _blockdiag(taps_ref[tau], rep_ref, tap_mask)
        for s in range(SSM_T - tau):
            t = s + tau
            kt_ref[s * LANES:(s + 1) * LANES, t * LANES:(t + 1) * LANES] = block
    c_mask = _group_mask(OCT_STATE, LANES, STATE_SHIFT, GROUP_SHIFT)
    for t in range(SSM_T):
        cols = slice(t * LANES, (t + 1) * LANES)
        cr_ref[:, cols] = _blockdiag(ccr_ref[t], rep_ref, c_mask)
        ci_ref[:, cols] = _blockdiag(cci_ref[t], rep_ref, c_mask)

    y = jnp.dot(_octet_lhs(u3_ref), kt_ref[...], preferred_element_type=F32)
    y += jnp.dot(xr_ref[...].astype(BF16), cr_ref[...], preferred_element_type=F32)
    y += jnp.dot(xi_ref[...].astype(BF16), ci_ref[...], preferred_element_type=F32)
    for t in range(SSM_T):
        y3_ref[t] = y[:, t * LANES:(t + 1) * LANES]


def _ssm_out(u3, taps, xs_re, xs_im, c_re, c_im, rep_group):
    n_sub = u3.shape[1]
    u_spec = pl.BlockSpec((SSM_T, n_sub, LANES), lambda j: (0, 0, j))
    x_spec = pl.BlockSpec((None, n_sub, OCT_STATE), lambda j: (j, 0, 0))
    c_spec = pl.BlockSpec((None, SSM_T, OCT_STATE, SSM_GROUP), lambda j: (j, 0, 0, 0))
    return pl.pallas_call(
        _ssm_out_kernel,
        grid=(N_OCT,),
        in_specs=[u_spec, pl.BlockSpec((None, SSM_T, LANES, SSM_GROUP), lambda j: (j, 0, 0, 0)),
                  x_spec, x_spec, c_spec, c_spec, pl.BlockSpec((SSM_GROUP, LANES), lambda j: (0, 0))],
        out_specs=u_spec,
        out_shape=jax.ShapeDtypeStruct((SSM_T, n_sub, SSM_W), F32),
        scratch_shapes=[pltpu.VMEM((OCT_K, OCT_K), BF16), pltpu.VMEM((OCT_STATE, OCT_K), BF16),
                        pltpu.VMEM((OCT_STATE, OCT_K), BF16)],
        compiler_params=_cparams("arbitrary"),
        name="ssm_out",
    )(u3, taps, xs_re, xs_im, c_re, c_im, rep_group)


def _ssm_matrices(a_re, a_im, log_step, b_re, b_im, c_re, c_im):
    hp = lax.Precision.HIGHEST
    G, P, T, O = SSM_GROUPS, SSM_STATE, SSM_T, SSM_OCT
    step = jnp.exp(log_step.astype(F32))[:, None]
    mag = jnp.exp(a_re * step)
    ang = a_im * step
    ab_re = mag * jnp.cos(ang)
    ab_im = mag * jnp.sin(ang)
    den = a_re * a_re + a_im * a_im
    n_re = ab_re - 1.0
    f_re = (n_re * a_re + ab_im * a_im) / den
    f_im = (ab_im * a_re - n_re * a_im) / den
    bb_re = f_re[..., None] * b_re - f_im[..., None] * b_im
    bb_im = f_re[..., None] * b_im + f_im[..., None] * b_re
    pr, pi = [jnp.ones_like(ab_re)], [jnp.zeros_like(ab_re)]
    for _ in range(T):
        pr.append(pr[-1] * ab_re - pi[-1] * ab_im)
        pi.append(pr[-2] * ab_im + pi[-1] * ab_re)
    pw_re = jnp.stack(pr)
    pw_im = jnp.stack(pi)
    ab_r = pw_re[:T, :, :, None] * bb_re[None] - pw_im[:T, :, :, None] * bb_im[None]
    ab_i = pw_re[:T, :, :, None] * bb_im[None] + pw_im[:T, :, :, None] * bb_re[None]
    taps = (jnp.einsum('gcp,tgpd->tgcd', c_re, ab_r, precision=hp)
            - jnp.einsum('gcp,tgpd->tgcd', c_im, ab_i, precision=hp))

    def per_octet(x, perm, rows, cols):
        return jnp.transpose(x, perm).reshape(N_OCT, T, rows, cols).astype(BF16)

    tap_rows = per_octet(taps.reshape(T, N_OCT, O, SSM_GROUP, SSM_GROUP), (1, 0, 2, 4, 3), LANES, SSM_GROUP)

    def inject(m):
        return per_octet(m[::-1].reshape(T, N_OCT, O, P, SSM_GROUP), (1, 0, 2, 4, 3), LANES, P)

    def readout(ca):
        return per_octet(ca.reshape(T, N_OCT, O, SSM_GROUP, P), (1, 0, 2, 4, 3), OCT_STATE, SSM_GROUP)

    ca_re = c_re[None] * pw_re[1:][:, :, None, :] - c_im[None] * pw_im[1:][:, :, None, :]
    ca_im = c_re[None] * pw_im[1:][:, :, None, :] + c_im[None] * pw_re[1:][:, :, None, :]
    return dict(taps=tap_rows, m_re=inject(ab_r), m_im=inject(ab_i),
                c_re=readout(ca_re), c_im=readout(-ca_im),
                rep_group=jnp.tile(jnp.eye(SSM_GROUP, dtype=BF16), (1, O)),
                rep_state=jnp.tile(jnp.eye(P, dtype=BF16), (1, O)),
                at_re=pw_re[T], at_im=pw_im[T])


def _ssm(u3, mats, first, seq_id, h0_re, h0_im):
    sre, sim = _ssm_state(u3, mats['m_re'], mats['m_im'], mats['rep_state'])
    xsr, xsi, xer, xei = _ssm_carry(first, seq_id, sre, sim, mats['at_re'], mats['at_im'], h0_re, h0_im)
    y3 = _ssm_out(u3, mats['taps'], xsr, xsi, mats['c_re'], mats['c_im'], mats['rep_group'])
    return y3, xer, xei


def _states_at(xe, idx):
    return jnp.transpose(xe[:, idx, :], (1, 0, 2)).reshape(idx.shape[0], SSM_GROUPS, SSM_STATE)


def _gelu_tanh(x):
    return 0.5 * x * (1.0 + jnp.tanh(math.sqrt(2.0 / math.pi) * (x + 0.044715 * (x * x * x))))


def _glu_kernel(y3_ref, u_ref, d_ref, w_ref, o_ref, wbf_ref, scr_ref):
    @pl.when(pl.program_id(0) == 0)
    def _():
        wbf_ref[...] = w_ref[...].astype(BF16)

    rows = y3_ref.shape[1]
    n_col = SSM_W // LANES
    for t in range(SSM_T):
        for c in range(n_col):
            scr_ref[c, pl.ds(t, rows, stride=SSM_T), :] = y3_ref[t, :, c * LANES:(c + 1) * LANES]
    y = jnp.concatenate([scr_ref[c] for c in range(n_col)], axis=1)
    ys = _gelu_tanh(y + d_ref[...] * u_ref[...])
    t = jnp.dot(ys.astype(BF16), wbf_ref[...], preferred_element_type=F32)
    o_ref[...] = (ys * jax.nn.sigmoid(t)).astype(o_ref.dtype)


def _glu(y3, z, d_all, w_all, l):
    nt = z.shape[0]
    tm = 512
    return pl.pallas_call(
        _glu_kernel,
        grid=(nt // tm,),
        in_specs=[
            pl.BlockSpec((SSM_T, tm // SSM_T, SSM_W), lambda i: (0, i, 0)),
            pl.BlockSpec((tm, SSM_W), lambda i: (i, 3 * ATT_W // SSM_W)),
            pl.BlockSpec((None, 1, SSM_W), lambda i: (l, 0, 0)),
            pl.BlockSpec((None, SSM_W, SSM_W), lambda i: (l, 0, 0)),
        ],
        out_specs=pl.BlockSpec((tm, SSM_W), lambda i: (i, 0)),
        out_shape=jax.ShapeDtypeStruct((nt, SSM_W), BF16),
        scratch_shapes=[pltpu.VMEM((SSM_W, SSM_W), BF16), pltpu.VMEM((SSM_W // LANES, tm, LANES), F32)],
        compiler_params=_cparams("arbitrary"),
        name="glu",
    )(y3, z, d_all.reshape(DEPTH, 1, SSM_W), w_all)


def _merge_kernel(o_ref, y_ref, ga_ref, gs_ref, wa_ref, ws_ref, m_ref):
    a = jnp.dot(o_ref[...], wa_ref[...].astype(BF16), preferred_element_type=F32)
    s = jnp.dot(y_ref[...], ws_ref[...].astype(BF16), preferred_element_type=F32)
    m_ref[...] = (jax.nn.sigmoid(ga_ref[...]) * a + jax.nn.sigmoid(gs_ref[...]) * s).astype(m_ref.dtype)


def _merge(o_p, o_s, ys2, z, w_ba_all, w_bs_all, l):
    nt = ys2.shape[0]
    tm, tn = 1024, 512
    ga0 = (3 * ATT_W + SSM_W) // tn
    gs0 = ga0 + D_MODEL // tn
    p_tiles = o_p.shape[0] // tm
    s_tiles = o_s.shape[0] // tm
    assert p_tiles * tm == o_p.shape[0] and s_tiles * tm == o_s.shape[0]

    def kern(op_ref, os_ref, *rest):
        i = pl.program_id(0)

        @pl.when(i < p_tiles)
        def _():
            _merge_kernel(op_ref, *rest)

        @pl.when(i >= p_tiles)
        def _():
            _merge_kernel(os_ref, *rest)

    return pl.pallas_call(
        kern,
        grid=(nt // tm, D_MODEL // tn),
        in_specs=[
            pl.BlockSpec((tm, ATT_W), lambda i, j: (jnp.minimum(i, p_tiles - 1), 0)),
            pl.BlockSpec((tm, ATT_W), lambda i, j: (jnp.maximum(i - p_tiles, 0), 0)),
            pl.BlockSpec((tm, SSM_W), lambda i, j: (i, 0)),
            pl.BlockSpec((tm, tn), lambda i, j: (i, ga0 + j)),
            pl.BlockSpec((tm, tn), lambda i, j: (i, gs0 + j)),
            pl.BlockSpec((None, ATT_W, tn), lambda i, j: (l, 0, j)),
            pl.BlockSpec((None, SSM_W, tn), lambda i, j: (l, 0, j)),
        ],
        out_specs=pl.BlockSpec((tm, tn), lambda i, j: (i, j)),
        out_shape=jax.ShapeDtypeStruct((nt, D_MODEL), BF16),
        compiler_params=_cparams("arbitrary", "arbitrary"),
        name="merge",
    )(o_p, o_s, ys2, z, z, w_ba_all, w_bs_all)


def _proj_residual_kernel(a_ref, w_ref, x_ref, g_ref, o_ref):
    r = jnp.dot(a_ref[...], w_ref[...].astype(BF16), preferred_element_type=F32)
    o_ref[...] = x_ref[...] + _gated(r, g_ref)


def _proj_residual(a, w_all, x, tab, l, which_gate, x_tail=None):
    nt, k = a.shape
    tm, tn = 1024, 512
    cpt = tm // CHUNK
    kern = _proj_residual_kernel
    x_specs = [pl.BlockSpec((tm, tn), lambda i, j: (i, j))]
    xs = [x]
    if x_tail is not None:
        assert x.shape[0] % tm == 0 and x_tail.shape[0] % tm == 0 and x.shape[0] + x_tail.shape[0] == nt
        head_tiles = x.shape[0] // tm
        kern = _split_rows(_proj_residual_kernel, 2, head_tiles)
        x_specs = _split_specs((tm, tn), head_tiles, True)
        xs = [x, x_tail]
    return pl.pallas_call(
        kern,
        grid=(nt // tm, D_MODEL // tn),
        in_specs=[
            pl.BlockSpec((tm, k), lambda i, j: (i, 0)),
            pl.BlockSpec((None, k, tn), lambda i, j: (l, 0, j)),
            *x_specs,
            _tab_spec(cpt, l, which_gate, width=tn, col=True),
        ],
        out_specs=pl.BlockSpec((tm, tn), lambda i, j: (i, j)),
        out_shape=jax.ShapeDtypeStruct((nt, D_MODEL), F32),
        compiler_params=_cparams("arbitrary", "arbitrary"),
        name="proj_residual",
    )(a, w_all, *xs, tab)


def _swiglu_step(h, wg_ref, wu_ref, wd_ref):
    hg = jnp.dot(h, wg_ref[...].astype(BF16), preferred_element_type=F32)
    hu = jnp.dot(h, wu_ref[...].astype(BF16), preferred_element_type=F32)
    a = (hg * jax.nn.sigmoid(hg)) * hu
    return jnp.dot(a.astype(BF16), wd_ref[...].astype(BF16), preferred_element_type=F32)


def _ffn_kernel(h_ref, wg_ref, wu_ref, wd_ref, o_ref):
    @pl.when(pl.program_id(1) == 0)
    def _():
        o_ref[...] = jnp.zeros_like(o_ref)

    o_ref[...] += _swiglu_step(h_ref[...], wg_ref, wu_ref, wd_ref)


def _ffn(h, w_gate, w_up, w_down, j):
    nt = h.shape[0]
    d_ff = w_gate.shape[-1]
    tm, tf = FFN_TM, FFN_TF
    return pl.pallas_call(
        _ffn_kernel,
        grid=(nt // tm, d_ff // tf),
        in_specs=[
            pl.BlockSpec((tm, D_MODEL), lambda i, f: (i, 0)),
            pl.BlockSpec((None, D_MODEL, tf), lambda i, f: (j, 0, f)),
            pl.BlockSpec((None, D_MODEL, tf), lambda i, f: (j, 0, f)),
            pl.BlockSpec((None, tf, D_MODEL), lambda i, f: (j, f, 0)),
        ],
        out_specs=pl.BlockSpec((tm, D_MODEL), lambda i, f: (i, 0)),
        out_shape=jax.ShapeDtypeStruct((nt, D_MODEL), F32),
        compiler_params=_cparams("arbitrary", "arbitrary"),
        name="ffn",
    )(h, w_gate, w_up, w_down)


def _moe_plan(route, nt):
    tm = MOE_TM
    n_assign = 2 * nt
    n_tiles = (n_assign + N_EXPERTS * (tm - 1)) // tm
    e_flat = jnp.concatenate([route[:, 0], route[:, 1]]).astype(jnp.int32)
    onehot = (e_flat[:, None] == jnp.arange(N_EXPERTS, dtype=jnp.int32)[None, :]).astype(jnp.int32)
    csum = jnp.cumsum(onehot, axis=0)
    counts = csum[-1]
    rank = jnp.sum((csum - onehot) * onehot, axis=1)
    tiles_per = (counts + tm - 1) // tm
    tile_end = jnp.cumsum(tiles_per)
    offsets = (tile_end - tiles_per) * tm
    dest = jnp.sum(onehot * offsets[None, :], axis=1) + rank
    tidx = jnp.arange(n_tiles, dtype=jnp.int32)
    tile_expert = jnp.minimum(jnp.sum((tidx[:, None] >= tile_end[None, :]).astype(jnp.int32), axis=1), N_EXPERTS - 1)
    tile_valid = (tidx < tile_end[-1]).astype(jnp.int32)
    tok = jnp.concatenate([jnp.arange(nt, dtype=jnp.int32)] * 2)
    src_token = jnp.zeros((n_tiles * tm,), jnp.int32).at[dest].set(tok)
    return dict(dest1=dest[:nt], dest2=dest[nt:], src_token=src_token,
                tile_expert=tile_expert.astype(jnp.int32), tile_valid=tile_valid)


def _row_copy(src_ref, row, dst_ref, r, sem):
    return pltpu.make_async_copy(src_ref.at[pl.ds(row, 1)], dst_ref.at[pl.ds(r, 1)], sem)


def _gather_start(idx_ref, base, src_ref, dst_ref, sem):
    rows = dst_ref.shape[0]

    def group(gi, c):
        for u in range(GATHER_UNROLL):
            r = gi * GATHER_UNROLL + u
            _row_copy(src_ref, idx_ref[base + r], dst_ref, r, sem).start(priority=u % 2)
        return c

    lax.fori_loop(0, rows // GATHER_UNROLL, group, 0)


def _gather_wait(src_ref, dst_ref, sem):
    pltpu.make_async_copy(src_ref.at[pl.ds(0, dst_ref.shape[0])], dst_ref, sem).wait()


def _gather_rows_kernel(idx_ref, live_ref, src_ref, o_ref, buf_ref, sem):
    i = pl.program_id(0)
    n = pl.num_programs(0)
    slot = i % 2
    rows = buf_ref.shape[1]

    @pl.when((i == 0) & (live_ref[0] == 1))
    def _():
        _gather_start(idx_ref, 0, src_ref, buf_ref.at[0], sem.at[0])

    @pl.when((i + 1 < n) & (live_ref[jnp.minimum(i + 1, n - 1)] == 1))
    def _():
        _gather_start(idx_ref, (i + 1) * rows, src_ref, buf_ref.at[1 - slot], sem.at[1 - slot])

    @pl.when(live_ref[i] == 1)
    def _():
        _gather_wait(src_ref, buf_ref.at[slot], sem.at[slot])
        o_ref[...] = buf_ref[slot].astype(o_ref.dtype)

    @pl.when(live_ref[i] == 0)
    def _():
        o_ref[...] = jnp.zeros_like(o_ref)


def _gather_rows(src, idx, live, out_dtype):
    n_out = idx.shape[0]
    width = src.shape[1]
    rows = MOE_TM
    assert n_out == live.shape[0] * rows and rows % GATHER_UNROLL == 0
    return pl.pallas_call(
        _gather_rows_kernel,
        grid_spec=pltpu.PrefetchScalarGridSpec(
            num_scalar_prefetch=2,
            grid=(n_out // rows,),
            in_specs=[pl.BlockSpec(memory_space=pl.ANY)],
            out_specs=pl.BlockSpec((rows, width), lambda i, *_: (i, 0)),
            scratch_shapes=[pltpu.VMEM((2, rows, width), src.dtype), pltpu.SemaphoreType.DMA((2,))],
        ),
        out_shape=jax.ShapeDtypeStruct((n_out, width), out_dtype),
        compiler_params=_cparams("arbitrary"),
        name="moe_gather",
    )(idx, live, src)


def _moe_ffn_kernel(te_ref, tv_ref, h_ref, wg_ref, wu_ref, wd_ref, o_ref):
    t = pl.program_id(0)

    @pl.when(pl.program_id(1) == 0)
    def _():
        o_ref[...] = jnp.zeros_like(o_ref)

    @pl.when(tv_ref[t] == 1)
    def _():
        o_ref[...] += _swiglu_step(h_ref[...], wg_ref, wu_ref, wd_ref)


def _moe_ffn(h_sorted, tile_expert, tile_valid, w_gate, w_up, w_down, j):
    d_ff = w_gate.shape[-1]
    tm, tf = MOE_TM, FFN_TF
    n_tiles = h_sorted.shape[0] // tm
    nf = d_ff // tf

    def f_idx(t, f, tv):
        return jnp.where(tv[t] == 1, f, nf - 1)

    return pl.pallas_call(
        _moe_ffn_kernel,
        grid_spec=pltpu.PrefetchScalarGridSpec(
            num_scalar_prefetch=2,
            grid=(n_tiles, nf),
            in_specs=[
                pl.BlockSpec((tm, D_MODEL), lambda t, f, te, tv: (t, 0)),
                pl.BlockSpec((None, None, D_MODEL, tf), lambda t, f, te, tv: (j, te[t], 0, f_idx(t, f, tv))),
                pl.BlockSpec((None, None, D_MODEL, tf), lambda t, f, te, tv: (j, te[t], 0, f_idx(t, f, tv))),
                pl.BlockSpec((None, None, tf, D_MODEL), lambda t, f, te, tv: (j, te[t], f_idx(t, f, tv), 0)),
            ],
            out_specs=pl.BlockSpec((tm, D_MODEL), lambda t, f, te, tv: (t, 0)),
        ),
        out_shape=jax.ShapeDtypeStruct((n_tiles * tm, D_MODEL), F32),
        compiler_params=_cparams("arbitrary", "arbitrary"),
        name="moe_ffn",
    )(tile_expert, tile_valid, h_sorted, w_gate, w_up, w_down)


def _moe_combine_kernel(d1_ref, d2_ref, y_ref, x_ref, route_ref, g_ref, gn_ref, *rest, final_norm, split_tiles):
    n_out = 1 if split_tiles is None else 2
    o_refs, (a_ref, b_ref, sem_a, sem_b) = rest[:n_out], rest[n_out:]
    i = pl.program_id(0)
    n = pl.num_programs(0)
    slot = i % 2

    def start(step, s):
        _gather_start(d1_ref, step * GATHER_ROWS, y_ref, a_ref.at[s], sem_a.at[s])
        _gather_start(d2_ref, step * GATHER_ROWS, y_ref, b_ref.at[s], sem_b.at[s])

    @pl.when(i == 0)
    def _():
        start(0, 0)

    @pl.when(i + 1 < n)
    def _():
        start(i + 1, 1 - slot)

    _gather_wait(y_ref, a_ref.at[slot], sem_a.at[slot])
    _gather_wait(y_ref, b_ref.at[slot], sem_b.at[slot])
    route = route_ref[...]
    r = route[:, 2:3] * a_ref[slot] + route[:, 3:4] * b_ref[slot]
    xn = x_ref[...] + _gated(r, g_ref)
    if final_norm:
        xn = xn * lax.rsqrt(jnp.mean(xn * xn, axis=-1, keepdims=True) + EPS) * gn_ref[...]
    if split_tiles is None:
        o_refs[0][...] = xn
    else:
        @pl.when(i < split_tiles)
        def _():
            o_refs[0][...] = xn

        @pl.when(i >= split_tiles)
        def _():
            o_refs[1][...] = xn


def _moe_combine(y_sorted, dest1, dest2, x, route, tab, l, which_gate, g_final=None, split_rows=None):
    final_norm = g_final is not None
    gn = (g_final if final_norm else jnp.ones((D_MODEL,), F32)).reshape(1, D_MODEL)
    nt = x.shape[0]
    tm = GATHER_ROWS
    blk = pl.BlockSpec((tm, D_MODEL), lambda i, *_: (i, 0))
    buf = pltpu.VMEM((2, tm, D_MODEL), F32)
    if split_rows is None:
        split_tiles = None
        out_specs = blk
        out_shape = jax.ShapeDtypeStruct((nt, D_MODEL), F32)
    else:
        split_tiles = split_rows // tm
        assert split_tiles * tm == split_rows and 0 < split_rows < nt
        out_specs = [pl.BlockSpec((tm, D_MODEL), lambda i, *_: (jnp.minimum(i, split_tiles - 1), 0)),
                     pl.BlockSpec((tm, D_MODEL), lambda i, *_: (jnp.maximum(i - split_tiles, 0), 0))]
        out_shape = [jax.ShapeDtypeStruct((split_rows, D_MODEL), F32),
                     jax.ShapeDtypeStruct((nt - split_rows, D_MODEL), F32)]
    return pl.pallas_call(
        functools.partial(_moe_combine_kernel, final_norm=final_norm, split_tiles=split_tiles),
        grid_spec=pltpu.PrefetchScalarGridSpec(
            num_scalar_prefetch=2,
            grid=(nt // tm,),
            in_specs=[pl.BlockSpec(memory_space=pl.ANY), blk, pl.BlockSpec((tm, LANES), lambda i, *_: (i, 0)),
                      _tab_spec(tm // CHUNK, l, which_gate), pl.BlockSpec((1, D_MODEL), lambda i, *_: (0, 0))],
            out_specs=out_specs,
            scratch_shapes=[buf, buf, pltpu.SemaphoreType.DMA((2,)), pltpu.SemaphoreType.DMA((2,))],
        ),
        out_shape=out_shape,
        compiler_params=_cparams("arbitrary"),
        name="moe_combine",
    )(dest1, dest2, y_sorted, x, route, tab, gn)


def _residual_kernel(x_ref, r_ref, g_ref, o_ref):
    o_ref[...] = x_ref[...] + _gated(r_ref[...], g_ref)


def _residual(x, r, tab, l, which_gate):
    nt = x.shape[0]
    tm = 512
    blk = pl.BlockSpec((tm, D_MODEL), lambda i: (i, 0))
    return pl.pallas_call(
        _residual_kernel, grid=(nt // tm,),
        in_specs=[blk, blk, _tab_spec(tm // CHUNK, l, which_gate)],
        out_specs=blk, out_shape=jax.ShapeDtypeStruct((nt, D_MODEL), F32),
        compiler_params=_cparams("arbitrary"), name="residual",
    )(x, r, tab)


def _residual_norm_kernel(x_ref, r_ref, gate_ref, g_ref, sh_ref, sc_ref, xo_ref, h_ref):
    xn = x_ref[...] + _gated(r_ref[...], gate_ref)
    xo_ref[...] = xn
    h_ref[...] = _modulate(xn, g_ref, sh_ref, sc_ref).astype(BF16)


def _residual_norm(x, r, tab, l, which_gate, g_all, l_next, which_shift, which_scale):
    nt = x.shape[0]
    tm = 512
    cpt = tm // CHUNK
    blk = pl.BlockSpec((tm, D_MODEL), lambda i: (i, 0))
    return pl.pallas_call(
        _residual_norm_kernel, grid=(nt // tm,),
        in_specs=[blk, blk, _tab_spec(cpt, l, which_gate),
                  pl.BlockSpec((None, 1, D_MODEL), lambda i: (l_next, 0, 0)),
                  _tab_spec(cpt, l_next, which_shift), _tab_spec(cpt, l_next, which_scale)],
        out_specs=[blk, blk],
        out_shape=[jax.ShapeDtypeStruct((nt, D_MODEL), F32), jax.ShapeDtypeStruct((nt, D_MODEL), BF16)],
        compiler_params=_cparams("arbitrary"), name="residual_norm",
    )(x, r, tab, g_all.reshape(DEPTH, 1, D_MODEL), tab, tab)


def _kv_rows_kernel(z_ref, o_ref):
    for h in range(N_HEADS):
        o_ref[:, h, :] = z_ref[:, h * HEAD_DIM:(h + 1) * HEAD_DIM]


def _kv_rows(z, n_pb, seq, n_sb):
    rows = PAST_WINDOW
    per_seq = seq // rows
    s_blocks = n_sb * CHUNK // rows
    assert per_seq * rows == seq and s_blocks * rows == n_sb * CHUNK

    def row_block(r):
        return jnp.where(r < n_pb, (r + 1) * per_seq - 1, n_pb * per_seq + (r - n_pb))

    n_blocks = n_pb + s_blocks
    return pl.pallas_call(
        _kv_rows_kernel, grid=(2, n_blocks),
        in_specs=[pl.BlockSpec((rows, ATT_W), lambda w, r: (row_block(r), 1 + w))],
        out_specs=pl.BlockSpec((None, rows, N_HEADS, HEAD_DIM), lambda w, r: (w, r, 0, 0)),
        out_shape=jax.ShapeDtypeStruct((2, n_blocks * rows, N_HEADS, HEAD_DIM), F32),
        compiler_params=_cparams("arbitrary", "arbitrary"), name="kv_rows",
    )(z)


def _final_norm_kernel(x_ref, g_ref, o_ref):
    x = x_ref[...]
    o_ref[...] = x * lax.rsqrt(jnp.mean(x * x, axis=-1, keepdims=True) + EPS) * g_ref[...]


def _final_norm(x, g):
    nt = x.shape[0]
    tm = 512
    blk = pl.BlockSpec((tm, D_MODEL), lambda i: (i, 0))
    return pl.pallas_call(
        _final_norm_kernel, grid=(nt // tm,),
        in_specs=[blk, pl.BlockSpec((1, D_MODEL), lambda i: (0, 0))],
        out_specs=blk, out_shape=jax.ShapeDtypeStruct((nt, D_MODEL), F32),
        compiler_params=_cparams("arbitrary"), name="final_norm",
    )(x, g.reshape(1, D_MODEL))


def kernel(x_prompt, x_sample, cache_k, cache_v, state_ssm_re, state_ssm_im, c_prompt, c_sample, w_ada, b_ada, g_norm_mix, g_norm_ffn, g_norm_final, w_in, rel_bias, ssm_a_re, ssm_a_im, ssm_log_step, ssm_b_re, ssm_b_im, ssm_c_re, ssm_c_im, ssm_d, w_ssm_glu, w_branch_attn, w_branch_ssm, w_out, w_ffn_gate, w_ffn_up, w_ffn_down, w_router, b_router, w_exp_gate, w_exp_up, w_exp_down):
    n_pb, seq = x_prompt.shape[0], x_prompt.shape[1]
    n_sb, dec = x_sample.shape[0], x_sample.shape[1]
    assert dec == CHUNK and cache_k.shape[2] == PAST_WINDOW and seq % ATT_ROWS == 0
    np_tok = n_pb * seq
    nt = np_tok + n_sb * dec
    n_batch = n_pb + n_sb

    x_head = x_prompt.reshape(np_tok, D_MODEL)
    x_tail = x_sample.reshape(n_sb * dec, D_MODEL)

    c_all = jnp.concatenate([c_prompt, c_sample], axis=0)
    c_pad = jnp.pad(c_all, ((0, -n_batch % 8), (0, 0)))
    ada = _ada(c_pad, w_ada, b_ada)
    ada_p = jnp.broadcast_to(ada[:, :n_pb, None, :], (DEPTH, n_pb, seq // CHUNK, 6 * D_MODEL))
    tab = jnp.concatenate([ada_p.reshape(DEPTH, np_tok // CHUNK, 6 * D_MODEL), ada[:, n_pb:n_batch]], axis=1)

    sub_p, sub_s = seq // SSM_T, dec // SSM_T
    seq_id = jnp.concatenate([jnp.repeat(jnp.arange(n_pb), sub_p), n_pb + jnp.repeat(jnp.arange(n_sb), sub_s)])
    seq_id = seq_id.astype(jnp.int32)
    first = jnp.concatenate([jnp.arange(n_pb * sub_p) % sub_p == 0, jnp.arange(n_sb * sub_s) % sub_s == 0])
    first = first.astype(jnp.int32)
    last_p = (jnp.arange(n_pb) + 1) * sub_p - 1
    last_s = n_pb * sub_p + (jnp.arange(n_sb) + 1) * sub_s - 1
    zeros_h0 = jnp.zeros((n_pb, SSM_GROUPS, SSM_STATE), F32)

    outs = {k: [] for k in ('kp', 'vp', 'srp', 'sip', 'ks', 'vs', 'srs', 'sis')}
    keep = min(PAST_WINDOW, seq)
    h = _norm_mod(x_head, g_norm_mix, tab, 0, SHIFT1, SCALE1, x_tail=x_tail)
    for l in range(DEPTH):
        last = l == DEPTH - 1
        z, u3 = _in_proj(h, w_in, l)

        b64 = _attention_bias(rel_bias[l])
        o_p = _attn_prompt(z, b64, n_pb, seq)
        o_s = _attn_sample(z, cache_k, cache_v, l, b64, np_tok, n_sb)

        mats = _ssm_matrices(ssm_a_re[l], ssm_a_im[l], ssm_log_step[l], ssm_b_re[l], ssm_b_im[l],
                             ssm_c_re[l], ssm_c_im[l])
        h0_re = jnp.concatenate([zeros_h0, state_ssm_re[l].astype(F32)], axis=0)
        h0_im = jnp.concatenate([zeros_h0, state_ssm_im[l].astype(F32)], axis=0)
        y3, xe_re, xe_im = _ssm(u3, mats, first, seq_id, h0_re, h0_im)

        ys2 = _glu(y3, z, ssm_d, w_ssm_glu, l)
        merged = _merge(o_p, o_s, ys2, z, w_branch_attn, w_branch_ssm, l)
        if l == 0:
            x = _proj_residual(merged, w_out, x_head, tab, l, GATE1, x_tail=x_tail)
        else:
            x = _proj_residual(merged, w_out, x, tab, l, GATE1)

        j = l // 2
        if l % 2 == 0:
            h2 = _norm_mod(x, g_norm_ffn, tab, l, SHIFT2, SCALE2)
            r = _ffn(h2, w_ffn_gate, w_ffn_up, w_ffn_down, j)
            if last:
                yn = _final_norm(_residual(x, r, tab, l, GATE2), g_norm_final)
                yn_p, yn_s = yn[:np_tok], yn[np_tok:]
            else:
                x, h = _residual_norm(x, r, tab, l, GATE2, g_norm_mix, l + 1, SHIFT1, SCALE1)
        else:
            h2, route = _norm_mod(x, g_norm_ffn, tab, l, SHIFT2, SCALE2, router=(w_router[j], b_router[j]))
            plan = _moe_plan(route, nt)
            h_sorted = _gather_rows(h2, plan['src_token'], plan['tile_valid'], BF16)
            y_sorted = _moe_ffn(h_sorted, plan['tile_expert'], plan['tile_valid'], w_exp_gate, w_exp_up, w_exp_down, j)
            if last:
                yn_p, yn_s = _moe_combine(y_sorted, plan['dest1'], plan['dest2'], x, route, tab, l, GATE2,
                                          g_final=g_norm_final, split_rows=np_tok)
            else:
                x = _moe_combine(y_sorted, plan['dest1'], plan['dest2'], x, route, tab, l, GATE2)
                h = _norm_mod(x, g_norm_mix, tab, l + 1, SHIFT1, SCALE1)

        kv = _kv_rows(z, n_pb, seq, n_sb)
        outs['kp'].append(kv[0, :n_pb * keep].reshape(n_pb, keep, N_HEADS, HEAD_DIM))
        outs['vp'].append(kv[1, :n_pb * keep].reshape(n_pb, keep, N_HEADS, HEAD_DIM))
        outs['ks'].append(kv[0, n_pb * keep:].reshape(n_sb, dec, N_HEADS, HEAD_DIM))
        outs['vs'].append(kv[1, n_pb * keep:].reshape(n_sb, dec, N_HEADS, HEAD_DIM))
        outs['srp'].append(_states_at(xe_re, last_p))
        outs['sip'].append(_states_at(xe_im, last_p))
        outs['srs'].append(_states_at(xe_re, last_s))
        outs['sis'].append(_states_at(xe_im, last_s))

    y_prompt = yn_p.reshape(n_pb, seq, D_MODEL)
    y_sample = yn_s.reshape(n_sb, dec, D_MODEL)
    st = lambda name: jnp.stack(outs[name])
    return (y_prompt, y_sample, st('kp'), st('vp'), st('srp'), st('sip'),
            st('ks'), st('vs'), st('srs'), st('sis'))
```

```python
import functools
import math

import jax
import jax.numpy as jnp
from jax import lax
from jax.experimental import pallas as pl
from jax.experimental.pallas import tpu as pltpu

F32 = jnp.float32
BF16 = jnp.bfloat16

D_MODEL = 2048
DEPTH = 2
CHUNK = 64
PAST_CHUNKS = 8
PAST_WINDOW = PAST_CHUNKS * CHUNK
BAND = PAST_WINDOW + CHUNK
N_HEADS = 8
HEAD_DIM = 128
ATT_W = N_HEADS * HEAD_DIM
ATT_SCALE = HEAD_DIM ** -0.5
REL_CLIP = 256
SSM_W = 1024
SSM_GROUP = 16
SSM_GROUPS = SSM_W // SSM_GROUP
SSM_STATE = 64
N_EXPERTS = 8
IN_COLS = 3 * ATT_W + SSM_W + 2 * D_MODEL
EPS = 1e-6
NEG_INF = -1e30

LANES = 128
VMEM_LIMIT_BYTES = 56 * 1024 * 1024

SSM_T = 16
SSM_OCT = LANES // SSM_GROUP
N_OCT = SSM_GROUPS // SSM_OCT
OCT_K = SSM_T * LANES
OCT_STATE = SSM_OCT * SSM_STATE
GROUP_SHIFT = SSM_GROUP.bit_length() - 1
STATE_SHIFT = SSM_STATE.bit_length() - 1
assert (1 << GROUP_SHIFT, 1 << STATE_SHIFT) == (SSM_GROUP, SSM_STATE)
ATT_GROUP = 4
ATT_ROWS = ATT_GROUP * CHUNK
ATT_KEYS = 3 * ATT_ROWS

IN_TM, IN_TN = 1536, 512
FFN_TM = 768
MOE_TM = 768
FFN_TF = 512
GATHER_ROWS = 512
GATHER_UNROLL = 8

SHIFT1, SCALE1, GATE1, SHIFT2, SCALE2, GATE2 = range(6)


def _cparams(*sem):
    return pltpu.CompilerParams(dimension_semantics=sem, vmem_limit_bytes=VMEM_LIMIT_BYTES)


def _tab_spec(cpt, l, which, width=D_MODEL, col=None):
    per = D_MODEL // width
    if col is None:
        return pl.BlockSpec((None, cpt, width), lambda i, *_: (l, i, which * per))
    return pl.BlockSpec((None, cpt, width), lambda i, j, *_: (l, i, which * per + j))


def _chunk_rows(v, fn):
    return jnp.concatenate([fn(v[c * CHUNK:(c + 1) * CHUNK], c) for c in range(v.shape[0] // CHUNK)], axis=0)


def _gated(r, g_ref):
    return _chunk_rows(r, lambda rows, c: rows * g_ref[c:c + 1, :])


def _ada_kernel(c_ref, w_ref, b_ref, o_ref):
    c = c_ref[...]
    a = (c * jax.nn.sigmoid(c)).astype(BF16)
    o_ref[...] = jnp.dot(a, w_ref[...].astype(BF16), preferred_element_type=F32) + b_ref[...]


def _ada(c_pad, w_ada, b_ada):
    rows = c_pad.shape[0]
    n = w_ada.shape[-1]
    tn = 1024
    return pl.pallas_call(
        _ada_kernel,
        grid=(DEPTH, n // tn),
        in_specs=[
            pl.BlockSpec((rows, D_MODEL), lambda l, j: (0, 0)),
            pl.BlockSpec((None, D_MODEL, tn), lambda l, j: (l, 0, j)),
            pl.BlockSpec((None, 1, tn), lambda l, j: (l, 0, j)),
        ],
        out_specs=pl.BlockSpec((None, rows, tn), lambda l, j: (l, 0, j)),
        out_shape=jax.ShapeDtypeStruct((DEPTH, rows, n), F32),
        compiler_params=_cparams("arbitrary", "arbitrary"),
        name="ada",
    )(c_pad, w_ada, b_ada.reshape(DEPTH, 1, n))


def _modulated(x_ref, g_ref, sh_ref, sc_ref):
    return _modulate(x_ref[...], g_ref, sh_ref, sc_ref)


def _modulate(x, g_ref, sh_ref, sc_ref):
    y = x * lax.rsqrt(jnp.mean(x * x, axis=-1, keepdims=True) + EPS) * g_ref[...]
    return _chunk_rows(y, lambda rows, c: rows * (1.0 + sc_ref[c:c + 1, :]) + sh_ref[c:c + 1, :])


def _norm_mod_kernel(x_ref, g_ref, sh_ref, sc_ref, h_ref):
    h_ref[...] = _modulated(x_ref, g_ref, sh_ref, sc_ref).astype(BF16)


def _split_bf16(a):
    hi = a.astype(BF16)
    lo = (a - hi.astype(F32)).astype(BF16)
    return hi, lo


def _norm_mod_router_kernel(x_ref, g_ref, sh_ref, sc_ref, wr_ref, br_ref, h_ref, route_ref):
    h = _modulated(x_ref, g_ref, sh_ref, sc_ref)
    h_ref[...] = h
    h_hi, h_lo = _split_bf16(h)
    w_hi, w_lo = _split_bf16(wr_ref[...])
    dot = functools.partial(jnp.dot, preferred_element_type=F32)
    logits = (dot(h_hi, w_hi) + (dot(h_hi, w_lo) + dot(h_lo, w_hi)) + dot(h_lo, w_lo)) + br_ref[...]
    lane = lax.broadcasted_iota(jnp.int32, logits.shape, 1).astype(F32)
    lg = jnp.where(lane < N_EXPERTS, logits, -jnp.inf)
    m1 = jnp.max(lg, axis=-1, keepdims=True)
    i1 = jnp.min(jnp.where(lg == m1, lane, float(LANES)), axis=-1, keepdims=True)
    lg2 = jnp.where(lane == i1, -jnp.inf, lg)
    m2 = jnp.max(lg2, axis=-1, keepdims=True)
    i2 = jnp.min(jnp.where(lg2 == m2, lane, float(LANES)), axis=-1, keepdims=True)
    e2 = jnp.exp(m2 - m1)
    den = 1.0 + e2
    route = jnp.where(lane == 0.0, i1, jnp.where(lane == 1.0, i2, jnp.where(lane == 2.0, 1.0 / den, e2 / den)))
    route_ref[...] = jnp.where(lane < 4.0, route, 0.0)


def _split_rows(kernel_fn, pos, head_tiles):
    def kern(*refs):
        before, after = refs[:pos], refs[pos + 2:]
        i = pl.program_id(0)

        @pl.when(i < head_tiles)
        def _():
            kernel_fn(*before, refs[pos], *after)

        @pl.when(i >= head_tiles)
        def _():
            kernel_fn(*before, refs[pos + 1], *after)

    return kern


def _split_specs(block, head_tiles, with_col):
    if with_col:
        return [pl.BlockSpec(block, lambda i, j: (jnp.minimum(i, head_tiles - 1), j)),
                pl.BlockSpec(block, lambda i, j: (jnp.maximum(i - head_tiles, 0), j))]
    return [pl.BlockSpec(block, lambda i: (jnp.minimum(i, head_tiles - 1), 0)),
            pl.BlockSpec(block, lambda i: (jnp.maximum(i - head_tiles, 0), 0))]


def _norm_mod(x, g_all, tab, l, which_shift, which_scale, router=None, x_tail=None):
    tm = 512 if router is not None else 1024
    cpt = tm // CHUNK
    in_specs = [
        pl.BlockSpec((tm, D_MODEL), lambda i: (i, 0)),
        pl.BlockSpec((None, 1, D_MODEL), lambda i: (l, 0, 0)),
        _tab_spec(cpt, l, which_shift),
        _tab_spec(cpt, l, which_scale),
    ]
    h_spec = pl.BlockSpec((tm, D_MODEL), lambda i: (i, 0))
    args = [x, g_all.reshape(DEPTH, 1, D_MODEL), tab, tab]
    if x_tail is not None:
        assert router is None and x.shape[0] % tm == 0 and x_tail.shape[0] % tm == 0
        head_tiles = x.shape[0] // tm
        nt = x.shape[0] + x_tail.shape[0]
        return pl.pallas_call(
            _split_rows(_norm_mod_kernel, 0, head_tiles), grid=(nt // tm,),
            in_specs=_split_specs((tm, D_MODEL), head_tiles, False) + in_specs[1:], out_specs=h_spec,
            out_shape=jax.ShapeDtypeStruct((nt, D_MODEL), BF16),
            compiler_params=_cparams("arbitrary"), name="norm_mod",
        )(x, x_tail, *args[1:])
    nt = x.shape[0]
    if router is None:
        return pl.pallas_call(
            _norm_mod_kernel, grid=(nt // tm,), in_specs=in_specs, out_specs=h_spec,
            out_shape=jax.ShapeDtypeStruct((nt, D_MODEL), BF16),
            compiler_params=_cparams("arbitrary"), name="norm_mod",
        )(*args)
    w_r, b_r = router
    w_pad = jnp.zeros((D_MODEL, LANES), F32).at[:, :N_EXPERTS].set(w_r)
    b_pad = jnp.zeros((1, LANES), F32).at[0, :N_EXPERTS].set(b_r)
    in_specs += [
        pl.BlockSpec((D_MODEL, LANES), lambda i: (0, 0)),
        pl.BlockSpec((1, LANES), lambda i: (0, 0)),
    ]
    return pl.pallas_call(
        _norm_mod_router_kernel, grid=(nt // tm,), in_specs=in_specs,
        out_specs=[h_spec, pl.BlockSpec((tm, LANES), lambda i: (i, 0))],
        out_shape=[jax.ShapeDtypeStruct((nt, D_MODEL), F32), jax.ShapeDtypeStruct((nt, LANES), F32)],
        compiler_params=_cparams("arbitrary"), name="norm_mod_router",
    )(*args, w_pad, b_pad)


def _in_proj_kernel(x_ref, w_ref, z_ref, u3_ref, scr_ref):
    j = pl.program_id(1)
    r = jnp.dot(x_ref[...], w_ref[...].astype(BF16), preferred_element_type=F32)
    z_ref[...] = r
    tm, tn = r.shape
    u_first = 3 * ATT_W // tn
    for part in range(SSM_W // tn):
        @pl.when(j == u_first + part)
        def _():
            for c in range(tn // LANES):
                scr_ref[c] = r[:, c * LANES:(c + 1) * LANES]
            for s in range(SSM_T):
                for c in range(tn // LANES):
                    lo = part * tn + c * LANES
                    rows = scr_ref[c, pl.ds(s, tm // SSM_T, stride=SSM_T), :]
                    u3_ref[s, :, lo:lo + LANES] = rows.astype(BF16)


def _in_proj(h, w_all, l):
    nt, k = h.shape
    n = w_all.shape[-1]
    tm, tn = IN_TM, IN_TN
    return pl.pallas_call(
        _in_proj_kernel,
        grid=(nt // tm, n // tn),
        in_specs=[
            pl.BlockSpec((tm, k), lambda i, j: (i, 0)),
            pl.BlockSpec((None, k, tn), lambda i, j: (l, 0, j)),
        ],
        out_specs=[
            pl.BlockSpec((tm, tn), lambda i, j: (i, j)),
            pl.BlockSpec((SSM_T, tm // SSM_T, SSM_W), lambda i, j: (0, i, 0)),
        ],
        out_shape=[
            jax.ShapeDtypeStruct((nt, n), F32),
            jax.ShapeDtypeStruct((SSM_T, nt // SSM_T, SSM_W), BF16),
        ],
        scratch_shapes=[pltpu.VMEM((tn // LANES, tm, LANES), F32)],
        compiler_params=_cparams("arbitrary", "arbitrary"),
        name="in_proj",
    )(h, w_all)


def _softmax_rows(s):
    m = jnp.max(s, axis=-1, keepdims=True)
    e = jnp.exp(s - m)
    return e / jnp.sum(e, axis=-1, keepdims=True)


def _attn_prompt_kernel(q_ref, k0_ref, k1_ref, k2_ref, v0_ref, v1_ref, v2_ref, b64_ref, o_ref, bias_ref):
    g = pl.program_id(1)

    @pl.when((pl.program_id(0) == 0) & (g == 0))
    def _():
        bias_ref[...] = jnp.full(bias_ref.shape, NEG_INF, F32)
        for c in range(ATT_GROUP):
            bias_ref[:, c * CHUNK:(c + 1) * CHUNK, c * CHUNK:c * CHUNK + BAND] = b64_ref[...]

    col = lax.broadcasted_iota(jnp.int32, (ATT_ROWS, ATT_KEYS), 1)
    visible = (g * ATT_ROWS - 2 * ATT_ROWS + col) >= 0
    for h in range(N_HEADS):
        sl = slice(h * HEAD_DIM, (h + 1) * HEAD_DIM)
        qh = q_ref[:, sl].astype(BF16)
        kh = jnp.concatenate([k0_ref[:, sl], k1_ref[:, sl], k2_ref[:, sl]], axis=0).astype(BF16)
        vh = jnp.concatenate([v0_ref[:, sl], v1_ref[:, sl], v2_ref[:, sl]], axis=0).astype(BF16)
        s = lax.dot_general(qh, kh, (((1,), (1,)), ((), ())), preferred_element_type=F32)
        s = s * ATT_SCALE + bias_ref[h]
        s = jnp.where(visible, s, NEG_INF)
        p = _softmax_rows(s).astype(BF16)
        o_ref[:, sl] = jnp.dot(p, vh, preferred_element_type=F32).astype(o_ref.dtype)


def _attn_prompt(z, b64, n_batch, seq):
    groups = seq // ATT_ROWS
    blk = (ATT_ROWS, ATT_W)

    def kv_spec(col, back):
        return pl.BlockSpec(blk, lambda b, g: (b * groups + jnp.maximum(g - back, 0), col))

    return pl.pallas_call(
        _attn_prompt_kernel,
        grid=(n_batch, groups),
        in_specs=[
            pl.BlockSpec(blk, lambda b, g: (b * groups + g, 0)),
            kv_spec(1, 2), kv_spec(1, 1), kv_spec(1, 0),
            kv_spec(2, 2), kv_spec(2, 1), kv_spec(2, 0),
            pl.BlockSpec((N_HEADS, CHUNK, BAND), lambda b, g: (0, 0, 0)),
        ],
        out_specs=pl.BlockSpec(blk, lambda b, g: (b * groups + g, 0)),
        out_shape=jax.ShapeDtypeStruct((n_batch * seq, ATT_W), BF16),
        scratch_shapes=[pltpu.VMEM((N_HEADS, ATT_ROWS, ATT_KEYS), F32)],
        compiler_params=_cparams("arbitrary", "arbitrary"),
        name="attn_prompt",
    )(z, z, z, z, z, z, z, b64)


def _attn_sample_kernel(q_ref, kn_ref, vn_ref, kc_ref, vc_ref, bias_ref, o_ref):
    for h in range(N_HEADS):
        sl = slice(h * HEAD_DIM, (h + 1) * HEAD_DIM)
        qh = q_ref[:, sl].astype(BF16)
        kh = jnp.concatenate([kc_ref[:, h, :], kn_ref[:, sl]], axis=0).astype(BF16)
        vh = jnp.concatenate([vc_ref[:, h, :], vn_ref[:, sl]], axis=0).astype(BF16)
        s = lax.dot_general(qh, kh, (((1,), (1,)), ((), ())), preferred_element_type=F32)
        s = s * ATT_SCALE + bias_ref[h]
        p = _softmax_rows(s).astype(BF16)
        o_ref[:, sl] = jnp.dot(p, vh, preferred_element_type=F32).astype(o_ref.dtype)


def _attn_sample(z, cache_k, cache_v, l, b64, row0, n_batch):
    base = row0 // CHUNK
    blk = (CHUNK, ATT_W)
    cache_spec = pl.BlockSpec((None, None, PAST_WINDOW, N_HEADS, HEAD_DIM), lambda b: (l, b, 0, 0, 0))
    return pl.pallas_call(
        _attn_sample_kernel,
        grid=(n_batch,),
        in_specs=[
            pl.BlockSpec(blk, lambda b: (base + b, 0)),
            pl.BlockSpec(blk, lambda b: (base + b, 1)),
            pl.BlockSpec(blk, lambda b: (base + b, 2)),
            cache_spec, cache_spec,
            pl.BlockSpec((N_HEADS, CHUNK, BAND), lambda b: (0, 0, 0)),
        ],
        out_specs=pl.BlockSpec(blk, lambda b: (b, 0)),
        out_shape=jax.ShapeDtypeStruct((n_batch * CHUNK, ATT_W), BF16),
        compiler_params=_cparams("arbitrary"),
        name="attn_sample",
    )(z, z, z, cache_k, cache_v, b64)


def _attention_bias(rel_bias):
    rev = rel_bias[::-1].astype(F32)
    flat = PAST_WINDOW - REL_CLIP + CHUNK - 1
    vec = jnp.concatenate([jnp.broadcast_to(rev[:1], (flat, N_HEADS)), rev[:BAND + CHUNK - 1 - flat]], axis=0)
    rows = [vec[CHUNK - 1 - qi:CHUNK - 1 - qi + BAND] for qi in range(CHUNK)]
    return jnp.transpose(jnp.stack(rows), (2, 0, 1))


def _octet_lhs(u3_ref):
    return jnp.concatenate([u3_ref[s] for s in range(SSM_T)], axis=1)


def _group_mask(rows, cols, row_shift, col_shift):
    r = lax.shift_right_logical(lax.broadcasted_iota(jnp.int32, (rows, cols), 0), row_shift)
    c = lax.shift_right_logical(lax.broadcasted_iota(jnp.int32, (rows, cols), 1), col_shift)
    return r == c


def _blockdiag(x, rep_ref, mask):
    e = jnp.dot(x, rep_ref[...], preferred_element_type=F32)
    return jnp.where(mask, e, 0.0).astype(BF16)


def _ssm_state_kernel(u3_ref, mre_ref, mim_ref, rep_ref, sre_ref, sim_ref, w_ref):
    mask = _group_mask(LANES, OCT_STATE, GROUP_SHIFT, STATE_SHIFT)
    for s in range(SSM_T):
        rows = slice(s * LANES, (s + 1) * LANES)
        w_ref[rows, :OCT_STATE] = _blockdiag(mre_ref[s], rep_ref, mask)
        w_ref[rows, OCT_STATE:] = _blockdiag(mim_ref[s], rep_ref, mask)
    s = jnp.dot(_octet_lhs(u3_ref), w_ref[...], preferred_element_type=F32)
    sre_ref[...] = s[:, :OCT_STATE]
    sim_ref[...] = s[:, OCT_STATE:]


def _ssm_state(u3, m_re, m_im, rep_state):
    n_sub = u3.shape[1]
    out = jax.ShapeDtypeStruct((N_OCT, n_sub, OCT_STATE), F32)
    m_spec = pl.BlockSpec((None, SSM_T, LANES, SSM_STATE), lambda j: (j, 0, 0, 0))
    return pl.pallas_call(
        _ssm_state_kernel,
        grid=(N_OCT,),
        in_specs=[
            pl.BlockSpec((SSM_T, n_sub, LANES), lambda j: (0, 0, j)),
            m_spec, m_spec,
            pl.BlockSpec((SSM_STATE, OCT_STATE), lambda j: (0, 0)),
        ],
        out_specs=[pl.BlockSpec((None, n_sub, OCT_STATE), lambda j: (j, 0, 0))] * 2,
        out_shape=[out, out],
        scratch_shapes=[pltpu.VMEM((OCT_K, 2 * OCT_STATE), BF16)],
        compiler_params=_cparams("arbitrary"),
        name="ssm_state",
    )(u3, m_re, m_im, rep_state)


def _ssm_carry_kernel(first_ref, seq_ref, sre_ref, sim_ref, ar_ref, ai_ref, h0r_ref, h0i_ref,
                      xsr_ref, xsi_ref, xer_ref, xei_ref, st_ref):
    blk = pl.program_id(0)
    steps = sre_ref.shape[1]

    @pl.when(blk == 0)
    def _():
        st_ref[...] = jnp.zeros_like(st_ref)

    ar = ar_ref[...]
    ai = ai_ref[...]

    def body(kk, carry):
        k = blk * steps + kk
        is_first = first_ref[k] == 1
        sq = seq_ref[k]
        xr = jnp.where(is_first, h0r_ref[sq], st_ref[0])
        xi = jnp.where(is_first, h0i_ref[sq], st_ref[1])
        xsr_ref[:, kk, :] = xr
        xsi_ref[:, kk, :] = xi
        nr = ar * xr - ai * xi + sre_ref[:, kk, :]
        ni = ar * xi + ai * xr + sim_ref[:, kk, :]
        xer_ref[:, kk, :] = nr
        xei_ref[:, kk, :] = ni
        st_ref[0] = nr
        st_ref[1] = ni
        return carry

    lax.fori_loop(0, steps, body, 0)


def _ssm_carry(first, seq_id, sre, sim, at_re, at_im, h0_re, h0_im):
    n_sub = sre.shape[1]
    steps = 64
    tile = (N_OCT, OCT_STATE)
    blk = pl.BlockSpec((N_OCT, steps, OCT_STATE), lambda i, *_: (0, i, 0))
    const = pl.BlockSpec(tile, lambda i, *_: (0, 0))
    tab = pl.BlockSpec((h0_re.shape[0],) + tile, lambda i, *_: (0, 0, 0))
    out = jax.ShapeDtypeStruct((N_OCT, n_sub, OCT_STATE), F32)
    return pl.pallas_call(
        _ssm_carry_kernel,
        grid_spec=pltpu.PrefetchScalarGridSpec(
            num_scalar_prefetch=2,
            grid=(n_sub // steps,),
            in_specs=[blk, blk, const, const, tab, tab],
            out_specs=[blk, blk, blk, blk],
            scratch_shapes=[pltpu.VMEM((2,) + tile, F32)],
        ),
        out_shape=[out, out, out, out],
        compiler_params=_cparams("arbitrary"),
        name="ssm_carry",
    )(first, seq_id, sre, sim, at_re.reshape(tile), at_im.reshape(tile),
      h0_re.reshape((-1,) + tile), h0_im.reshape((-1,) + tile))


def _ssm_out_kernel(u3_ref, taps_ref, xr_ref, xi_ref, ccr_ref, cci_ref, rep_ref, y3_ref, kt_ref, cr_ref, ci_ref):
    @pl.when(pl.program_id(0) == 0)
    def _():
        kt_ref[...] = jnp.zeros_like(kt_ref)

    tap_mask = _group_mask(LANES, LANES, GROUP_SHIFT, GROUP_SHIFT)
    for tau in range(SSM_T):
        block = _blockdiag(taps_ref[tau], rep_ref, tap_mask)
        for s in range(SSM_T - tau):
            t = s + tau
            kt_ref[s * LANES:(s + 1) * LANES, t * LANES:(t + 1) * LANES] = block
    c_mask = _group_mask(OCT_STATE, LANES, STATE_SHIFT, GROUP_SHIFT)
    for t in range(SSM_T):
        cols = slice(t * LANES, (t + 1) * LANES)
        cr_ref[:, cols] = _blockdiag(ccr_ref[t], rep_ref, c_mask)
        ci_ref[:, cols] = _blockdiag(cci_ref[t], rep_ref, c_mask)

    y = jnp.dot(_octet_lhs(u3_ref), kt_ref[...], preferred_element_type=F32)
    y += jnp.dot(xr_ref[...].astype(BF16), cr_ref[...], preferred_element_type=F32)
    y += jnp.dot(xi_ref[...].astype(BF16), ci_ref[...], preferred_element_type=F32)
    for t in range(SSM_T):
        y3_ref[t] = y[:, t * LANES:(t + 1) * LANES]


def _ssm_out(u3, taps, xs_re, xs_im, c_re, c_im, rep_group):
    n_sub = u3.shape[1]
    u_spec = pl.BlockSpec((SSM_T, n_sub, LANES), lambda j: (0, 0, j))
    x_spec = pl.BlockSpec((None, n_sub, OCT_STATE), lambda j: (j, 0, 0))
    c_spec = pl.BlockSpec((None, SSM_T, OCT_STATE, SSM_GROUP), lambda j: (j, 0, 0, 0))
    return pl.pallas_call(
        _ssm_out_kernel,
        grid=(N_OCT,),
        in_specs=[u_spec, pl.BlockSpec((None, SSM_T, LANES, SSM_GROUP), lambda j: (j, 0, 0, 0)),
                  x_spec, x_spec, c_spec, c_spec, pl.BlockSpec((SSM_GROUP, LANES), lambda j: (0, 0))],
        out_specs=u_spec,
        out_shape=jax.ShapeDtypeStruct((SSM_T, n_sub, SSM_W), F32),
        scratch_shapes=[pltpu.VMEM((OCT_K, OCT_K), BF16), pltpu.VMEM((OCT_STATE, OCT_K), BF16),
                        pltpu.VMEM((OCT_STATE, OCT_K), BF16)],
        compiler_params=_cparams("arbitrary"),
        name="ssm_out",
    )(u3, taps, xs_re, xs_im, c_re, c_im, rep_group)


def _ssm_matrices(a_re, a_im, log_step, b_re, b_im, c_re, c_im):
    hp = lax.Precision.HIGHEST
    G, P, T, O = SSM_GROUPS, SSM_STATE, SSM_T, SSM_OCT
    step = jnp.exp(log_step.astype(F32))[:, None]
    mag = jnp.exp(a_re * step)
    ang = a_im * step
    ab_re = mag * jnp.cos(ang)
    ab_im = mag * jnp.sin(ang)
    den = a_re * a_re + a_im * a_im
    n_re = ab_re - 1.0
    f_re = (n_re * a_re + ab_im * a_im) / den
    f_im = (ab_im * a_re - n_re * a_im) / den
    bb_re = f_re[..., None] * b_re - f_im[..., None] * b_im
    bb_im = f_re[..., None] * b_im + f_im[..., None] * b_re
    pr, pi = [jnp.ones_like(ab_re)], [jnp.zeros_like(ab_re)]
    for _ in range(T):
        pr.append(pr[-1] * ab_re - pi[-1] * ab_im)
        pi.append(pr[-2] * ab_im + pi[-1] * ab_re)
    pw_re = jnp.stack(pr)
    pw_im = jnp.stack(pi)
    ab_r = pw_re[:T, :, :, None] * bb_re[None] - pw_im[:T, :, :, None] * bb_im[None]
    ab_i = pw_re[:T, :, :, None] * bb_im[None] + pw_im[:T, :, :, None] * bb_re[None]
    taps = (jnp.einsum('gcp,tgpd->tgcd', c_re, ab_r, precision=hp)
            - jnp.einsum('gcp,tgpd->tgcd', c_im, ab_i, precision=hp))

    def per_octet(x, perm, rows, cols):
        return jnp.transpose(x, perm).reshape(N_OCT, T, rows, cols).astype(BF16)

    tap_rows = per_octet(taps.reshape(T, N_OCT, O, SSM_GROUP, SSM_GROUP), (1, 0, 2, 4, 3), LANES, SSM_GROUP)

    def inject(m):
        return per_octet(m[::-1].reshape(T, N_OCT, O, P, SSM_GROUP), (1, 0, 2, 4, 3), LANES, P)

    def readout(ca):
        return per_octet(ca.reshape(T, N_OCT, O, SSM_GROUP, P), (1, 0, 2, 4, 3), OCT_STATE, SSM_GROUP)

    ca_re = c_re[None] * pw_re[1:][:, :, None, :] - c_im[None] * pw_im[1:][:, :, None, :]
    ca_im = c_re[None] * pw_im[1:][:, :, None, :] + c_im[None] * pw_re[1:][:, :, None, :]
    return dict(taps=tap_rows, m_re=inject(ab_r), m_im=inject(ab_i),
                c_re=readout(ca_re), c_im=readout(-ca_im),
                rep_group=jnp.tile(jnp.eye(SSM_GROUP, dtype=BF16), (1, O)),
                rep_state=jnp.tile(jnp.eye(P, dtype=BF16), (1, O)),
                at_re=pw_re[T], at_im=pw_im[T])


def _ssm(u3, mats, first, seq_id, h0_re, h0_im):
    sre, sim = _ssm_state(u3, mats['m_re'], mats['m_im'], mats['rep_state'])
    xsr, xsi, xer, xei = _ssm_carry(first, seq_id, sre, sim, mats['at_re'], mats['at_im'], h0_re, h0_im)
    y3 = _ssm_out(u3, mats['taps'], xsr, xsi, mats['c_re'], mats['c_im'], mats['rep_group'])
    return y3, xer, xei


def _states_at(xe, idx):
    return jnp.transpose(xe[:, idx, :], (1, 0, 2)).reshape(idx.shape[0], SSM_GROUPS, SSM_STATE)


def _gelu_tanh(x):
    return 0.5 * x * (1.0 + jnp.tanh(math.sqrt(2.0 / math.pi) * (x + 0.044715 * (x * x * x))))


def _glu_kernel(y3_ref, u_ref, d_ref, w_ref, o_ref, wbf_ref, scr_ref):
    @pl.when(pl.program_id(0) == 0)
    def _():
        wbf_ref[...] = w_ref[...].astype(BF16)

    rows = y3_ref.shape[1]
    n_col = SSM_W // LANES
    for t in range(SSM_T):
        for c in range(n_col):
            scr_ref[c, pl.ds(t, rows, stride=SSM_T), :] = y3_ref[t, :, c * LANES:(c + 1) * LANES]
    y = jnp.concatenate([scr_ref[c] for c in range(n_col)], axis=1)
    ys = _gelu_tanh(y + d_ref[...] * u_ref[...])
    t = jnp.dot(ys.astype(BF16), wbf_ref[...], preferred_element_type=F32)
    o_ref[...] = (ys * jax.nn.sigmoid(t)).astype(o_ref.dtype)


def _glu(y3, z, d_all, w_all, l):
    nt = z.shape[0]
    tm = 1024
    return pl.pallas_call(
        _glu_kernel,
        grid=(nt // tm,),
        in_specs=[
            pl.BlockSpec((SSM_T, tm // SSM_T, SSM_W), lambda i: (0, i, 0)),
            pl.BlockSpec((tm, SSM_W), lambda i: (i, 3 * ATT_W // SSM_W)),
            pl.BlockSpec((None, 1, SSM_W), lambda i: (l, 0, 0)),
            pl.BlockSpec((None, SSM_W, SSM_W), lambda i: (l, 0, 0)),
        ],
        out_specs=pl.BlockSpec((tm, SSM_W), lambda i: (i, 0)),
        out_shape=jax.ShapeDtypeStruct((nt, SSM_W), BF16),
        scratch_shapes=[pltpu.VMEM((SSM_W, SSM_W), BF16), pltpu.VMEM((SSM_W // LANES, tm, LANES), F32)],
        compiler_params=_cparams("arbitrary"),
        name="glu",
    )(y3, z, d_all.reshape(DEPTH, 1, SSM_W), w_all)


def _merge_kernel(o_ref, y_ref, ga_ref, gs_ref, wa_ref, ws_ref, m_ref):
    a = jnp.dot(o_ref[...], wa_ref[...].astype(BF16), preferred_element_type=F32)
    s = jnp.dot(y_ref[...], ws_ref[...].astype(BF16), preferred_element_type=F32)
    m_ref[...] = (jax.nn.sigmoid(ga_ref[...]) * a + jax.nn.sigmoid(gs_ref[...]) * s).astype(m_ref.dtype)


def _merge(o_p, o_s, ys2, z, w_ba_all, w_bs_all, l):
    nt = ys2.shape[0]
    tm, tn = 1024, 1024
    ga0 = (3 * ATT_W + SSM_W) // tn
    gs0 = ga0 + D_MODEL // tn
    p_tiles = o_p.shape[0] // tm
    s_tiles = o_s.shape[0] // tm
    assert p_tiles * tm == o_p.shape[0] and s_tiles * tm == o_s.shape[0]

    def kern(op_ref, os_ref, *rest):
        i = pl.program_id(0)

        @pl.when(i < p_tiles)
        def _():
            _merge_kernel(op_ref, *rest)

        @pl.when(i >= p_tiles)
        def _():
            _merge_kernel(os_ref, *rest)

    return pl.pallas_call(
        kern,
        grid=(nt // tm, D_MODEL // tn),
        in_specs=[
            pl.BlockSpec((tm, ATT_W), lambda i, j: (jnp.minimum(i, p_tiles - 1), 0)),
            pl.BlockSpec((tm, ATT_W), lambda i, j: (jnp.maximum(i - p_tiles, 0), 0)),
            pl.BlockSpec((tm, SSM_W), lambda i, j: (i, 0)),
            pl.BlockSpec((tm, tn), lambda i, j: (i, ga0 + j)),
            pl.BlockSpec((tm, tn), lambda i, j: (i, gs0 + j)),
            pl.BlockSpec((None, ATT_W, tn), lambda i, j: (l, 0, j)),
            pl.BlockSpec((None, SSM_W, tn), lambda i, j: (l, 0, j)),
        ],
        out_specs=pl.BlockSpec((tm, tn), lambda i, j: (i, j)),
        out_shape=jax.ShapeDtypeStruct((nt, D_MODEL), BF16),
        compiler_params=_cparams("arbitrary", "arbitrary"),
        name="merge",
    )(o_p, o_s, ys2, z, z, w_ba_all, w_bs_all)


def _proj_residual_kernel(a_ref, w_ref, x_ref, g_ref, o_ref):
    r = jnp.dot(a_ref[...], w_ref[...].astype(BF16), preferred_element_type=F32)
    o_ref[...] = x_ref[...] + _gated(r, g_ref)


def _proj_residual(a, w_all, x, tab, l, which_gate, x_tail=None):
    nt, k = a.shape
    tm, tn = 1024, 1024
    cpt = tm // CHUNK
    kern = _proj_residual_kernel
    x_specs = [pl.BlockSpec((tm, tn), lambda i, j: (i, j))]
    xs = [x]
    if x_tail is not None:
        assert x.shape[0] % tm == 0 and x_tail.shape[0] % tm == 0 and x.shape[0] + x_tail.shape[0] == nt
        head_tiles = x.shape[0] // tm
        kern = _split_rows(_proj_residual_kernel, 2, head_tiles)
        x_specs = _split_specs((tm, tn), head_tiles, True)
        xs = [x, x_tail]
    return pl.pallas_call(
        kern,
        grid=(nt // tm, D_MODEL // tn),
        in_specs=[
            pl.BlockSpec((tm, k), lambda i, j: (i, 0)),
            pl.BlockSpec((None, k, tn), lambda i, j: (l, 0, j)),
            *x_specs,
            _tab_spec(cpt, l, which_gate, width=tn, col=True),
        ],
        out_specs=pl.BlockSpec((tm, tn), lambda i, j: (i, j)),
        out_shape=jax.ShapeDtypeStruct((nt, D_MODEL), F32),
        compiler_params=_cparams("arbitrary", "arbitrary"),
        name="proj_residual",
    )(a, w_all, *xs, tab)


def _swiglu_step(h, wg_ref, wu_ref, wd_ref):
    hg = jnp.dot(h, wg_ref[...].astype(BF16), preferred_element_type=F32)
    hu = jnp.dot(h, wu_ref[...].astype(BF16), preferred_element_type=F32)
    a = (hg * jax.nn.sigmoid(hg)) * hu
    return jnp.dot(a.astype(BF16), wd_ref[...].astype(BF16), preferred_element_type=F32)


def _ffn_kernel(h_ref, wg_ref, wu_ref, wd_ref, o_ref):
    @pl.when(pl.program_id(1) == 0)
    def _():
        o_ref[...] = jnp.zeros_like(o_ref)

    o_ref[...] += _swiglu_step(h_ref[...], wg_ref, wu_ref, wd_ref)


def _ffn(h, w_gate, w_up, w_down, j):
    nt = h.shape[0]
    d_ff = w_gate.shape[-1]
    tm, tf = FFN_TM, FFN_TF
    return pl.pallas_call(
        _ffn_kernel,
        grid=(nt // tm, d_ff // tf),
        in_specs=[
            pl.BlockSpec((tm, D_MODEL), lambda i, f: (i, 0)),
            pl.BlockSpec((None, D_MODEL, tf), lambda i, f: (j, 0, f)),
            pl.BlockSpec((None, D_MODEL, tf), lambda i, f: (j, 0, f)),
            pl.BlockSpec((None, tf, D_MODEL), lambda i, f: (j, f, 0)),
        ],
        out_specs=pl.BlockSpec((tm, D_MODEL), lambda i, f: (i, 0)),
        out_shape=jax.ShapeDtypeStruct((nt, D_MODEL), F32),
        compiler_params=_cparams("arbitrary", "arbitrary"),
        name="ffn",
    )(h, w_gate, w_up, w_down)


def _moe_plan(route, nt):
    tm = MOE_TM
    n_assign = 2 * nt
    n_tiles = (n_assign + N_EXPERTS * (tm - 1)) // tm
    e_flat = jnp.concatenate([route[:, 0], route[:, 1]]).astype(jnp.int32)
    onehot = (e_flat[:, None] == jnp.arange(N_EXPERTS, dtype=jnp.int32)[None, :]).astype(jnp.int32)
    csum = jnp.cumsum(onehot, axis=0)
    counts = csum[-1]
    rank = jnp.sum((csum - onehot) * onehot, axis=1)
    tiles_per = (counts + tm - 1) // tm
    tile_end = jnp.cumsum(tiles_per)
    offsets = (tile_end - tiles_per) * tm
    dest = jnp.sum(onehot * offsets[None, :], axis=1) + rank
    tidx = jnp.arange(n_tiles, dtype=jnp.int32)
    tile_expert = jnp.minimum(jnp.sum((tidx[:, None] >= tile_end[None, :]).astype(jnp.int32), axis=1), N_EXPERTS - 1)
    tile_valid = (tidx < tile_end[-1]).astype(jnp.int32)
    tok = jnp.concatenate([jnp.arange(nt, dtype=jnp.int32)] * 2)
    src_token = jnp.zeros((n_tiles * tm,), jnp.int32).at[dest].set(tok)
    return dict(dest1=dest[:nt], dest2=dest[nt:], src_token=src_token,
                tile_expert=tile_expert.astype(jnp.int32), tile_valid=tile_valid)


def _row_copy(src_ref, row, dst_ref, r, sem):
    return pltpu.make_async_copy(src_ref.at[pl.ds(row, 1)], dst_ref.at[pl.ds(r, 1)], sem)


def _gather_start(idx_ref, base, src_ref, dst_ref, sem):
    rows = dst_ref.shape[0]

    def group(gi, c):
        for u in range(GATHER_UNROLL):
            r = gi * GATHER_UNROLL + u
            _row_copy(src_ref, idx_ref[base + r], dst_ref, r, sem).start(priority=u % 2)
        return c

    lax.fori_loop(0, rows // GATHER_UNROLL, group, 0)


def _gather_wait(src_ref, dst_ref, sem):
    pltpu.make_async_copy(src_ref.at[pl.ds(0, dst_ref.shape[0])], dst_ref, sem).wait()


def _gather_rows_kernel(idx_ref, live_ref, src_ref, o_ref, buf_ref, sem):
    i = pl.program_id(0)
    n = pl.num_programs(0)
    slot = i % 2
    rows = buf_ref.shape[1]

    @pl.when((i == 0) & (live_ref[0] == 1))
    def _():
        _gather_start(idx_ref, 0, src_ref, buf_ref.at[0], sem.at[0])

    @pl.when((i + 1 < n) & (live_ref[jnp.minimum(i + 1, n - 1)] == 1))
    def _():
        _gather_start(idx_ref, (i + 1) * rows, src_ref, buf_ref.at[1 - slot], sem.at[1 - slot])

    @pl.when(live_ref[i] == 1)
    def _():
        _gather_wait(src_ref, buf_ref.at[slot], sem.at[slot])
        o_ref[...] = buf_ref[slot].astype(o_ref.dtype)

    @pl.when(live_ref[i] == 0)
    def _():
        o_ref[...] = jnp.zeros_like(o_ref)


def _gather_rows(src, idx, live, out_dtype):
    n_out = idx.shape[0]
    width = src.shape[1]
    rows = MOE_TM
    assert n_out == live.shape[0] * rows and rows % GATHER_UNROLL == 0
    return pl.pallas_call(
        _gather_rows_kernel,
        grid_spec=pltpu.PrefetchScalarGridSpec(
            num_scalar_prefetch=2,
            grid=(n_out // rows,),
            in_specs=[pl.BlockSpec(memory_space=pl.ANY)],
            out_specs=pl.BlockSpec((rows, width), lambda i, *_: (i, 0)),
            scratch_shapes=[pltpu.VMEM((2, rows, width), src.dtype), pltpu.SemaphoreType.DMA((2,))],
        ),
        out_shape=jax.ShapeDtypeStruct((n_out, width), out_dtype),
        compiler_params=_cparams("arbitrary"),
        name="moe_gather",
    )(idx, live, src)


def _moe_ffn_kernel(te_ref, tv_ref, h_ref, wg_ref, wu_ref, wd_ref, o_ref):
    t = pl.program_id(0)

    @pl.when(pl.program_id(1) == 0)
    def _():
        o_ref[...] = jnp.zeros_like(o_ref)

    @pl.when(tv_ref[t] == 1)
    def _():
        o_ref[...] += _swiglu_step(h_ref[...], wg_ref, wu_ref, wd_ref)


def _moe_ffn(h_sorted, tile_expert, tile_valid, w_gate, w_up, w_down, j):
    d_ff = w_gate.shape[-1]
    tm, tf = MOE_TM, FFN_TF
    n_tiles = h_sorted.shape[0] // tm
    nf = d_ff // tf

    def f_idx(t, f, tv):
        return jnp.where(tv[t] == 1, f, nf - 1)

    return pl.pallas_call(
        _moe_ffn_kernel,
        grid_spec=pltpu.PrefetchScalarGridSpec(
            num_scalar_prefetch=2,
            grid=(n_tiles, nf),
            in_specs=[
                pl.BlockSpec((tm, D_MODEL), lambda t, f, te, tv: (t, 0)),
                pl.BlockSpec((None, None, D_MODEL, tf), lambda t, f, te, tv: (j, te[t], 0, f_idx(t, f, tv))),
                pl.BlockSpec((None, None, D_MODEL, tf), lambda t, f, te, tv: (j, te[t], 0, f_idx(t, f, tv))),
                pl.BlockSpec((None, None, tf, D_MODEL), lambda t, f, te, tv: (j, te[t], f_idx(t, f, tv), 0)),
            ],
            out_specs=pl.BlockSpec((tm, D_MODEL), lambda t, f, te, tv: (t, 0)),
        ),
        out_shape=jax.ShapeDtypeStruct((n_tiles * tm, D_MODEL), F32),
        compiler_params=_cparams("arbitrary", "arbitrary"),
        name="moe_ffn",
    )(tile_expert, tile_valid, h_sorted, w_gate, w_up, w_down)


def _moe_combine_kernel(d1_ref, d2_ref, y_ref, x_ref, route_ref, g_ref, gn_ref, *rest, final_norm, split_tiles):
    n_out = 1 if split_tiles is None else 2
    o_refs, (a_ref, b_ref, sem_a, sem_b) = rest[:n_out], rest[n_out:]
    i = pl.program_id(0)
    n = pl.num_programs(0)
    slot = i % 2

    def start(step, s):
        _gather_start(d1_ref, step * GATHER_ROWS, y_ref, a_ref.at[s], sem_a.at[s])
        _gather_start(d2_ref, step * GATHER_ROWS, y_ref, b_ref.at[s], sem_b.at[s])

    @pl.when(i == 0)
    def _():
        start(0, 0)

    @pl.when(i + 1 < n)
    def _():
        start(i + 1, 1 - slot)

    _gather_wait(y_ref, a_ref.at[slot], sem_a.at[slot])
    _gather_wait(y_ref, b_ref.at[slot], sem_b.at[slot])
    route = route_ref[...]
    r = route[:, 2:3] * a_ref[slot] + route[:, 3:4] * b_ref[slot]
    xn = x_ref[...] + _gated(r, g_ref)
    if final_norm:
        xn = xn * lax.rsqrt(jnp.mean(xn * xn, axis=-1, keepdims=True) + EPS) * gn_ref[...]
    if split_tiles is None:
        o_refs[0][...] = xn
    else:
        @pl.when(i < split_tiles)
        def _():
            o_refs[0][...] = xn

        @pl.when(i >= split_tiles)
        def _():
            o_refs[1][...] = xn


def _moe_combine(y_sorted, dest1, dest2, x, route, tab, l, which_gate, g_final=None, split_rows=None):
    final_norm = g_final is not None
    gn = (g_final if final_norm else jnp.ones((D_MODEL,), F32)).reshape(1, D_MODEL)
    nt = x.shape[0]
    tm = GATHER_ROWS
    blk = pl.BlockSpec((tm, D_MODEL), lambda i, *_: (i, 0))
    buf = pltpu.VMEM((2, tm, D_MODEL), F32)
    if split_rows is None:
        split_tiles = None
        out_specs = blk
        out_shape = jax.ShapeDtypeStruct((nt, D_MODEL), F32)
    else:
        split_tiles = split_rows // tm
        assert split_tiles * tm == split_rows and 0 < split_rows < nt
        out_specs = [pl.BlockSpec((tm, D_MODEL), lambda i, *_: (jnp.minimum(i, split_tiles - 1), 0)),
                     pl.BlockSpec((tm, D_MODEL), lambda i, *_: (jnp.maximum(i - split_tiles, 0), 0))]
        out_shape = [jax.ShapeDtypeStruct((split_rows, D_MODEL), F32),
                     jax.ShapeDtypeStruct((nt - split_rows, D_MODEL), F32)]
    return pl.pallas_call(
        functools.partial(_moe_combine_kernel, final_norm=final_norm, split_tiles=split_tiles),
        grid_spec=pltpu.PrefetchScalarGridSpec(
            num_scalar_prefetch=2,
            grid=(nt // tm,),
            in_specs=[pl.BlockSpec(memory_space=pl.ANY), blk, pl.BlockSpec((tm, LANES), lambda i, *_: (i, 0)),
                      _tab_spec(tm // CHUNK, l, which_gate), pl.BlockSpec((1, D_MODEL), lambda i, *_: (0, 0))],
            out_specs=out_specs,
            scratch_shapes=[buf, buf, pltpu.SemaphoreType.DMA((2,)), pltpu.SemaphoreType.DMA((2,))],
        ),
        out_shape=out_shape,
        compiler_params=_cparams("arbitrary"),
        name="moe_combine",
    )(dest1, dest2, y_sorted, x, route, tab, gn)


def _residual_kernel(x_ref, r_ref, g_ref, o_ref):
    o_ref[...] = x_ref[...] + _gated(r_ref[...], g_ref)


def _residual(x, r, tab, l, which_gate):
    nt = x.shape[0]
    tm = 512
    blk = pl.BlockSpec((tm, D_MODEL), lambda i: (i, 0))
    return pl.pallas_call(
        _residual_kernel, grid=(nt // tm,),
        in_specs=[blk, blk, _tab_spec(tm // CHUNK, l, which_gate)],
        out_specs=blk, out_shape=jax.ShapeDtypeStruct((nt, D_MODEL), F32),
        compiler_params=_cparams("arbitrary"), name="residual",
    )(x, r, tab)


def _residual_norm_kernel(x_ref, r_ref, gate_ref, g_ref, sh_ref, sc_ref, xo_ref, h_ref):
    xn = x_ref[...] + _gated(r_ref[...], gate_ref)
    xo_ref[...] = xn
    h_ref[...] = _modulate(xn, g_ref, sh_ref, sc_ref).astype(BF16)


def _residual_norm(x, r, tab, l, which_gate, g_all, l_next, which_shift, which_scale):
    nt = x.shape[0]
    tm = 512
    cpt = tm // CHUNK
    blk = pl.BlockSpec((tm, D_MODEL), lambda i: (i, 0))
    return pl.pallas_call(
        _residual_norm_kernel, grid=(nt // tm,),
        in_specs=[blk, blk, _tab_spec(cpt, l, which_gate),
                  pl.BlockSpec((None, 1, D_MODEL), lambda i: (l_next, 0, 0)),
                  _tab_spec(cpt, l_next, which_shift), _tab_spec(cpt, l_next, which_scale)],
        out_specs=[blk, blk],
        out_shape=[jax.ShapeDtypeStruct((nt, D_MODEL), F32), jax.ShapeDtypeStruct((nt, D_MODEL), BF16)],
        compiler_params=_cparams("arbitrary"), name="residual_norm",
    )(x, r, tab, g_all.reshape(DEPTH, 1, D_MODEL), tab, tab)


def _kv_rows_kernel(z_ref, o_ref):
    for h in range(N_HEADS):
        o_ref[:, h, :] = z_ref[:, h * HEAD_DIM:(h + 1) * HEAD_DIM]


def _kv_rows(z, n_pb, seq, n_sb):
    rows = PAST_WINDOW
    per_seq = seq // rows
    s_blocks = n_sb * CHUNK // rows
    assert per_seq * rows == seq and s_blocks * rows == n_sb * CHUNK

    def row_block(r):
        return jnp.where(r < n_pb, (r + 1) * per_seq - 1, n_pb * per_seq + (r - n_pb))

    n_blocks = n_pb + s_blocks
    return pl.pallas_call(
        _kv_rows_kernel, grid=(2, n_blocks),
        in_specs=[pl.BlockSpec((rows, ATT_W), lambda w, r: (row_block(r), 1 + w))],
        out_specs=pl.BlockSpec((None, rows, N_HEADS, HEAD_DIM), lambda w, r: (w, r, 0, 0)),
        out_shape=jax.ShapeDtypeStruct((2, n_blocks * rows, N_HEADS, HEAD_DIM), F32),
        compiler_params=_cparams("arbitrary", "arbitrary"), name="kv_rows",
    )(z)


def _final_norm_kernel(x_ref, g_ref, o_ref):
    x = x_ref[...]
    o_ref[...] = x * lax.rsqrt(jnp.mean(x * x, axis=-1, keepdims=True) + EPS) * g_ref[...]


def _final_norm(x, g):
    nt = x.shape[0]
    tm = 512
    blk = pl.BlockSpec((tm, D_MODEL), lambda i: (i, 0))
    return pl.pallas_call(
        _final_norm_kernel, grid=(nt // tm,),
        in_specs=[blk, pl.BlockSpec((1, D_MODEL), lambda i: (0, 0))],
        out_specs=blk, out_shape=jax.ShapeDtypeStruct((nt, D_MODEL), F32),
        compiler_params=_cparams("arbitrary"), name="final_norm",
    )(x, g.reshape(1, D_MODEL))


def kernel(x_prompt, x_sample, cache_k, cache_v, state_ssm_re, state_ssm_im, c_prompt, c_sample, w_ada, b_ada, g_norm_mix, g_norm_ffn, g_norm_final, w_in, rel_bias, ssm_a_re, ssm_a_im, ssm_log_step, ssm_b_re, ssm_b_im, ssm_c_re, ssm_c_im, ssm_d, w_ssm_glu, w_branch_attn, w_branch_ssm, w_out, w_ffn_gate, w_ffn_up, w_ffn_down, w_router, b_router, w_exp_gate, w_exp_up, w_exp_down):
    n_pb, seq = x_prompt.shape[0], x_prompt.shape[1]
    n_sb, dec = x_sample.shape[0], x_sample.shape[1]
    assert dec == CHUNK and cache_k.shape[2] == PAST_WINDOW and seq % ATT_ROWS == 0
    np_tok = n_pb * seq
    nt = np_tok + n_sb * dec
    n_batch = n_pb + n_sb

    x_head = x_prompt.reshape(np_tok, D_MODEL)
    x_tail = x_sample.reshape(n_sb * dec, D_MODEL)

    c_all = jnp.concatenate([c_prompt, c_sample], axis=0)
    c_pad = jnp.pad(c_all, ((0, -n_batch % 8), (0, 0)))
    ada = _ada(c_pad, w_ada, b_ada)
    ada_p = jnp.broadcast_to(ada[:, :n_pb, None, :], (DEPTH, n_pb, seq // CHUNK, 6 * D_MODEL))
    tab = jnp.concatenate([ada_p.reshape(DEPTH, np_tok // CHUNK, 6 * D_MODEL), ada[:, n_pb:n_batch]], axis=1)

    sub_p, sub_s = seq // SSM_T, dec // SSM_T
    seq_id = jnp.concatenate([jnp.repeat(jnp.arange(n_pb), sub_p), n_pb + jnp.repeat(jnp.arange(n_sb), sub_s)])
    seq_id = seq_id.astype(jnp.int32)
    first = jnp.concatenate([jnp.arange(n_pb * sub_p) % sub_p == 0, jnp.arange(n_sb * sub_s) % sub_s == 0])
    first = first.astype(jnp.int32)
    last_p = (jnp.arange(n_pb) + 1) * sub_p - 1
    last_s = n_pb * sub_p + (jnp.arange(n_sb) + 1) * sub_s - 1
    zeros_h0 = jnp.zeros((n_pb, SSM_GROUPS, SSM_STATE), F32)

    outs = {k: [] for k in ('kp', 'vp', 'srp', 'sip', 'ks', 'vs', 'srs', 'sis')}
    keep = min(PAST_WINDOW, seq)
    h = _norm_mod(x_head, g_norm_mix, tab, 0, SHIFT1, SCALE1, x_tail=x_tail)
    for l in range(DEPTH):
        last = l == DEPTH - 1
        z, u3 = _in_proj(h, w_in, l)

        b64 = _attention_bias(rel_bias[l])
        o_p = _attn_prompt(z, b64, n_pb, seq)
        o_s = _attn_sample(z, cache_k, cache_v, l, b64, np_tok, n_sb)

        mats = _ssm_matrices(ssm_a_re[l], ssm_a_im[l], ssm_log_step[l], ssm_b_re[l], ssm_b_im[l],
                             ssm_c_re[l], ssm_c_im[l])
        h0_re = jnp.concatenate([zeros_h0, state_ssm_re[l].astype(F32)], axis=0)
        h0_im = jnp.concatenate([zeros_h0, state_ssm_im[l].astype(F32)], axis=0)
        y3, xe_re, xe_im = _ssm(u3, mats, first, seq_id, h0_re, h0_im)

        ys2 = _glu(y3, z, ssm_d, w_ssm_glu, l)
        merged = _merge(o_p, o_s, ys2, z, w_branch_attn, w_branch_ssm, l)
        if l == 0:
            x = _proj_residual(merged, w_out, x_head, tab, l, GATE1, x_tail=x_tail)
        else:
            x = _proj_residual(merged, w_out, x, tab, l, GATE1)

        j = l // 2
        if l % 2 == 0:
            h2 = _norm_mod(x, g_norm_ffn, tab, l, SHIFT2, SCALE2)
            r = _ffn(h2, w_ffn_gate, w_ffn_up, w_ffn_down, j)
            if last:
                yn = _final_norm(_residual(x, r, tab, l, GATE2), g_norm_final)
                yn_p, yn_s = yn[:np_tok], yn[np_tok:]
            else:
                x, h = _residual_norm(x, r, tab, l, GATE2, g_norm_mix, l + 1, SHIFT1, SCALE1)
        else:
            h2, route = _norm_mod(x, g_norm_ffn, tab, l, SHIFT2, SCALE2, router=(w_router[j], b_router[j]))
            plan = _moe_plan(route, nt)
            h_sorted = _gather_rows(h2, plan['src_token'], plan['tile_valid'], BF16)
            y_sorted = _moe_ffn(h_sorted, plan['tile_expert'], plan['tile_valid'], w_exp_gate, w_exp_up, w_exp_down, j)
            if last:
                yn_p, yn_s = _moe_combine(y_sorted, plan['dest1'], plan['dest2'], x, route, tab, l, GATE2,
                                          g_final=g_norm_final, split_rows=np_tok)
            else:
                x = _moe_combine(y_sorted, plan['dest1'], plan['dest2'], x, route, tab, l, GATE2)
                h = _norm_mod(x, g_norm_mix, tab, l + 1, SHIFT1, SCALE1)

        kv = _kv_rows(z, n_pb, seq, n_sb)
        outs['kp'].append(kv[0, :n_pb * keep].reshape(n_pb, keep, N_HEADS, HEAD_DIM))
        outs['vp'].append(kv[1, :n_pb * keep].reshape(n_pb, keep, N_HEADS, HEAD_DIM))
        outs['ks'].append(kv[0, n_pb * keep:].reshape(n_sb, dec, N_HEADS, HEAD_DIM))
        outs['vs'].append(kv[1, n_pb * keep:].reshape(n_sb, dec, N_HEADS, HEAD_DIM))
        outs['srp'].append(_states_at(xe_re, last_p))
        outs['sip'].append(_states_at(xe_im, last_p))
        outs['srs'].append(_states_at(xe_re, last_s))
        outs['sis'].append(_states_at(xe_im, last_s))

    y_prompt = yn_p.reshape(n_pb, seq, D_MODEL)
    y_sample = yn_s.reshape(n_sb, dec, D_MODEL)
    st = lambda name: jnp.stack(outs[name])
    return (y_prompt, y_sample, st('kp'), st('vp'), st('srp'), st('sip'),
            st('ks'), st('vs'), st('srs'), st('sis'))
```
